```python
import math
import jax, jax.numpy as jnp
from jax import lax
import numpy as np

D_MODEL = 1024
BATCH = 8
SEQ = 2048
DEPTH = 2

CTX_LEN = 256
GRID_W = 64
EPS = 1e-6
ROPE_BASE = 10000.0

D_MIX = D_MODEL
N_MIXERS = 4
GROUP_W = D_MIX // N_MIXERS
HEAD_DIM = 64
CHUNK = 64

RET_HEADS = GROUP_W // HEAD_DIM
RET_DK = HEAD_DIM
GLA_HEADS = 4
GLA_DV = GROUP_W // GLA_HEADS
GLA_DK = GLA_DV // 2
GLA_QK = GLA_HEADS * GLA_DK
GLA_RANK = 16
GLA_TAU = 16.0
GLA_CHUNK = 16
SSD_HEADS = GROUP_W // HEAD_DIM
SSD_GROUPS = 2
SSD_STATE = 64
SSD_CONV = 5
SSD_BC = SSD_GROUPS * SSD_STATE
SSD_CONV_CH = GROUP_W + 2 * SSD_BC
MLSTM_HEADS = 4
MLSTM_DH = GROUP_W // MLSTM_HEADS
N_EXPERTS = 16
EC_CAPACITY = 2
EXPERT_FF = 1536

RET_COLS = 4 * GROUP_W
GLA_COLS = 2 * GLA_QK + 2 * GROUP_W + GLA_RANK
SSD_COLS = GROUP_W + SSD_CONV_CH + 2 * SSD_HEADS
MLSTM_COLS = 4 * GROUP_W + 4 * MLSTM_HEADS
IN_COLS = RET_COLS + GLA_COLS + SSD_COLS + MLSTM_COLS

kernel_name = 'hybrid_bidir_ret_gla_ssd_mlstm_ec'

F32 = jnp.float32


def rms_norm(x, w):
    xf = x.astype(F32)
    y = xf * lax.rsqrt(jnp.mean(jnp.square(xf), axis=-1, keepdims=True) + EPS)
    return (y * w).astype(x.dtype)


def modulate(x, w, shift, scale):
    return rms_norm(x, w) * (1 + scale) + shift


def to_heads(t, n_heads):
    b, n, _ = t.shape
    return t.reshape(b, n, n_heads, -1).transpose(0, 2, 1, 3)


def from_heads(t):
    b, h, n, d = t.shape
    return t.transpose(0, 2, 1, 3).reshape(b, n, h * d)


def head_norm(y, w, center):
    if center:
        y = y - jnp.mean(y, axis=-1, keepdims=True)
    y = y * lax.rsqrt(jnp.mean(jnp.square(y), axis=-1, keepdims=True) + EPS)
    return from_heads(y) * w


def _rotate(t, pos):
    nf = t.shape[-1] // 2
    inv = ROPE_BASE ** (-jnp.arange(nf, dtype=F32) / nf)
    ang = pos[:, None] * inv[None, :]
    cos, sin = jnp.cos(ang).astype(t.dtype), jnp.sin(ang).astype(t.dtype)
    t1, t2 = t[..., :nf], t[..., nf:]
    return jnp.concatenate([t1 * cos - t2 * sin, t1 * sin + t2 * cos], axis=-1)


def axial_rope(t, rows, cols):
    half = t.shape[-1] // 2
    return jnp.concatenate([_rotate(t[..., :half], rows), _rotate(t[..., half:], cols)], axis=-1)


def depthwise_conv(x, w, b):
    pad = (w.shape[0] - 1) // 2
    y = lax.conv_general_dilated(x, w[:, None, :], window_strides=(1,), padding=[(pad, pad)],
                                 dimension_numbers=('NWC', 'WIO', 'NWC'),
                                 feature_group_count=x.shape[-1])
    return y + b


def _causal_mask(n):
    return jnp.tril(jnp.ones((n, n), dtype=bool))


def scalar_decay_scan(q, k, v, log_a, state, with_output):
    b, h, n_tok, dk = q.shape
    dv = v.shape[-1]
    n = n_tok // CHUNK
    qc = q.astype(F32).reshape(b, h, n, CHUNK, dk)
    kc = k.astype(F32).reshape(b, h, n, CHUNK, dk)
    vc = v.astype(F32).reshape(b, h, n, CHUNK, dv)
    g = jnp.cumsum(log_a.astype(F32).reshape(b, h, n, CHUNK), axis=-1)
    g_last = g[..., -1]
    kv = jnp.einsum('bhncd,bhnce->bhnde', kc * jnp.exp(g_last[..., None] - g)[..., None], vc)

    def step(s, inp):
        a, kv_c = inp
        return a[..., None, None] * s + kv_c, s

    s_fin, s_in = lax.scan(step, state, (jnp.moveaxis(jnp.exp(g_last), 2, 0), jnp.moveaxis(kv, 2, 0)))
    if not with_output:
        return None, s_fin
    s_in = jnp.moveaxis(s_in, 0, 2)
    decay = jnp.exp(jnp.where(_causal_mask(CHUNK), g[..., :, None] - g[..., None, :], -jnp.inf))
    scores = jnp.einsum('bhnid,bhnjd->bhnij', qc, kc) * decay
    o = (jnp.einsum('bhnij,bhnje->bhnie', scores, vc)
         + jnp.einsum('bhnid,bhnde->bhnie', qc * jnp.exp(g)[..., None], s_in))
    return o.reshape(b, h, n_tok, dv), s_fin


def gla_scan(q, k, v, log_a, state, with_output):
    b, h, n_tok, dk = q.shape
    dv = v.shape[-1]
    n = n_tok // GLA_CHUNK
    qc = q.astype(F32).reshape(b, h, n, GLA_CHUNK, dk)
    kc = k.astype(F32).reshape(b, h, n, GLA_CHUNK, dk)
    vc = v.astype(F32).reshape(b, h, n, GLA_CHUNK, dv)
    g = jnp.cumsum(log_a.astype(F32).reshape(b, h, n, GLA_CHUNK, dk), axis=3)
    g_last = g[:, :, :, -1]
    kv = jnp.einsum('bhncd,bhnce->bhnde', kc * jnp.exp(g_last[:, :, :, None] - g), vc)

    def step(s, inp):
        a, kv_c = inp
        return a[..., None] * s + kv_c, s

    s_fin, s_in = lax.scan(step, state, (jnp.moveaxis(jnp.exp(g_last), 2, 0), jnp.moveaxis(kv, 2, 0)))
    if not with_output:
        return None, s_fin
    s_in = jnp.moveaxis(s_in, 0, 2)
    diff = g[:, :, :, :, None, :] - g[:, :, :, None, :, :]
    decay = jnp.exp(jnp.where(_causal_mask(GLA_CHUNK)[:, :, None], diff, -jnp.inf))
    scores = jnp.einsum('bhnid,bhnjd,bhnijd->bhnij', qc, kc, decay)
    o = (jnp.einsum('bhnij,bhnje->bhnie', scores, vc)
         + jnp.einsum('bhnid,bhnde->bhnie', qc * jnp.exp(g), s_in))
    return o.reshape(b, h, n_tok, dv), s_fin


def mlstm_scan(q, k, v, log_f, i_pre, state, with_output):
    b, h, n_tok, dk = q.shape
    dv = v.shape[-1]
    n = n_tok // CHUNK
    qc = q.astype(F32).reshape(b, h, n, CHUNK, dk)
    kc = k.astype(F32).reshape(b, h, n, CHUNK, dk)
    vc = v.astype(F32).reshape(b, h, n, CHUNK, dv)
    g = jnp.cumsum(log_f.astype(F32).reshape(b, h, n, CHUNK), axis=-1)
    ig = i_pre.astype(F32).reshape(b, h, n, CHUNK)
    g_last = g[..., -1]
    w_end = g_last[..., None] - g + ig
    b_max = jnp.max(w_end, axis=-1)
    e = jnp.exp(w_end - b_max[..., None])
    kv = jnp.einsum('bhnc,bhncd,bhnce->bhnde', e, kc, vc)
    ksum = jnp.einsum('bhnc,bhncd->bhnd', e, kc)

    def step(carry, inp):
        c_s, n_s, m_s = carry
        a, b_c, kv_c, k_c = inp
        m_new = jnp.maximum(a + m_s, b_c)
        old = jnp.exp(a + m_s - m_new)
        new = jnp.exp(b_c - m_new)
        return (old[..., None, None] * c_s + new[..., None, None] * kv_c,
                old[..., None] * n_s + new[..., None] * k_c, m_new), carry

    mv = lambda t: jnp.moveaxis(t, 2, 0)
    final, (c_in, n_in, m_in) = lax.scan(step, state, (mv(g_last), mv(b_max), mv(kv), mv(ksum)))
    if not with_output:
        return None, final
    c_in, n_in, m_in = [jnp.moveaxis(t, 0, 2) for t in (c_in, n_in, m_in)]
    d_log = jnp.where(_causal_mask(CHUNK), g[..., :, None] - g[..., None, :] + ig[..., None, :], -jnp.inf)
    m_inter = g + m_in[..., None]
    m_tot = jnp.maximum(m_inter, jnp.max(d_log, axis=-1))
    qk = jnp.einsum('bhnid,bhnjd->bhnij', qc, kc) * jnp.exp(d_log - m_tot[..., None])
    s_inter = jnp.exp(m_inter - m_tot)
    num = (jnp.einsum('bhnij,bhnje->bhnie', qk, vc)
           + s_inter[..., None] * jnp.einsum('bhnid,bhnde->bhnie', qc, c_in))
    den = jnp.sum(qk, axis=-1) + s_inter * jnp.einsum('bhnid,bhnd->bhni', qc, n_in)
    hout = num / jnp.maximum(jnp.abs(den), jnp.exp(-m_tot))[..., None]
    return hout.reshape(b, h, n_tok, dv), final


def _seq_flip(t, reverse):
    return jnp.flip(t, axis=2) if reverse else t


def bidirectional(scan_fn, ctx_dirs, lat_dirs, init_state, ctx_out):
    y_ctx, y_lat = None, None
    for d in range(2):
        rev = d == 1
        o_c, s_c = scan_fn(*[_seq_flip(t, rev) for t in ctx_dirs[d]], init_state, ctx_out)
        o_l, _ = scan_fn(*[_seq_flip(t, rev) for t in lat_dirs[d]], s_c, True)
        o_l = _seq_flip(o_l, rev)
        y_lat = o_l if y_lat is None else y_lat + o_l
        if ctx_out:
            o_c = _seq_flip(o_c, rev)
            y_ctx = o_c if y_ctx is None else y_ctx + o_c
    return y_ctx, y_lat


def retention_mixer(u_ctx, u_lat, decay_logit, norm_w, rows, cols, ctx_out):
    log_gamma = jax.nn.log_sigmoid(decay_logit.astype(F32))

    def prep(u, rope):
        q, k, v, g = jnp.split(u, 4, axis=-1)
        q = to_heads(q, RET_HEADS) * RET_DK ** -0.5
        k = to_heads(k, RET_HEADS)
        if rope:
            q, k = axial_rope(q, rows, cols), axial_rope(k, rows, cols)
        v = to_heads(v, RET_HEADS)
        b, h, n, _ = q.shape
        dirs = [(q, k, v, jnp.broadcast_to(log_gamma[d][None, :, None], (b, h, n))) for d in range(2)]
        return dirs, g

    ctx_dirs, g_ctx = prep(u_ctx, False)
    lat_dirs, g_lat = prep(u_lat, True)
    s0 = jnp.zeros((u_lat.shape[0], RET_HEADS, RET_DK, RET_DK), F32)
    o_ctx, o_lat = bidirectional(scalar_decay_scan, ctx_dirs, lat_dirs, s0, ctx_out)
    out = lambda o, g: (head_norm(o, norm_w, True) * jax.nn.silu(g.astype(F32))).astype(g.dtype)
    return (out(o_ctx, g_ctx) if ctx_out else None), out(o_lat, g_lat)


def gla_mixer(u_ctx, u_lat, gate_w, gate_b, norm_w, ctx_out):
    cuts = [GLA_QK, 2 * GLA_QK, 2 * GLA_QK + GROUP_W, 2 * GLA_QK + 2 * GROUP_W]

    def prep(u):
        q, k, v, r, lr = jnp.split(u, cuts, axis=-1)
        q = to_heads(q, GLA_HEADS) * GLA_DK ** -0.5
        k = to_heads(k, GLA_HEADS)
        v = to_heads(v, GLA_HEADS)
        dirs = []
        for d in range(2):
            log_a = jax.nn.log_sigmoid((lr @ gate_w[d] + gate_b[d]).astype(F32)) / GLA_TAU
            dirs.append((q, k, v, to_heads(log_a, GLA_HEADS)))
        return dirs, r

    ctx_dirs, r_ctx = prep(u_ctx)
    lat_dirs, r_lat = prep(u_lat)
    s0 = jnp.zeros((u_lat.shape[0], GLA_HEADS, GLA_DK, GLA_DV), F32)
    o_ctx, o_lat = bidirectional(gla_scan, ctx_dirs, lat_dirs, s0, ctx_out)
    out = lambda o, r: (head_norm(o, norm_w, False) * jax.nn.silu(r.astype(F32))).astype(r.dtype)
    return (out(o_ctx, r_ctx) if ctx_out else None), out(o_lat, r_lat)


def ssd_mixer(u_ctx, u_lat, conv_w, conv_b, dt_bias, a_log, d_skip, norm_w, ctx_out):
    a_neg = -jnp.exp(a_log.astype(F32))
    rep = SSD_HEADS // SSD_GROUPS

    def prep(u):
        z, xbc, dt = jnp.split(u, [GROUP_W, GROUP_W + SSD_CONV_CH], axis=-1)
        xbc = jax.nn.silu(depthwise_conv(xbc, conv_w, conv_b))
        xs, bm, cm = jnp.split(xbc, [GROUP_W, GROUP_W + SSD_BC], axis=-1)
        xs = to_heads(xs, SSD_HEADS)
        bm = jnp.repeat(to_heads(bm, SSD_GROUPS), rep, axis=1)
        cm = jnp.repeat(to_heads(cm, SSD_GROUPS), rep, axis=1)
        dirs = []
        for d in range(2):
            dt_d = jax.nn.softplus(dt[..., d * SSD_HEADS:(d + 1) * SSD_HEADS].astype(F32) + dt_bias[d])
            dt_d = dt_d.transpose(0, 2, 1)
            dirs.append((cm, bm, xs * dt_d[..., None], dt_d * a_neg[d][:, None]))
        return dirs, xs, z

    ctx_dirs, x_ctx, z_ctx = prep(u_ctx)
    lat_dirs, x_lat, z_lat = prep(u_lat)
    s0 = jnp.zeros((u_lat.shape[0], SSD_HEADS, SSD_STATE, HEAD_DIM), F32)
    o_ctx, o_lat = bidirectional(scalar_decay_scan, ctx_dirs, lat_dirs, s0, ctx_out)

    def out(o, xs, z):
        y = from_heads(o + d_skip[:, None, None] * xs)
        return rms_norm(y * jax.nn.silu(z.astype(F32)), norm_w).astype(z.dtype)

    return (out(o_ctx, x_ctx, z_ctx) if ctx_out else None), out(o_lat, x_lat, z_lat)


def mlstm_mixer(u_ctx, u_lat, gate_b, norm_w, ctx_out):
    def prep(u):
        q, k, v, o, gates = jnp.split(u, [GROUP_W, 2 * GROUP_W, 3 * GROUP_W, 4 * GROUP_W], axis=-1)
        q = to_heads(q, MLSTM_HEADS)
        k = to_heads(k, MLSTM_HEADS) * MLSTM_DH ** -0.5
        v = to_heads(v, MLSTM_HEADS)
        b, n, _ = gates.shape
        gates = gates.astype(F32).reshape(b, n, 2, 2, MLSTM_HEADS) + gate_b
        dirs = [(q, k, v, jax.nn.log_sigmoid(gates[:, :, d, 1]).transpose(0, 2, 1),
                 gates[:, :, d, 0].transpose(0, 2, 1)) for d in range(2)]
        return dirs, o

    ctx_dirs, o_ctx_gate = prep(u_ctx)
    lat_dirs, o_lat_gate = prep(u_lat)
    b = u_lat.shape[0]
    s0 = (jnp.zeros((b, MLSTM_HEADS, MLSTM_DH, MLSTM_DH), F32),
          jnp.zeros((b, MLSTM_HEADS, MLSTM_DH), F32),
          jnp.zeros((b, MLSTM_HEADS), F32))
    h_ctx, h_lat = bidirectional(mlstm_scan, ctx_dirs, lat_dirs, s0, ctx_out)
    out = lambda h, og: (jax.nn.sigmoid(og.astype(F32)) * head_norm(h, norm_w, False)).astype(og.dtype)
    return (out(h_ctx, o_ctx_gate) if ctx_out else None), out(h_lat, o_lat_gate)


def parallel_mixers(u_ctx, u_lat, ret_decay_logit, ret_norm_w, gla_gate_w, gla_gate_b, gla_norm_w,
                    ssd_conv_w, ssd_conv_b, ssd_dt_bias, ssd_a_log, ssd_d, ssd_norm_w,
                    mlstm_gate_b, mlstm_norm_w, rows, cols, ctx_out):
    cuts = [RET_COLS, RET_COLS + GLA_COLS, RET_COLS + GLA_COLS + SSD_COLS]
    a_c, b_c, s_c, m_c = jnp.split(u_ctx, cuts, axis=-1)
    a_l, b_l, s_l, m_l = jnp.split(u_lat, cuts, axis=-1)
    outs = [retention_mixer(a_c, a_l, ret_decay_logit, ret_norm_w, rows, cols, ctx_out),
            gla_mixer(b_c, b_l, gla_gate_w, gla_gate_b, gla_norm_w, ctx_out),
            ssd_mixer(s_c, s_l, ssd_conv_w, ssd_conv_b, ssd_dt_bias, ssd_a_log, ssd_d, ssd_norm_w, ctx_out),
            mlstm_mixer(m_c, m_l, mlstm_gate_b, mlstm_norm_w, ctx_out)]
    y_lat = jnp.concatenate([o[1] for o in outs], axis=-1)
    y_ctx = jnp.concatenate([o[0] for o in outs], axis=-1) if ctx_out else None
    return y_ctx, y_lat


def expert_choice_ffn(h, router_w, w_gate, w_up, w_down):
    n_tok = h.shape[1]
    cap = EC_CAPACITY * n_tok // N_EXPERTS
    aff = jax.nn.softmax(jnp.einsum('btd,de->bte', h, router_w).astype(F32), axis=-1)
    top_aff, top_idx = lax.top_k(jnp.swapaxes(aff, 1, 2), cap)
    xs = jax.vmap(lambda hb, ib: hb[ib])(h, top_idx)
    hid = (jax.nn.silu(jnp.einsum('becd,edf->becf', xs, w_gate))
           * jnp.einsum('becd,edf->becf', xs, w_up))
    out = jnp.einsum('becf,efd->becd', hid, w_down) * top_aff[..., None].astype(h.dtype)
    b, e, cp, d = out.shape
    return jax.vmap(lambda o, i: jax.ops.segment_sum(o, i, num_segments=n_tok))(
        out.reshape(b, e * cp, d), top_idx.reshape(b, e * cp))


def setup_inputs(seed: int = 0) -> dict:
    key = jax.random.key(seed)
    kit = iter(jax.random.split(key, 40))
    nrm = lambda shape, scale: jax.random.normal(next(kit), shape, F32) * scale
    gam = 1.0 - 2.0 ** (-5.0 - jnp.arange(RET_HEADS, dtype=F32))
    ret_logit = jnp.log(gam / (1.0 - gam))
    dt0 = jnp.exp(jax.random.uniform(next(kit), (DEPTH, 2, SSD_HEADS), F32,
                                     minval=math.log(1e-3), maxval=math.log(1e-1)))
    i_bias = nrm((DEPTH, 2, MLSTM_HEADS), 0.1)
    f_bias = jnp.linspace(3.0, 6.0, MLSTM_HEADS, dtype=F32) + nrm((DEPTH, 2, MLSTM_HEADS), 0.1)
    return {
        'x': nrm((BATCH, SEQ, D_MODEL), 1.0),
        'c': nrm((BATCH, D_MODEL), 1.0),
        'ctx': nrm((BATCH, CTX_LEN, D_MODEL), 1.0),
        'c_ctx': nrm((D_MODEL,), 1.0),
        'w_mod': nrm((DEPTH, D_MODEL, 6 * D_MODEL), 0.5 * D_MODEL ** -0.5),
        'b_mod': nrm((DEPTH, 6 * D_MODEL), 0.02),
        'norm1_w': 1.0 + nrm((DEPTH, D_MODEL), 0.05),
        'norm2_w': 1.0 + nrm((DEPTH, D_MODEL), 0.05),
        'w_in': nrm((DEPTH, D_MODEL, IN_COLS), D_MODEL ** -0.5),
        'w_out': nrm((DEPTH, D_MIX, D_MODEL), D_MIX ** -0.5),
        'ret_decay_logit': ret_logit + nrm((DEPTH, 2, RET_HEADS), 0.1),
        'ret_norm_w': 1.0 + nrm((DEPTH, GROUP_W), 0.05),
        'gla_gate_w': nrm((DEPTH, 2, GLA_RANK, GLA_QK), GLA_RANK ** -0.5),
        'gla_gate_b': nrm((DEPTH, 2, GLA_QK), 0.5),
        'gla_norm_w': 1.0 + nrm((DEPTH, GROUP_W), 0.05),
        'ssd_conv_w': nrm((DEPTH, SSD_CONV, SSD_CONV_CH), SSD_CONV ** -0.5),
        'ssd_conv_b': nrm((DEPTH, SSD_CONV_CH), 0.02),
        'ssd_dt_bias': dt0 + jnp.log(-jnp.expm1(-dt0)),
        'ssd_a_log': jnp.log(jax.random.uniform(next(kit), (DEPTH, 2, SSD_HEADS), F32, minval=1.0, maxval=16.0)),
        'ssd_d': 1.0 + nrm((DEPTH, SSD_HEADS), 0.1),
        'ssd_norm_w': 1.0 + nrm((DEPTH, GROUP_W), 0.05),
        'mlstm_gate_b': jnp.stack([i_bias, f_bias], axis=2),
        'mlstm_norm_w': 1.0 + nrm((DEPTH, GROUP_W), 0.05),
        'router_w': nrm((DEPTH, D_MODEL, N_EXPERTS), D_MODEL ** -0.5),
        'expert_w_gate': nrm((DEPTH, N_EXPERTS, D_MODEL, EXPERT_FF), D_MODEL ** -0.5),
        'expert_w_up': nrm((DEPTH, N_EXPERTS, D_MODEL, EXPERT_FF), D_MODEL ** -0.5),
        'expert_w_down': nrm((DEPTH, N_EXPERTS, EXPERT_FF, D_MODEL), EXPERT_FF ** -0.5),
        'final_norm_w': 1.0 + nrm((D_MODEL,), 0.05),
    }


def reference(x, c, ctx, c_ctx, w_mod, b_mod, norm1_w, norm2_w, w_in, w_out,
              ret_decay_logit, ret_norm_w, gla_gate_w, gla_gate_b, gla_norm_w,
              ssd_conv_w, ssd_conv_b, ssd_dt_bias, ssd_a_log, ssd_d, ssd_norm_w,
              mlstm_gate_b, mlstm_norm_w, router_w, expert_w_gate, expert_w_up, expert_w_down,
              final_norm_w):
    n_lat = x.shape[1]
    ROWS = n_lat // GRID_W
    rows = jnp.repeat(jnp.arange(ROWS), GRID_W).astype(F32)
    cols = jnp.tile(jnp.arange(GRID_W), ROWS).astype(F32)
    h_ctx, h_lat = ctx, x
    for l in range(DEPTH):
        ctx_out = l < DEPTH - 1
        sh1, sc1, g1, sh2, sc2, g2 = jnp.split((jax.nn.silu(c) @ w_mod[l] + b_mod[l])[:, None, :], 6, axis=-1)
        csh1, csc1, cg1, csh2, csc2, cg2 = jnp.split(jax.nn.silu(c_ctx) @ w_mod[l] + b_mod[l], 6, axis=-1)
        u_lat = modulate(h_lat, norm1_w[l], sh1, sc1) @ w_in[l]
        u_ctx = modulate(h_ctx, norm1_w[l], csh1, csc1) @ w_in[l]
        y_ctx, y_lat = parallel_mixers(u_ctx, u_lat, ret_decay_logit[l], ret_norm_w[l],
                                       gla_gate_w[l], gla_gate_b[l], gla_norm_w[l],
                                       ssd_conv_w[l], ssd_conv_b[l], ssd_dt_bias[l], ssd_a_log[l],
                                       ssd_d[l], ssd_norm_w[l], mlstm_gate_b[l], mlstm_norm_w[l],
                                       rows, cols, ctx_out)
        h_lat = h_lat + g1 * (y_lat @ w_out[l])
        h_lat = h_lat + g2 * expert_choice_ffn(modulate(h_lat, norm2_w[l], sh2, sc2), router_w[l],
                                               expert_w_gate[l], expert_w_up[l], expert_w_down[l])
        if ctx_out:
            h_ctx = h_ctx + cg1 * (y_ctx @ w_out[l])
            h_ctx = h_ctx + cg2 * expert_choice_ffn(modulate(h_ctx, norm2_w[l], csh2, csc2), router_w[l],
                                                   expert_w_gate[l], expert_w_up[l], expert_w_down[l])
    return rms_norm(h_lat, final_norm_w)
```

```python
import functools
import math

import jax
import jax.numpy as jnp
from jax import lax
from jax.experimental import pallas as pl
from jax.experimental.pallas import tpu as pltpu

F32 = jnp.float32
BF16 = jnp.bfloat16
I32 = jnp.int32

D_MODEL = 1024
GRID_W = 64
EPS = 1e-6
ROPE_BASE = 10000.0
GROUP_W = 256
HEAD_DIM = 64
N_HEADS = 4
CHUNK = 64
GLA_DK = 32
GLA_QK = 128
GLA_RANK = 16
GLA_TAU = 16.0
GLA_SUB = 16
SSD_CONV = 5
SSD_BC = 128
N_EXPERTS = 16
EC_CAPACITY = 2
EXPERT_FF = 1536
SMALL_W = 128
LR_OFF, DT_OFF, GATE_OFF = 0, 16, 24
ROW_PAD = 16
VMEM_LIMIT = 56 * 1024 * 1024


def _cparams(*sem):
    return pltpu.CompilerParams(dimension_semantics=sem, vmem_limit_bytes=VMEM_LIMIT)


def _bdot(a, b):
    return jnp.dot(a.astype(BF16), b.astype(BF16), preferred_element_type=F32)


def _bdot_nt(a, b):
    return lax.dot_general(a.astype(BF16), b.astype(BF16), (((1,), (1,)), ((), ())),
                           preferred_element_type=F32)


def _bdot_tn(a, b):
    return lax.dot_general(a.astype(BF16), b.astype(BF16), (((0,), (0,)), ((), ())),
                           preferred_element_type=F32)


def _silu(x):
    return x * jax.nn.sigmoid(x)


def _softplus(x):
    return jnp.maximum(x, 0.0) + jnp.log1p(jnp.exp(-jnp.abs(x)))


def _log_sigmoid(x):
    return -_softplus(-x)


def _rms(x, w):
    return x * lax.rsqrt(jnp.mean(x * x, axis=-1, keepdims=True) + EPS) * w


def _mod_body(cs_ref, w_ref, b_ref, o_ref):
    o_ref[...] = _bdot(_silu(cs_ref[...]), w_ref[...]) + b_ref[...]


def _mod_call(cs, w_mod, b_mod):
    depth, d, n = w_mod.shape
    tn = 1536
    return pl.pallas_call(
        _mod_body,
        grid=(depth, n // tn),
        in_specs=[pl.BlockSpec((ROW_PAD, d), lambda l, j: (0, 0)),
                  pl.BlockSpec((None, d, tn), lambda l, j: (l, 0, j)),
                  pl.BlockSpec((None, 1, tn), lambda l, j: (l, 0, j))],
        out_specs=pl.BlockSpec((None, ROW_PAD, tn), lambda l, j: (l, 0, j)),
        out_shape=jax.ShapeDtypeStruct((depth, ROW_PAD, n), F32),
        compiler_params=_cparams("parallel", "parallel"),
        name="adaln",
    )(cs, w_mod, b_mod.reshape(depth, 1, n))


def _mod_spec(is_ctx, batch):
    if is_ctx:
        return pl.BlockSpec((None, 6, D_MODEL), lambda b, i: (batch, 0, 0))
    return pl.BlockSpec((None, 6, D_MODEL), lambda b, i: (b, 0, 0))


IN_WIDTHS = (1024, 768, 768, 1024, SMALL_W)


def _in_body(h_ref, mod_ref, nw_ref, w_ref, ret_ref, gla_ref, ssd_ref, mls_ref, sm_ref):
    mod = mod_ref[...]
    xn = _rms(h_ref[...], nw_ref[...]) * (1.0 + mod[1:2, :]) + mod[0:1, :]
    u = jnp.dot(xn.astype(BF16), w_ref[...], preferred_element_type=F32)
    off = 0
    for ref, w in zip((ret_ref, gla_ref, ssd_ref, mls_ref, sm_ref), IN_WIDTHS):
        ref[...] = u[:, off:off + w]
        off += w


def _in_call(h, mod_l, nw, w_big, is_ctx):
    batch, t, d = h.shape
    tm = min(512, t)
    n = w_big.shape[1]
    return pl.pallas_call(
        _in_body,
        grid=(batch, t // tm),
        in_specs=[pl.BlockSpec((None, tm, d), lambda b, i: (b, i, 0)),
                  _mod_spec(is_ctx, batch),
                  pl.BlockSpec((1, d), lambda b, i: (0, 0)),
                  pl.BlockSpec((d, n), lambda b, i: (0, 0))],
        out_specs=[pl.BlockSpec((None, tm, w), lambda b, i: (b, i, 0)) for w in IN_WIDTHS],
        out_shape=[jax.ShapeDtypeStruct((batch, t, w), F32) for w in IN_WIDTHS],
        compiler_params=_cparams("parallel", "parallel"),
        name="in_proj",
    )(h, mod_l, nw, w_big)


def _arrange_w_in(w):
    ret_end = 4 * GROUP_W
    gla_main = ret_end + 2 * GLA_QK + 2 * GROUP_W
    gla_end = gla_main + GLA_RANK
    ssd_main = gla_end + GROUP_W + GROUP_W + 2 * SSD_BC
    ssd_end = ssd_main + 2 * N_HEADS
    mls_main = ssd_end + 4 * GROUP_W
    mls_end = mls_main + 4 * N_HEADS
    narrow = (gla_end - gla_main) + (ssd_end - ssd_main) + (mls_end - mls_main)
    pad = jnp.zeros((w.shape[0], SMALL_W - narrow), w.dtype)
    return jnp.concatenate([w[:, :ret_end], w[:, ret_end:gla_main], w[:, gla_end:ssd_main],
                            w[:, ssd_end:mls_main], w[:, gla_main:gla_end], w[:, ssd_main:ssd_end],
                            w[:, mls_main:mls_end], pad], axis=1).astype(BF16)


def _masks(n, rev):
    i = lax.broadcasted_iota(I32, (n, n), 0)
    j = lax.broadcasted_iota(I32, (n, n), 1)
    return ((j >= i), (j <= i)) if rev else ((j <= i), (j >= i))


def _cum(la_col, la_row, mask, mask_t):
    g_col = jnp.sum(jnp.where(mask, la_row, 0.0), axis=1, keepdims=True)
    g_row = jnp.sum(jnp.where(mask_t, la_col, 0.0), axis=0, keepdims=True)
    g_tot = jnp.sum(la_col, axis=0, keepdims=True)
    return g_col, g_row, g_tot


def _decay_step(q, k, v, g_col, g_row, g_tot, mask, s_ref, want_out):
    s_in = s_ref[...]
    o = None
    if want_out:
        dec = jnp.exp(jnp.where(mask, g_col - g_row, -jnp.inf))
        o = _bdot(_bdot_nt(q, k) * dec, v) + _bdot(q * jnp.exp(g_col), s_in)
    s_ref[...] = jnp.exp(g_tot) * s_in + _bdot_tn(k * jnp.exp(g_tot - g_col), v)
    return o


def _scan_both(n_ctx, n_lat, step):
    def ctx_it(s, c):
        step(False, s, 0)
        step(False, n_ctx - 1 - s, 1)
        return c

    def lat_it(s, c):
        step(True, s, 0)
        step(True, n_lat - 1 - s, 1)
        return c

    lax.fori_loop(0, n_ctx, ctx_it, 0)
    lax.fori_loop(0, n_lat, lat_it, 0)


def _heads_apply(x, fn):
    return jnp.concatenate([fn(x[:, h * HEAD_DIM:(h + 1) * HEAD_DIM]) for h in range(N_HEADS)], axis=1)


def _head_rms(x, center):
    if center:
        x = x - jnp.mean(x, axis=-1, keepdims=True)
    return x * lax.rsqrt(jnp.mean(x * x, axis=-1, keepdims=True) + EPS)


def _post_tiles(t):
    tile = min(256, t)
    return [(r, tile) for r in range(0, t, tile)]


def _mixer_call(body, name, batch, t_lat, t_ctx, ctx_out, per_sample, shared, scratch):
    in_specs, args = [], []
    for a in per_sample:
        in_specs.append(pl.BlockSpec((None,) + a.shape[1:], lambda b, nd=a.ndim: (b,) + (0,) * (nd - 1)))
        args.append(a)
    for a in shared:
        if isinstance(a, tuple):
            in_specs.append(pl.BlockSpec(memory_space=pltpu.SMEM))
            args.append(a[0])
        else:
            in_specs.append(pl.BlockSpec(a.shape, lambda b, nd=a.ndim: (0,) * nd))
            args.append(a)
    out_shape = [jax.ShapeDtypeStruct((batch, t_lat, GROUP_W), F32)]
    out_specs = [pl.BlockSpec((None, t_lat, GROUP_W), lambda b: (b, 0, 0))]
    if ctx_out:
        out_shape.append(jax.ShapeDtypeStruct((batch, t_ctx, GROUP_W), F32))
        out_specs.append(pl.BlockSpec((None, t_ctx, GROUP_W), lambda b: (b, 0, 0)))
    return pl.pallas_call(
        body, grid=(batch,), in_specs=in_specs, out_specs=out_specs, out_shape=out_shape,
        scratch_shapes=scratch, compiler_params=_cparams("parallel"), name=name,
    )(*args)


def _rope(x, cos, sin_signed, first_half):
    return x * cos + jnp.where(first_half, pltpu.roll(x, x.shape[1] - 16, 1), pltpu.roll(x, 16, 1)) * sin_signed


def _ret_body(lg_ref, ul_ref, uc_ref, cos_ref, sin_ref, nw_ref, *rest, ctx_out):
    if ctx_out:
        yl_ref, yc_ref, s_ref, accl_ref, accc_ref = rest
    else:
        yl_ref, s_ref, accl_ref, accc_ref = rest
        yc_ref = None
    t_lat, t_ctx = ul_ref.shape[0], uc_ref.shape[0]
    s_ref[...] = jnp.zeros_like(s_ref)
    accl_ref[...] = jnp.zeros_like(accl_ref)
    if ctx_out:
        accc_ref[...] = jnp.zeros_like(accc_ref)
    lane = lax.broadcasted_iota(I32, (1, GROUP_W), 1)
    first_half = (lane % 32) < 16
    idx = lax.broadcasted_iota(I32, (CHUNK, 1), 0).astype(F32)
    masks = (_masks(CHUNK, False), _masks(CHUNK, True))

    def step(is_lat, c, d):
        u_ref, acc_ref = (ul_ref, accl_ref) if is_lat else (uc_ref, accc_ref)
        want_out = is_lat or ctx_out
        r = pl.multiple_of(c * CHUNK, CHUNK)
        q = u_ref[pl.ds(r, CHUNK), 0:GROUP_W] * (HEAD_DIM ** -0.5)
        k = u_ref[pl.ds(r, CHUNK), GROUP_W:2 * GROUP_W]
        v = u_ref[pl.ds(r, CHUNK), 2 * GROUP_W:3 * GROUP_W]
        if is_lat:
            cos, sin = cos_ref[pl.ds(r, CHUNK), :], sin_ref[pl.ds(r, CHUNK), :]
            q, k = _rope(q, cos, sin, first_half), _rope(k, cos, sin, first_half)
        mask, _ = masks[d]
        steps = (idx + 1.0) if d == 0 else (CHUNK - idx)
        outs = []
        for h in range(N_HEADS):
            sl = slice(h * HEAD_DIM, (h + 1) * HEAD_DIM)
            lg = lg_ref[d, h]
            g_col = steps * lg
            g_row = (lax.broadcasted_iota(I32, (1, CHUNK), 1).astype(F32) + 1.0) * lg if d == 0 else \
                (CHUNK - lax.broadcasted_iota(I32, (1, CHUNK), 1).astype(F32)) * lg
            g_tot = jnp.full((1, 1), CHUNK, F32) * lg
            outs.append(_decay_step(q[:, sl], k[:, sl], v[:, sl], g_col, g_row, g_tot, mask,
                                    s_ref.at[d, h], want_out))
        if want_out:
            acc_ref[pl.ds(r, CHUNK), :] += jnp.concatenate(outs, axis=1)

    _scan_both(t_ctx // CHUNK, t_lat // CHUNK, step)

    def post(u_ref, acc_ref, y_ref, t):
        for r, n in _post_tiles(t):
            o = _heads_apply(acc_ref[r:r + n, :], lambda x: _head_rms(x, True))
            y_ref[r:r + n, :] = o * nw_ref[...] * _silu(u_ref[r:r + n, 3 * GROUP_W:4 * GROUP_W])

    post(ul_ref, accl_ref, yl_ref, t_lat)
    if ctx_out:
        post(uc_ref, accc_ref, yc_ref, t_ctx)


def _ret_call(u_lat, u_ctx, log_gamma, cos, sin, nw, ctx_out):
    batch, t_lat, _ = u_lat.shape
    t_ctx = u_ctx.shape[1]
    scratch = [pltpu.VMEM((2, N_HEADS, HEAD_DIM, HEAD_DIM), F32),
               pltpu.VMEM((t_lat, GROUP_W), F32), pltpu.VMEM((t_ctx, GROUP_W), F32)]
    in_specs = [pl.BlockSpec(memory_space=pltpu.SMEM),
                pl.BlockSpec((None,) + u_lat.shape[1:], lambda b: (b, 0, 0)),
                pl.BlockSpec((None,) + u_ctx.shape[1:], lambda b: (b, 0, 0)),
                pl.BlockSpec(cos.shape, lambda b: (0, 0)),
                pl.BlockSpec(sin.shape, lambda b: (0, 0)),
                pl.BlockSpec(nw.shape, lambda b: (0, 0))]
    out_shape = [jax.ShapeDtypeStruct((batch, t_lat, GROUP_W), F32)]
    out_specs = [pl.BlockSpec((None, t_lat, GROUP_W), lambda b: (b, 0, 0))]
    if ctx_out:
        out_shape.append(jax.ShapeDtypeStruct((batch, t_ctx, GROUP_W), F32))
        out_specs.append(pl.BlockSpec((None, t_ctx, GROUP_W), lambda b: (b, 0, 0)))
    return pl.pallas_call(
        functools.partial(_ret_body, ctx_out=ctx_out), grid=(batch,), in_specs=in_specs,
        out_specs=out_specs, out_shape=out_shape, scratch_shapes=scratch,
        compiler_params=_cparams("parallel"), name="retention",
    )(log_gamma, u_lat, u_ctx, cos, sin, nw)


def _rope_tables(t_lat):
    nf = 16
    inv = ROPE_BASE ** (-jnp.arange(nf, dtype=F32) / nf)
    rows = jnp.repeat(jnp.arange(t_lat // GRID_W), GRID_W).astype(F32)
    cols = jnp.tile(jnp.arange(GRID_W), t_lat // GRID_W).astype(F32)
    ang_r = rows[:, None] * inv[None, :]
    ang_c = cols[:, None] * inv[None, :]
    cos_h = jnp.concatenate([jnp.cos(ang_r)] * 2 + [jnp.cos(ang_c)] * 2, axis=1)
    sin_h = jnp.concatenate([-jnp.sin(ang_r), jnp.sin(ang_r), -jnp.sin(ang_c), jnp.sin(ang_c)], axis=1)
    return jnp.tile(cos_h, (1, N_HEADS)), jnp.tile(sin_h, (1, N_HEADS))


def _ssd_body(ul_ref, uc_ref, sml_ref, smc_ref, cw_ref, cb_ref, dtb_ref, an_ref, dsk_ref, nw_ref, *rest,
              ctx_out):
    if ctx_out:
        yl_ref, yc_ref, s_ref, accl_ref, accc_ref, xl_ref, xc_ref, pad_ref = rest
    else:
        yl_ref, s_ref, accl_ref, accc_ref, xl_ref, xc_ref, pad_ref = rest
        yc_ref = None
    t_lat, t_ctx = ul_ref.shape[0], uc_ref.shape[0]
    conv_ch = GROUP_W + 2 * SSD_BC
    half = (SSD_CONV - 1) // 2
    halo = 8

    def conv(u_ref, x_ref, t):
        pad_ref[0:halo, :] = jnp.zeros((halo, conv_ch), F32)
        pad_ref[halo + t:2 * halo + t, :] = jnp.zeros((halo, conv_ch), F32)
        for r, n in _post_tiles(t):
            pad_ref[halo + r:halo + r + n, :] = u_ref[r:r + n, GROUP_W:GROUP_W + conv_ch]
        for r, n in _post_tiles(t):
            acc = jnp.zeros((n, conv_ch), F32) + cb_ref[...]
            for tap in range(SSD_CONV):
                lo = halo + r + tap - half
                acc = acc + pad_ref[lo:lo + n, :] * cw_ref[tap:tap + 1, :]
            x_ref[r:r + n, :] = _silu(acc)

    conv(uc_ref, xc_ref, t_ctx)
    conv(ul_ref, xl_ref, t_lat)
    s_ref[...] = jnp.zeros_like(s_ref)
    accl_ref[...] = jnp.zeros_like(accl_ref)
    if ctx_out:
        accc_ref[...] = jnp.zeros_like(accc_ref)
    masks = (_masks(CHUNK, False), _masks(CHUNK, True))

    def step(is_lat, c, d):
        x_ref, sm_ref, acc_ref = (xl_ref, sml_ref, accl_ref) if is_lat else (xc_ref, smc_ref, accc_ref)
        want_out = is_lat or ctx_out
        r = pl.multiple_of(c * CHUNK, CHUNK)
        dt = _softplus(sm_ref[pl.ds(r, CHUNK), :] + dtb_ref[...])
        la = dt * an_ref[...]
        la_t = la.T
        mask, mask_t = masks[d]
        outs = []
        for h in range(N_HEADS):
            lane = DT_OFF + d * N_HEADS + h
            grp = h // 2
            xs = x_ref[pl.ds(r, CHUNK), h * HEAD_DIM:(h + 1) * HEAD_DIM]
            bm = x_ref[pl.ds(r, CHUNK), GROUP_W + grp * HEAD_DIM:GROUP_W + (grp + 1) * HEAD_DIM]
            cm = x_ref[pl.ds(r, CHUNK), GROUP_W + SSD_BC + grp * HEAD_DIM:GROUP_W + SSD_BC + (grp + 1) * HEAD_DIM]
            g_col, g_row, g_tot = _cum(la[:, lane:lane + 1], la_t[lane:lane + 1, :], mask, mask_t)
            outs.append(_decay_step(cm, bm, xs * dt[:, lane:lane + 1], g_col, g_row, g_tot, mask,
                                    s_ref.at[d, h], want_out))
        if want_out:
            acc_ref[pl.ds(r, CHUNK), :] += jnp.concatenate(outs, axis=1)

    _scan_both(t_ctx // CHUNK, t_lat // CHUNK, step)

    def post(u_ref, x_ref, acc_ref, y_ref, t):
        for r, n in _post_tiles(t):
            y = acc_ref[r:r + n, :] + dsk_ref[...] * x_ref[r:r + n, 0:GROUP_W]
            y_ref[r:r + n, :] = _rms(y * _silu(u_ref[r:r + n, 0:GROUP_W]), nw_ref[...])

    post(ul_ref, xl_ref, accl_ref, yl_ref, t_lat)
    if ctx_out:
        post(uc_ref, xc_ref, accc_ref, yc_ref, t_ctx)


def _ssd_call(u_lat, u_ctx, sm_lat, sm_ctx, conv_w, conv_b, dt_bias, a_log, d_skip, nw, ctx_out):
    batch, t_lat, _ = u_lat.shape
    t_ctx = u_ctx.shape[1]
    conv_ch = GROUP_W + 2 * SSD_BC
    zeros = jnp.zeros((SMALL_W,), F32)
    lanes = slice(DT_OFF, DT_OFF + 2 * N_HEADS)
    dtb = zeros.at[lanes].set(dt_bias.reshape(-1))[None]
    a_neg = zeros.at[lanes].set(-jnp.exp(a_log.astype(F32)).reshape(-1))[None]
    cw = jnp.concatenate([conv_w, jnp.zeros((8 - SSD_CONV, conv_ch), F32)], axis=0)
    dsk = jnp.repeat(d_skip, HEAD_DIM)[None]
    scratch = [pltpu.VMEM((2, N_HEADS, HEAD_DIM, HEAD_DIM), F32),
               pltpu.VMEM((t_lat, GROUP_W), F32), pltpu.VMEM((t_ctx, GROUP_W), F32),
               pltpu.VMEM((t_lat, conv_ch), F32), pltpu.VMEM((t_ctx, conv_ch), F32),
               pltpu.VMEM((t_lat + 16, conv_ch), F32)]
    return _mixer_call(functools.partial(_ssd_body, ctx_out=ctx_out), "ssd", batch, t_lat, t_ctx, ctx_out,
                       [u_lat, u_ctx, sm_lat, sm_ctx], [cw, conv_b[None], dtb, a_neg, dsk, nw], scratch)


def _mlstm_body(ul_ref, uc_ref, sml_ref, smc_ref, gb_ref, nw_ref, *rest, ctx_out):
    if ctx_out:
        yl_ref, yc_ref, c_ref, n_ref, m_ref, accl_ref, accc_ref = rest
    else:
        yl_ref, c_ref, n_ref, m_ref, accl_ref, accc_ref = rest
        yc_ref = None
    t_lat, t_ctx = ul_ref.shape[0], uc_ref.shape[0]
    c_ref[...] = jnp.zeros_like(c_ref)
    n_ref[...] = jnp.zeros_like(n_ref)
    m_ref[...] = jnp.zeros_like(m_ref)
    accl_ref[...] = jnp.zeros_like(accl_ref)
    if ctx_out:
        accc_ref[...] = jnp.zeros_like(accc_ref)
    masks = (_masks(CHUNK, False), _masks(CHUNK, True))

    def step(is_lat, c, d):
        u_ref, sm_ref, acc_ref = (ul_ref, sml_ref, accl_ref) if is_lat else (uc_ref, smc_ref, accc_ref)
        want_out = is_lat or ctx_out
        r = pl.multiple_of(c * CHUNK, CHUNK)
        gates = sm_ref[pl.ds(r, CHUNK), :] + gb_ref[...]
        lf = _log_sigmoid(gates)
        gates_t, lf_t = gates.T, lf.T
        mask, mask_t = masks[d]
        outs = []
        for h in range(N_HEADS):
            sl = slice(h * HEAD_DIM, (h + 1) * HEAD_DIM)
            i_lane = GATE_OFF + d * 2 * N_HEADS + h
            f_lane = i_lane + N_HEADS
            q = u_ref[pl.ds(r, CHUNK), sl]
            k = u_ref[pl.ds(r, CHUNK), GROUP_W + h * HEAD_DIM:GROUP_W + (h + 1) * HEAD_DIM] * (HEAD_DIM ** -0.5)
            v = u_ref[pl.ds(r, CHUNK), 2 * GROUP_W + h * HEAD_DIM:2 * GROUP_W + (h + 1) * HEAD_DIM]
            ig_col, ig_row = gates[:, i_lane:i_lane + 1], gates_t[i_lane:i_lane + 1, :]
            g_col, g_row, g_tot = _cum(lf[:, f_lane:f_lane + 1], lf_t[f_lane:f_lane + 1, :], mask, mask_t)
            c_in, n_in, m_in = c_ref[d, h], n_ref[d, h], m_ref[d, h]
            if want_out:
                d_log = jnp.where(mask, g_col - g_row + ig_row, -jnp.inf)
                m_inter = g_col + m_in
                m_tot = jnp.maximum(m_inter, jnp.max(d_log, axis=1, keepdims=True))
                qk = _bdot_nt(q, k) * jnp.exp(d_log - m_tot)
                s_inter = jnp.exp(m_inter - m_tot)
                num = _bdot(qk, v) + s_inter * _bdot(q, c_in)
                den = jnp.sum(qk, axis=1, keepdims=True) + s_inter * jnp.sum(q * n_in, axis=1, keepdims=True)
                outs.append(num / jnp.maximum(jnp.abs(den), jnp.exp(-m_tot)))
            w_end = g_tot - g_col + ig_col
            b_max = jnp.max(w_end, axis=0, keepdims=True)
            ke = k * jnp.exp(w_end - b_max)
            m_new = jnp.maximum(g_tot + m_in, b_max)
            old, new = jnp.exp(g_tot + m_in - m_new), jnp.exp(b_max - m_new)
            c_ref[d, h] = old * c_in + new * _bdot_tn(ke, v)
            n_ref[d, h] = old * n_in + new * jnp.sum(ke, axis=0, keepdims=True)
            m_ref[d, h] = m_new
        if want_out:
            acc_ref[pl.ds(r, CHUNK), :] += jnp.concatenate(outs, axis=1)

    _scan_both(t_ctx // CHUNK, t_lat // CHUNK, step)

    def post(u_ref, acc_ref, y_ref, t):
        for r, n in _post_tiles(t):
            o = _heads_apply(acc_ref[r:r + n, :], lambda x: _head_rms(x, False))
            y_ref[r:r + n, :] = jax.nn.sigmoid(u_ref[r:r + n, 3 * GROUP_W:4 * GROUP_W]) * (o * nw_ref[...])

    post(ul_ref, accl_ref, yl_ref, t_lat)
    if ctx_out:
        post(uc_ref, accc_ref, yc_ref, t_ctx)


def _mlstm_call(u_lat, u_ctx, sm_lat, sm_ctx, gate_b, nw, ctx_out):
    batch, t_lat, _ = u_lat.shape
    t_ctx = u_ctx.shape[1]
    gb = jnp.zeros((SMALL_W,), F32).at[GATE_OFF:GATE_OFF + 4 * N_HEADS].set(gate_b.reshape(-1))[None]
    scratch = [pltpu.VMEM((2, N_HEADS, HEAD_DIM, HEAD_DIM), F32), pltpu.VMEM((2, N_HEADS, 1, HEAD_DIM), F32),
               pltpu.VMEM((2, N_HEADS, 1, 1), F32),
               pltpu.VMEM((t_lat, GROUP_W), F32), pltpu.VMEM((t_ctx, GROUP_W), F32)]
    return _mixer_call(functools.partial(_mlstm_body, ctx_out=ctx_out), "mlstm", batch, t_lat, t_ctx, ctx_out,
                       [u_lat, u_ctx, sm_lat, sm_ctx], [gb, nw], scratch)


def _split3(x):
    hi = x.astype(BF16)
    r1 = x - hi.astype(F32)
    mid = r1.astype(BF16)
    return hi, mid, (r1 - mid.astype(F32)).astype(BF16)


def _gla_body(ul_ref, uc_ref, sml_ref, smc_ref, gw_ref, gb_ref, nw_ref, *rest, ctx_out):
    if ctx_out:
        yl_ref, yc_ref, st_ref, accl_ref, accc_ref = rest
    else:
        yl_ref, st_ref, accl_ref, accc_ref = rest
        yc_ref = None
    t_lat, t_ctx = ul_ref.shape[0], uc_ref.shape[0]
    st_ref[...] = jnp.zeros_like(st_ref)
    accl_ref[...] = jnp.zeros_like(accl_ref)
    if ctx_out:
        accc_ref[...] = jnp.zeros_like(accc_ref)
    n_sub = CHUNK // GLA_SUB
    i = lax.broadcasted_iota(I32, (CHUNK, CHUNK), 0)
    j = lax.broadcasted_iota(I32, (CHUNK, CHUNK), 1)
    same = (i // GLA_SUB) == (j // GLA_SUB)
    lcum = ((same & (j <= i)).astype(F32).astype(BF16), (same & (j >= i)).astype(F32).astype(BF16))
    expand = ((lax.broadcasted_iota(I32, (GLA_QK, GROUP_W), 0) // GLA_DK)
              == (lax.broadcasted_iota(I32, (GLA_QK, GROUP_W), 1) // HEAD_DIM)).astype(F32).astype(BF16)
    blockdiag = ((lax.broadcasted_iota(I32, (GROUP_W, GLA_QK), 0) // HEAD_DIM)
                 == (lax.broadcasted_iota(I32, (GROUP_W, GLA_QK), 1) // GLA_DK)).astype(F32)
    sub_row = lax.broadcasted_iota(I32, (CHUNK, 1), 0) % GLA_SUB

    def step(is_lat, c, d):
        u_ref, sm_ref, acc_ref = (ul_ref, sml_ref, accl_ref) if is_lat else (uc_ref, smc_ref, accc_ref)
        want_out = is_lat or ctx_out
        r = pl.multiple_of(c * CHUNK, CHUNK)
        q = u_ref[pl.ds(r, CHUNK), 0:GLA_QK] * (GLA_DK ** -0.5)
        k = u_ref[pl.ds(r, CHUNK), GLA_QK:2 * GLA_QK]
        v = u_ref[pl.ds(r, CHUNK), 2 * GLA_QK:2 * GLA_QK + GROUP_W]
        lr = sm_ref[pl.ds(r, CHUNK), LR_OFF:LR_OFF + GLA_RANK]
        la = _log_sigmoid(_bdot(lr, gw_ref[d]) + gb_ref[d]) / GLA_TAU
        g = sum(jnp.dot(lcum[d], piece, preferred_element_type=F32) for piece in _split3(la))
        o = None
        if want_out:
            prods, vals = [], []
            for delta in range(GLA_SUB):
                if delta == 0:
                    k_s, g_s, v_s = k, g, v
                    e = jnp.ones_like(g)
                else:
                    sh = delta if d == 0 else CHUNK - delta
                    k_s, g_s, v_s = pltpu.roll(k, sh, 0), pltpu.roll(g, sh, 0), pltpu.roll(v, sh, 0)
                    valid = (sub_row >= delta) if d == 0 else (sub_row + delta < GLA_SUB)
                    e = jnp.exp(jnp.where(valid, g - g_s, -jnp.inf))
                prods.append(q * k_s * e)
                vals.append(v_s)
            p = jnp.concatenate(prods, axis=0)
            p_hi = p.astype(BF16)
            p_lo = (p - p_hi.astype(F32)).astype(BF16)
            w = (jnp.dot(p_hi, expand, preferred_element_type=F32)
                 + jnp.dot(p_lo, expand, preferred_element_type=F32))
            o = sum(w[n * CHUNK:(n + 1) * CHUNK] * vals[n] for n in range(GLA_SUB))
        o_inter = [None] * n_sub
        for a in (range(n_sub) if d == 0 else range(n_sub - 1, -1, -1)):
            rows = slice(a * GLA_SUB, (a + 1) * GLA_SUB)
            g_a = g[rows]
            g_end = g_a[GLA_SUB - 1:GLA_SUB] if d == 0 else g_a[0:1]
            st = st_ref[d]
            if want_out:
                o_inter[a] = _bdot_nt(q[rows] * jnp.exp(g_a), st)
            st_ref[d] = st * jnp.exp(g_end) + _bdot_tn(v[rows], k[rows] * jnp.exp(g_end - g_a)) * blockdiag
        if want_out:
            acc_ref[pl.ds(r, CHUNK), :] += o + jnp.concatenate(o_inter, axis=0)

    _scan_both(t_ctx // CHUNK, t_lat // CHUNK, step)

    def post(u_ref, acc_ref, y_ref, t):
        for r, n in _post_tiles(t):
            o = _heads_apply(acc_ref[r:r + n, :], lambda x: _head_rms(x, False))
            y_ref[r:r + n, :] = o * nw_ref[...] * _silu(u_ref[r:r + n, 2 * GLA_QK + GROUP_W:2 * GLA_QK + 2 * GROUP_W])

    post(ul_ref, accl_ref, yl_ref, t_lat)
    if ctx_out:
        post(uc_ref, accc_ref, yc_ref, t_ctx)


def _gla_call(u_lat, u_ctx, sm_lat, sm_ctx, gate_w, gate_b, nw, ctx_out):
    batch, t_lat, _ = u_lat.shape
    t_ctx = u_ctx.shape[1]
    scratch = [pltpu.VMEM((2, GROUP_W, GLA_QK), F32),
               pltpu.VMEM((t_lat, GROUP_W), F32), pltpu.VMEM((t_ctx, GROUP_W), F32)]
    return _mixer_call(functools.partial(_gla_body, ctx_out=ctx_out), "gla", batch, t_lat, t_ctx, ctx_out,
                       [u_lat, u_ctx, sm_lat, sm_ctx], [gate_w, gate_b[:, None, :], nw], scratch)


def _out_body(yr_ref, yg_ref, ys_ref, ym_ref, h_ref, mod_ref, wo_ref, n2_ref, rw_ref, h1_ref, hn_ref, lt_ref):
    mod = mod_ref[...]
    proj = sum(_bdot(y_ref[...], wo_ref[n * GROUP_W:(n + 1) * GROUP_W, :])
               for n, y_ref in enumerate((yr_ref, yg_ref, ys_ref, ym_ref)))
    h1 = h_ref[...] + mod[2:3, :] * proj
    h1_ref[...] = h1
    xn = _rms(h1, n2_ref[...]) * (1.0 + mod[4:5, :]) + mod[3:4, :]
    hn_ref[...] = xn.astype(BF16)
    x_hi, x_mid, x_lo = _split3(xn)
    w_hi, w_mid, w_lo = rw_ref[0], rw_ref[1], rw_ref[2]
    dot = functools.partial(jnp.dot, preferred_element_type=F32)
    logits = (dot(x_lo, w_hi) + dot(x_mid, w_mid) + dot(x_hi, w_lo)
              + dot(x_mid, w_hi) + dot(x_hi, w_mid) + dot(x_hi, w_hi))
    lt_ref[...] = logits.T


def _out_call(ys, h, mod_l, wo, n2w, rw3, is_ctx):
    batch, t, d = h.shape
    tm = min(512, t)
    y_spec = pl.BlockSpec((None, tm, GROUP_W), lambda b, i: (b, i, 0))
    return pl.pallas_call(
        _out_body,
        grid=(batch, t // tm),
        in_specs=[y_spec, y_spec, y_spec, y_spec,
                  pl.BlockSpec((None, tm, d), lambda b, i: (b, i, 0)),
                  _mod_spec(is_ctx, batch),
                  pl.BlockSpec((d, d), lambda b, i: (0, 0)),
                  pl.BlockSpec((1, d), lambda b, i: (0, 0)),
                  pl.BlockSpec(rw3.shape, lambda b, i: (0, 0, 0))],
        out_specs=[pl.BlockSpec((None, tm, d), lambda b, i: (b, i, 0)),
                   pl.BlockSpec((None, tm, d), lambda b, i: (b, i, 0)),
                   pl.BlockSpec((None, SMALL_W, tm), lambda b, i: (b, 0, i))],
        out_shape=[jax.ShapeDtypeStruct((batch, t, d), F32),
                   jax.ShapeDtypeStruct((batch, t, d), BF16),
                   jax.ShapeDtypeStruct((batch, SMALL_W, t), F32)],
        compiler_params=_cparams("parallel", "parallel"),
        name="out_proj",
    )(*ys, h, mod_l, wo, n2w, rw3)


def _excl_prefix(flags, t):
    blk = min(256, t)
    below = (lax.broadcasted_iota(I32, (blk, blk), 0) < lax.broadcasted_iota(I32, (blk, blk), 1))
    below = below.astype(F32).astype(BF16)
    outs = []
    carry = jnp.zeros((flags.shape[0], 1), F32)
    for n in range(t // blk):
        f = flags[:, n * blk:(n + 1) * blk]
        outs.append(jnp.dot(f.astype(BF16), below, preferred_element_type=F32) + carry)
        carry = carry + jnp.sum(f, axis=1, keepdims=True)
    return jnp.concatenate(outs, axis=1)


def _route_body(lt_ref, hn_ref, xg_ref, affc_ref, post_ref, pos_s, aff_s, *, cap):
    t = hn_ref.shape[0]
    logits = lt_ref[0:N_EXPERTS, :]
    ex = jnp.exp(logits - jnp.max(logits, axis=0, keepdims=True))
    aff = ex / jnp.sum(ex, axis=0, keepdims=True)
    def fix_bit(n, thr):
        cand = thr | (jnp.int32(1) << (30 - n))
        cnt = jnp.sum((aff >= pltpu.bitcast(cand, F32)).astype(F32), axis=1, keepdims=True)
        return jnp.where(cnt >= float(cap), cand, thr)

    thr = lax.fori_loop(0, 31, fix_bit, jnp.zeros((N_EXPERTS, 1), I32))
    above = (aff >= pltpu.bitcast(thr + 1, F32)).astype(F32)
    tied = (aff >= pltpu.bitcast(thr, F32)).astype(F32) - above
    need = float(cap) - jnp.sum(above, axis=1, keepdims=True)
    sel = above + tied * (_excl_prefix(tied, t) < need).astype(F32)
    pos = jnp.where(sel > 0.0, _excl_prefix(sel, t), -1.0)
    pos_s[...] = pos
    aff_s[...] = aff
    post_ref[...] = jnp.concatenate([pos, jnp.full((SMALL_W - N_EXPERTS, t), -1.0, F32)], axis=0).T
    slots = lax.broadcasted_iota(I32, (cap, t), 0).astype(F32)

    def gather(e, carry):
        hit = slots == pos_s[pl.ds(e, 1), :]
        xg_ref[e] = jnp.dot(hit.astype(BF16), hn_ref[...], preferred_element_type=F32).astype(BF16)
        a = jnp.sum(jnp.where(hit, aff_s[pl.ds(e, 1), :], 0.0), axis=1, keepdims=True)
        affc_ref[e] = jnp.broadcast_to(a, (cap, SMALL_W))
        return carry

    lax.fori_loop(0, N_EXPERTS, gather, 0)


def _route_call(lt, hn, cap):
    batch, t, d = hn.shape
    return pl.pallas_call(
        functools.partial(_route_body, cap=cap),
        grid=(batch,),
        in_specs=[pl.BlockSpec((None, SMALL_W, t), lambda b: (b, 0, 0)),
                  pl.BlockSpec((None, t, d), lambda b: (b, 0, 0))],
        out_specs=[pl.BlockSpec((N_EXPERTS, cap, d), lambda b: (0, b, 0)),
                   pl.BlockSpec((N_EXPERTS, cap, SMALL_W), lambda b: (0, b, 0)),
                   pl.BlockSpec((None, t, SMALL_W), lambda b: (b, 0, 0))],
        out_shape=[jax.ShapeDtypeStruct((N_EXPERTS, batch * cap, d), BF16),
                   jax.ShapeDtypeStruct((N_EXPERTS, batch * cap, SMALL_W), F32),
                   jax.ShapeDtypeStruct((batch, t, SMALL_W), F32)],
        scratch_shapes=[pltpu.VMEM((N_EXPERTS, t), F32), pltpu.VMEM((N_EXPERTS, t), F32)],
        compiler_params=_cparams("parallel"),
        name="route_gather",
    )(lt, hn)


FFN_TF = 512
FFN_TM = 512


def _ffn_body(*refs, n_sets):
    ins = refs[:2 * n_sets]
    wg_ref, wu_ref, wd_ref = refs[2 * n_sets:2 * n_sets + 3]
    outs = refs[2 * n_sets + 3:3 * n_sets + 3]
    accs = refs[3 * n_sets + 3:4 * n_sets + 3]
    wg_s, wu_s, wd_s = refs[4 * n_sets + 3:]
    f = pl.program_id(1)
    wg_s[...] = wg_ref[...].astype(BF16)
    wu_s[...] = wu_ref[...].astype(BF16)
    wd_s[...] = wd_ref[...].astype(BF16)
    dot = functools.partial(jnp.dot, preferred_element_type=F32)
    for s in range(n_sets):
        x_ref, a_ref, y_ref, acc_ref = ins[2 * s], ins[2 * s + 1], outs[s], accs[s]
        rows = x_ref.shape[0]

        @pl.when(f == 0)
        def _():
            acc_ref[...] = jnp.zeros_like(acc_ref)

        for r in range(0, rows, FFN_TM):
            n = min(FFN_TM, rows - r)
            x = x_ref[r:r + n, :]
            hid = _silu(dot(x, wg_s[...])) * dot(x, wu_s[...])
            acc_ref[r:r + n, :] += dot(hid.astype(BF16), wd_s[...])

        @pl.when(f == pl.num_programs(1) - 1)
        def _():
            for r in range(0, rows, FFN_TM):
                n = min(FFN_TM, rows - r)
                y_ref[r:r + n, :] = (acc_ref[r:r + n, :] * a_ref[r:r + n, 0:1]).astype(BF16)


def _ffn_call(sets, wg, wu, wd):
    n_exp, d, ff = wg.shape
    in_specs, args, out_specs, out_shape, scratch = [], [], [], [], []
    for xg, affc in sets:
        m = xg.shape[1]
        in_specs += [pl.BlockSpec((None, m, d), lambda e, f: (e, 0, 0)),
                     pl.BlockSpec((None, m, SMALL_W), lambda e, f: (e, 0, 0))]
        args += [xg, affc]
        out_specs.append(pl.BlockSpec((None, m, d), lambda e, f: (e, 0, 0)))
        out_shape.append(jax.ShapeDtypeStruct((n_exp, m, d), BF16))
        scratch.append(pltpu.VMEM((m, d), F32))
    in_specs += [pl.BlockSpec((None, d, FFN_TF), lambda e, f: (e, 0, f)),
                 pl.BlockSpec((None, d, FFN_TF), lambda e, f: (e, 0, f)),
                 pl.BlockSpec((None, FFN_TF, d), lambda e, f: (e, f, 0))]
    scratch += [pltpu.VMEM((d, FFN_TF), BF16), pltpu.VMEM((d, FFN_TF), BF16), pltpu.VMEM((FFN_TF, d), BF16)]
    return pl.pallas_call(
        functools.partial(_ffn_body, n_sets=len(sets)),
        grid=(n_exp, ff // FFN_TF),
        in_specs=in_specs, out_specs=out_specs, out_shape=out_shape, scratch_shapes=scratch,
        compiler_params=_cparams("parallel", "arbitrary"),
        name="expert_ffn",
    )(*args, wg, wu, wd)


def _scatter_body(post_ref, y_ref, h1_ref, mod_ref, fw_ref, o_ref, *, final):
    n_exp, cap, d = y_ref.shape
    pos = post_ref[...]
    slots = lax.broadcasted_iota(I32, (pos.shape[0], cap), 1).astype(F32)
    hit = jnp.concatenate([(pos[:, e:e+1] == slots).astype(BF16) for e in range(n_exp)], axis=1)
    upd = jnp.dot(hit, y_ref[...].reshape(n_exp * cap, d), preferred_element_type=F32)
    h2 = h1_ref[...] + mod_ref[5:6, :] * upd
    o_ref[...] = _rms(h2, fw_ref[...]) if final else h2


def _scatter_call(post, y, h1, mod_l, is_ctx, final_w):
    batch, t, d = h1.shape
    n_exp = y.shape[0]
    cap = y.shape[1] // batch
    tm = min(256, t)
    final = final_w is not None
    fw = final_w if final else jnp.ones((1, d), F32)
    return pl.pallas_call(
        functools.partial(_scatter_body, final=final),
        grid=(batch, t // tm),
        in_specs=[pl.BlockSpec((None, tm, SMALL_W), lambda b, i: (b, i, 0)),
                  pl.BlockSpec((n_exp, cap, d), lambda b, i: (0, b, 0)),
                  pl.BlockSpec((None, tm, d), lambda b, i: (b, i, 0)),
                  _mod_spec(is_ctx, batch),
                  pl.BlockSpec((1, d), lambda b, i: (0, 0))],
        out_specs=pl.BlockSpec((None, tm, d), lambda b, i: (b, i, 0)),
        out_shape=jax.ShapeDtypeStruct((batch, t, d), F32),
        compiler_params=_cparams("parallel", "parallel"),
        name="scatter_residual",
    )(post, y, h1, mod_l, fw)


def kernel(x, c, ctx, c_ctx, w_mod, b_mod, norm1_w, norm2_w, w_in, w_out, ret_decay_logit, ret_norm_w,
           gla_gate_w, gla_gate_b, gla_norm_w, ssd_conv_w, ssd_conv_b, ssd_dt_bias, ssd_a_log, ssd_d,
           ssd_norm_w, mlstm_gate_b, mlstm_norm_w, router_w, expert_w_gate, expert_w_up, expert_w_down,
           final_norm_w):
    batch, t_lat, d = x.shape
    t_ctx = ctx.shape[1]
    depth = w_mod.shape[0]
    assert batch + 1 <= ROW_PAD and d == D_MODEL
    assert t_lat % 256 == 0 and t_ctx % 128 == 0
    cs = jnp.concatenate([c, c_ctx[None], jnp.zeros((ROW_PAD - batch - 1, d), F32)], axis=0)
    mod = _mod_call(cs, w_mod, b_mod).reshape(depth, ROW_PAD, 6, d)
    cos, sin = _rope_tables(t_lat)
    cap_lat = EC_CAPACITY * t_lat // N_EXPERTS
    cap_ctx = EC_CAPACITY * t_ctx // N_EXPERTS
    h_lat, h_ctx = x, ctx
    for l in range(depth):
        ctx_out = l < depth - 1
        mod_l = mod[l]
        w_big = _arrange_w_in(w_in[l])
        ul = _in_call(h_lat, mod_l, norm1_w[l][None], w_big, False)
        uc = _in_call(h_ctx, mod_l, norm1_w[l][None], w_big, True)
        ys = [_ret_call(ul[0], uc[0], jax.nn.log_sigmoid(ret_decay_logit[l].astype(F32)), cos, sin,
                        ret_norm_w[l][None], ctx_out),
              _gla_call(ul[1], uc[1], ul[4], uc[4], gla_gate_w[l], gla_gate_b[l], gla_norm_w[l][None], ctx_out),
              _ssd_call(ul[2], uc[2], ul[4], uc[4], ssd_conv_w[l], ssd_conv_b[l], ssd_dt_bias[l], ssd_a_log[l],
                        ssd_d[l], ssd_norm_w[l][None], ctx_out),
              _mlstm_call(ul[3], uc[3], ul[4], uc[4], mlstm_gate_b[l], mlstm_norm_w[l][None], ctx_out)]
        wo = w_out[l].astype(BF16)
        rw3 = jnp.stack(_split3(jnp.pad(router_w[l], ((0, 0), (0, SMALL_W - N_EXPERTS)))))
        n2w = norm2_w[l][None]
        h1_lat, hn_lat, lt_lat = _out_call([y[0] for y in ys], h_lat, mod_l, wo, n2w, rw3, False)
        xg_lat, affc_lat, post_lat = _route_call(lt_lat, hn_lat, cap_lat)
        sets = [(xg_lat, affc_lat)]
        if ctx_out:
            h1_ctx, hn_ctx, lt_ctx = _out_call([y[1] for y in ys], h_ctx, mod_l, wo, n2w, rw3, True)
            xg_ctx, affc_ctx, post_ctx = _route_call(lt_ctx, hn_ctx, cap_ctx)
            sets.append((xg_ctx, affc_ctx))
        y_exp = _ffn_call(sets, expert_w_gate[l], expert_w_up[l], expert_w_down[l])
        h_lat = _scatter_call(post_lat, y_exp[0], h1_lat, mod_l, False,
                              None if ctx_out else final_norm_w[None])
        if ctx_out:
            h_ctx = _scatter_call(post_ctx, y_exp[1], h1_ctx, mod_l, True, None)
    return h_lat
```

```python
import functools
import math

import jax
import jax.numpy as jnp
from jax import lax
from jax.experimental import pallas as pl
from jax.experimental.pallas import tpu as pltpu

F32 = jnp.float32
BF16 = jnp.bfloat16
I32 = jnp.int32

D_MODEL = 1024
GRID_W = 64
EPS = 1e-6
ROPE_BASE = 10000.0
GROUP_W = 256
HEAD_DIM = 64
N_HEADS = 4
CHUNK = 64
GLA_DK = 32
GLA_QK = 128
GLA_RANK = 16
GLA_TAU = 16.0
GLA_SUB = 16
SSD_CONV = 5
SSD_BC = 128
N_EXPERTS = 16
EC_CAPACITY = 2
EXPERT_FF = 1536
SMALL_W = 128
LR_OFF, DT_OFF, GATE_OFF = 0, 16, 24
ROW_PAD = 16
VMEM_LIMIT = 56 * 1024 * 1024


def _cparams(*sem):
    return pltpu.CompilerParams(dimension_semantics=sem, vmem_limit_bytes=VMEM_LIMIT)


def _bdot(a, b):
    return jnp.dot(a.astype(BF16), b.astype(BF16), preferred_element_type=F32)


def _bdot_nt(a, b):
    return lax.dot_general(a.astype(BF16), b.astype(BF16), (((1,), (1,)), ((), ())),
                           preferred_element_type=F32)


def _bdot_tn(a, b):
    return lax.dot_general(a.astype(BF16), b.astype(BF16), (((0,), (0,)), ((), ())),
                           preferred_element_type=F32)


def _silu(x):
    return x * jax.nn.sigmoid(x)


def _softplus(x):
    return jnp.maximum(x, 0.0) + jnp.log1p(jnp.exp(-jnp.abs(x)))


def _log_sigmoid(x):
    return -_softplus(-x)


def _rms(x, w):
    return x * lax.rsqrt(jnp.mean(x * x, axis=-1, keepdims=True) + EPS) * w


def _mod_body(cs_ref, w_ref, b_ref, o_ref):
    o_ref[...] = _bdot(_silu(cs_ref[...]), w_ref[...]) + b_ref[...]


def _mod_call(cs, w_mod, b_mod):
    depth, d, n = w_mod.shape
    tn = 1536
    return pl.pallas_call(
        _mod_body,
        grid=(depth, n // tn),
        in_specs=[pl.BlockSpec((ROW_PAD, d), lambda l, j: (0, 0)),
                  pl.BlockSpec((None, d, tn), lambda l, j: (l, 0, j)),
                  pl.BlockSpec((None, 1, tn), lambda l, j: (l, 0, j))],
        out_specs=pl.BlockSpec((None, ROW_PAD, tn), lambda l, j: (l, 0, j)),
        out_shape=jax.ShapeDtypeStruct((depth, ROW_PAD, n), F32),
        compiler_params=_cparams("parallel", "parallel"),
        name="adaln",
    )(cs, w_mod, b_mod.reshape(depth, 1, n))


def _mod_spec(is_ctx, batch):
    if is_ctx:
        return pl.BlockSpec((None, 6, D_MODEL), lambda b, i: (batch, 0, 0))
    return pl.BlockSpec((None, 6, D_MODEL), lambda b, i: (b, 0, 0))


IN_WIDTHS = (1024, 768, 768, 1024, SMALL_W)


def _in_body(h_ref, mod_ref, nw_ref, w_ref, ret_ref, gla_ref, ssd_ref, mls_ref, sm_ref):
    mod = mod_ref[...]
    xn = _rms(h_ref[...], nw_ref[...]) * (1.0 + mod[1:2, :]) + mod[0:1, :]
    u = jnp.dot(xn.astype(BF16), w_ref[...], preferred_element_type=F32)
    off = 0
    for ref, w in zip((ret_ref, gla_ref, ssd_ref, mls_ref, sm_ref), IN_WIDTHS):
        ref[...] = u[:, off:off + w]
        off += w


def _in_call(h, mod_l, nw, w_big, is_ctx):
    batch, t, d = h.shape
    tm = min(512, t)
    n = w_big.shape[1]
    return pl.pallas_call(
        _in_body,
        grid=(batch, t // tm),
        in_specs=[pl.BlockSpec((None, tm, d), lambda b, i: (b, i, 0)),
                  _mod_spec(is_ctx, batch),
                  pl.BlockSpec((1, d), lambda b, i: (0, 0)),
                  pl.BlockSpec((d, n), lambda b, i: (0, 0))],
        out_specs=[pl.BlockSpec((None, tm, w), lambda b, i: (b, i, 0)) for w in IN_WIDTHS],
        out_shape=[jax.ShapeDtypeStruct((batch, t, w), F32) for w in IN_WIDTHS],
        compiler_params=_cparams("parallel", "parallel"),
        name="in_proj",
    )(h, mod_l, nw, w_big)


def _arrange_w_in(w):
    ret_end = 4 * GROUP_W
    gla_main = ret_end + 2 * GLA_QK + 2 * GROUP_W
    gla_end = gla_main + GLA_RANK
    ssd_main = gla_end + GROUP_W + GROUP_W + 2 * SSD_BC
    ssd_end = ssd_main + 2 * N_HEADS
    mls_main = ssd_end + 4 * GROUP_W
    mls_end = mls_main + 4 * N_HEADS
    narrow = (gla_end - gla_main) + (ssd_end - ssd_main) + (mls_end - mls_main)
    pad = jnp.zeros((w.shape[0], SMALL_W - narrow), w.dtype)
    return jnp.concatenate([w[:, :ret_end], w[:, ret_end:gla_main], w[:, gla_end:ssd_main],
                            w[:, ssd_end:mls_main], w[:, gla_main:gla_end], w[:, ssd_main:ssd_end],
                            w[:, mls_main:mls_end], pad], axis=1).astype(BF16)


def _split3(x):
    hi = x.astype(BF16)
    r1 = x - hi.astype(F32)
    mid = r1.astype(BF16)
    return hi, mid, (r1 - mid.astype(F32)).astype(BF16)


def _iota2(shape, axis):
    return lax.broadcasted_iota(I32, shape, axis)


def _block_consts(rev):
    i = _iota2((CHUNK, GROUP_W), 0)
    j = _iota2((CHUNK, GROUP_W), 1) % HEAD_DIM
    return ((j >= i) if rev else (j <= i)), (j == i)


def _same_head():
    return (_iota2((GROUP_W, GROUP_W), 0) // HEAD_DIM) == (_iota2((GROUP_W, GROUP_W), 1) // HEAD_DIM)


def _cum_mat(rev):
    i = _iota2((CHUNK, 3 * CHUNK), 0)
    t = _iota2((CHUNK, 3 * CHUNK), 1) % CHUNK
    return ((t >= i) if rev else (t <= i)).astype(F32).astype(BF16)


def _expand_mat(first_lane, width=GROUP_W):
    return (_iota2((SMALL_W, width), 0) == first_lane + _iota2((SMALL_W, width), 1) // HEAD_DIM
            ).astype(F32).astype(BF16)


def _spread(x, expand):
    n = x.shape[0]
    y = jnp.dot(jnp.concatenate(_split3(x), axis=0), expand, preferred_element_type=F32)
    return (y[0:n] + y[n:2 * n]) + y[2 * n:3 * n]


def _cumsum_b(x_b, cum3):
    return jnp.dot(cum3, jnp.concatenate(_split3(x_b), axis=0), preferred_element_type=F32)


def _tile_heads(x):
    return jnp.concatenate([x] * N_HEADS, axis=0)


def _blockdiag(x, same_head):
    xb = _tile_heads(x.astype(BF16))
    return jnp.where(same_head, xb, jnp.zeros_like(xb))


def _decay_step(q, k, v, g_b, rev, consts, s_ref, want_out):
    mask_b, eye_b, same_head = consts
    g_tot = g_b[0:1] if rev else g_b[CHUNK - 1:CHUNK]
    s_in = s_ref[...]
    o = None
    if want_out:
        g_row = jnp.sum(jnp.where(eye_b, g_b, 0.0), axis=0, keepdims=True)
        dec = jnp.exp(jnp.where(mask_b, g_b - g_row, -jnp.inf))
        scores = _bdot_nt(q, _blockdiag(k, same_head)) * dec
        o = _bdot(scores, _blockdiag(v, same_head)) + _bdot(q * jnp.exp(g_b), s_in)
    kv = _bdot_tn(k * jnp.exp(g_tot - g_b), v)
    s_ref[...] = jnp.exp(g_tot) * s_in + jnp.where(same_head, kv, 0.0)
    return o


def _seg_sum(x, same_head):
    seg = same_head.astype(F32).astype(BF16)
    hi = x.astype(BF16)
    lo = (x - hi.astype(F32)).astype(BF16)
    return jnp.dot(hi, seg, preferred_element_type=F32) + jnp.dot(lo, seg, preferred_element_type=F32)


def _head_rms(x, same_head, center):
    if center:
        x = x - _seg_sum(x, same_head) * (1.0 / HEAD_DIM)
    return x * lax.rsqrt(_seg_sum(x * x, same_head) * (1.0 / HEAD_DIM) + EPS)


def _scan_both(n_ctx, n_lat, step):
    def ctx_it(s, c):
        step(False, s, 0)
        step(False, n_ctx - 1 - s, 1)
        return c

    def lat_it(s, c):
        step(True, s, 0)
        step(True, n_lat - 1 - s, 1)
        return c

    lax.fori_loop(0, n_ctx, ctx_it, 0)
    lax.fori_loop(0, n_lat, lat_it, 0)


def _post_tiles(t):
    tile = min(256, t)
    return [(r, tile) for r in range(0, t, tile)]


def _mixer_call(body, name, batch, t_lat, t_ctx, ctx_out, per_sample, shared, scratch):
    in_specs = [pl.BlockSpec((None,) + a.shape[1:], lambda b, nd=a.ndim: (b,) + (0,) * (nd - 1))
                for a in per_sample]
    in_specs += [pl.BlockSpec(a.shape, lambda b, nd=a.ndim: (0,) * nd) for a in shared]
    args = list(per_sample) + list(shared)
    out_shape = [jax.ShapeDtypeStruct((batch, t_lat, GROUP_W), F32)]
    out_specs = [pl.BlockSpec((None, t_lat, GROUP_W), lambda b: (b, 0, 0))]
    if ctx_out:
        out_shape.append(jax.ShapeDtypeStruct((batch, t_ctx, GROUP_W), F32))
        out_specs.append(pl.BlockSpec((None, t_ctx, GROUP_W), lambda b: (b, 0, 0)))
    return pl.pallas_call(
        body, grid=(batch,), in_specs=in_specs, out_specs=out_specs, out_shape=out_shape,
        scratch_shapes=scratch, compiler_params=_cparams("parallel"), name=name,
    )(*args)


def _rope(x, cos, sin_signed, first_half):
    return x * cos + jnp.where(first_half, pltpu.roll(x, x.shape[1] - 16, 1), pltpu.roll(x, 16, 1)) * sin_signed


def _scan_consts():
    same_head = _same_head()
    return tuple(_block_consts(rev) + (same_head,) for rev in (False, True))


def _ret_body(ul_ref, uc_ref, lg_ref, cos_ref, sin_ref, nw_ref, *rest, ctx_out):
    if ctx_out:
        yl_ref, yc_ref, s_ref, accl_ref, accc_ref = rest
    else:
        yl_ref, s_ref, accl_ref, accc_ref = rest
        yc_ref = None
    t_lat, t_ctx = ul_ref.shape[0], uc_ref.shape[0]
    s_ref[...] = jnp.zeros_like(s_ref)
    accl_ref[...] = jnp.zeros_like(accl_ref)
    if ctx_out:
        accc_ref[...] = jnp.zeros_like(accc_ref)
    first_half = (_iota2((1, GROUP_W), 1) % 32) < 16
    consts = _scan_consts()
    row = _iota2((CHUNK, GROUP_W), 0).astype(F32)
    n_decays = (row + 1.0, CHUNK - row)

    def step(is_lat, c, d):
        u_ref, acc_ref = (ul_ref, accl_ref) if is_lat else (uc_ref, accc_ref)
        want_out = is_lat or ctx_out
        r = pl.multiple_of(c * CHUNK, CHUNK)
        q = u_ref[pl.ds(r, CHUNK), 0:GROUP_W] * (HEAD_DIM ** -0.5)
        k = u_ref[pl.ds(r, CHUNK), GROUP_W:2 * GROUP_W]
        v = u_ref[pl.ds(r, CHUNK), 2 * GROUP_W:3 * GROUP_W]
        if is_lat:
            cos, sin = cos_ref[pl.ds(r, CHUNK), :], sin_ref[pl.ds(r, CHUNK), :]
            q, k = _rope(q, cos, sin, first_half), _rope(k, cos, sin, first_half)
        o = _decay_step(q, k, v, n_decays[d] * lg_ref[d], d == 1, consts[d], s_ref.at[d], want_out)
        if want_out:
            acc_ref[pl.ds(r, CHUNK), :] += o

    _scan_both(t_ctx // CHUNK, t_lat // CHUNK, step)

    def post(u_ref, acc_ref, y_ref, t):
        for r, n in _post_tiles(t):
            o = _head_rms(acc_ref[r:r + n, :], consts[0][2], True)
            y_ref[r:r + n, :] = o * nw_ref[...] * _silu(u_ref[r:r + n, 3 * GROUP_W:4 * GROUP_W])

    post(ul_ref, accl_ref, yl_ref, t_lat)
    if ctx_out:
        post(uc_ref, accc_ref, yc_ref, t_ctx)


def _state_scratch(n, t_lat, t_ctx):
    return ([pltpu.VMEM((2, GROUP_W, GROUP_W), F32)] * n
            + [pltpu.VMEM((t_lat, GROUP_W), F32), pltpu.VMEM((t_ctx, GROUP_W), F32)])


def _ret_call(u_lat, u_ctx, log_gamma, cos, sin, nw, ctx_out):
    batch, t_lat, _ = u_lat.shape
    t_ctx = u_ctx.shape[1]
    lg = jnp.repeat(log_gamma, HEAD_DIM, axis=1)[:, None, :]
    return _mixer_call(functools.partial(_ret_body, ctx_out=ctx_out), "retention", batch, t_lat, t_ctx, ctx_out,
                       [u_lat, u_ctx], [lg, cos, sin, nw], _state_scratch(1, t_lat, t_ctx))


def _rope_tables(t_lat):
    nf = 16
    inv = ROPE_BASE ** (-jnp.arange(nf, dtype=F32) / nf)
    rows = jnp.repeat(jnp.arange(t_lat // GRID_W), GRID_W).astype(F32)
    cols = jnp.tile(jnp.arange(GRID_W), t_lat // GRID_W).astype(F32)
    ang_r = rows[:, None] * inv[None, :]
    ang_c = cols[:, None] * inv[None, :]
    cos_h = jnp.concatenate([jnp.cos(ang_r)] * 2 + [jnp.cos(ang_c)] * 2, axis=1)
    sin_h = jnp.concatenate([-jnp.sin(ang_r), jnp.sin(ang_r), -jnp.sin(ang_c), jnp.sin(ang_c)], axis=1)
    return jnp.tile(cos_h, (1, N_HEADS)), jnp.tile(sin_h, (1, N_HEADS))


def _ssd_body(ul_ref, uc_ref, sml_ref, smc_ref, cw_ref, cb_ref, dtb_ref, an_ref, dsk_ref, nw_ref, *rest,
              ctx_out):
    if ctx_out:
        yl_ref, yc_ref, s_ref, accl_ref, accc_ref, xl_ref, xc_ref, pad_ref = rest
    else:
        yl_ref, s_ref, accl_ref, accc_ref, xl_ref, xc_ref, pad_ref = rest
        yc_ref = None
    t_lat, t_ctx = ul_ref.shape[0], uc_ref.shape[0]
    conv_ch = GROUP_W + 2 * SSD_BC
    half = (SSD_CONV - 1) // 2
    halo = 8

    def conv(u_ref, x_ref, t):
        pad_ref[0:halo, :] = jnp.zeros((halo, conv_ch), F32)
        pad_ref[halo + t:2 * halo + t, :] = jnp.zeros((halo, conv_ch), F32)
        for r, n in _post_tiles(t):
            pad_ref[halo + r:halo + r + n, :] = u_ref[r:r + n, GROUP_W:GROUP_W + conv_ch]
        for r, n in _post_tiles(t):
            acc = jnp.zeros((n, conv_ch), F32) + cb_ref[...]
            for tap in range(SSD_CONV):
                lo = halo + r + tap - half
                acc = acc + pad_ref[lo:lo + n, :] * cw_ref[tap:tap + 1, :]
            act = _silu(acc)
            grp = [act[:, GROUP_W + m * HEAD_DIM:GROUP_W + (m + 1) * HEAD_DIM] for m in range(4)]
            x_ref[r:r + n, 0:GROUP_W] = act[:, 0:GROUP_W]
            x_ref[r:r + n, GROUP_W:2 * GROUP_W] = jnp.concatenate([grp[0], grp[0], grp[1], grp[1]], axis=1)
            x_ref[r:r + n, 2 * GROUP_W:3 * GROUP_W] = jnp.concatenate([grp[2], grp[2], grp[3], grp[3]], axis=1)

    conv(uc_ref, xc_ref, t_ctx)
    conv(ul_ref, xl_ref, t_lat)
    s_ref[...] = jnp.zeros_like(s_ref)
    accl_ref[...] = jnp.zeros_like(accl_ref)
    if ctx_out:
        accc_ref[...] = jnp.zeros_like(accc_ref)
    consts = _scan_consts()
    cum3 = (_cum_mat(False), _cum_mat(True))
    expand = (_expand_mat(DT_OFF), _expand_mat(DT_OFF + N_HEADS))

    def step(is_lat, c, d):
        x_ref, sm_ref, acc_ref = (xl_ref, sml_ref, accl_ref) if is_lat else (xc_ref, smc_ref, accc_ref)
        want_out = is_lat or ctx_out
        r = pl.multiple_of(c * CHUNK, CHUNK)
        dt = _spread(_softplus(sm_ref[pl.ds(r, CHUNK), :] + dtb_ref[...]), expand[d])
        g_b = _cumsum_b(dt * an_ref[d], cum3[d])
        xs = x_ref[pl.ds(r, CHUNK), 0:GROUP_W]
        bm = x_ref[pl.ds(r, CHUNK), GROUP_W:2 * GROUP_W]
        cm = x_ref[pl.ds(r, CHUNK), 2 * GROUP_W:3 * GROUP_W]
        o = _decay_step(cm, bm, xs * dt, g_b, d == 1, consts[d], s_ref.at[d], want_out)
        if want_out:
            acc_ref[pl.ds(r, CHUNK), :] += o

    _scan_both(t_ctx // CHUNK, t_lat // CHUNK, step)

    def post(u_ref, x_ref, acc_ref, y_ref, t):
        for r, n in _post_tiles(t):
            y = acc_ref[r:r + n, :] + dsk_ref[...] * x_ref[r:r + n, 0:GROUP_W]
            y_ref[r:r + n, :] = _rms(y * _silu(u_ref[r:r + n, 0:GROUP_W]), nw_ref[...])

    post(ul_ref, xl_ref, accl_ref, yl_ref, t_lat)
    if ctx_out:
        post(uc_ref, xc_ref, accc_ref, yc_ref, t_ctx)


def _ssd_call(u_lat, u_ctx, sm_lat, sm_ctx, conv_w, conv_b, dt_bias, a_log, d_skip, nw, ctx_out):
    batch, t_lat, _ = u_lat.shape
    t_ctx = u_ctx.shape[1]
    conv_ch = GROUP_W + 2 * SSD_BC
    dtb = jnp.zeros((SMALL_W,), F32).at[DT_OFF:DT_OFF + 2 * N_HEADS].set(dt_bias.reshape(-1))[None]
    a_neg = jnp.repeat(-jnp.exp(a_log.astype(F32)), HEAD_DIM, axis=1)[:, None, :]
    cw = jnp.concatenate([conv_w, jnp.zeros((8 - SSD_CONV, conv_ch), F32)], axis=0)
    dsk = jnp.repeat(d_skip, HEAD_DIM)[None]
    scratch = _state_scratch(1, t_lat, t_ctx) + [
        pltpu.VMEM((t_lat, 3 * GROUP_W), F32), pltpu.VMEM((t_ctx, 3 * GROUP_W), F32),
        pltpu.VMEM((t_lat + 16, conv_ch), F32)]
    return _mixer_call(functools.partial(_ssd_body, ctx_out=ctx_out), "ssd", batch, t_lat, t_ctx, ctx_out,
                       [u_lat, u_ctx, sm_lat, sm_ctx], [cw, conv_b[None], dtb, a_neg, dsk, nw], scratch)


def _mlstm_body(ul_ref, uc_ref, sml_ref, smc_ref, gb_ref, nw_ref, *rest, ctx_out):
    if ctx_out:
        yl_ref, yc_ref, c_ref, n_ref, m_ref, accl_ref, accc_ref = rest
    else:
        yl_ref, c_ref, n_ref, m_ref, accl_ref, accc_ref = rest
        yc_ref = None
    t_lat, t_ctx = ul_ref.shape[0], uc_ref.shape[0]
    c_ref[...] = jnp.zeros_like(c_ref)
    n_ref[...] = jnp.zeros_like(n_ref)
    m_ref[...] = jnp.zeros_like(m_ref)
    accl_ref[...] = jnp.zeros_like(accl_ref)
    if ctx_out:
        accc_ref[...] = jnp.zeros_like(accc_ref)
    consts = _scan_consts()
    same_head = consts[0][2]
    seg = same_head.astype(F32).astype(BF16)
    cum3 = (_cum_mat(False), _cum_mat(True))
    f_lane = tuple(GATE_OFF + 2 * N_HEADS * d + N_HEADS for d in (0, 1))
    expand_f = tuple(_expand_mat(f_lane[d]) for d in (0, 1))
    expand_fi = tuple(jnp.concatenate([expand_f[d], _expand_mat(f_lane[d] - N_HEADS)], axis=1) for d in (0, 1))
    lane = _iota2((1, SMALL_W), 1)
    is_f = (lane >= GATE_OFF) & ((lane - GATE_OFF) % (2 * N_HEADS) >= N_HEADS)
    i_to_f = (_iota2((SMALL_W, SMALL_W), 1) == _iota2((SMALL_W, SMALL_W), 0) + N_HEADS).astype(F32).astype(BF16)
    ones_b = jnp.ones((CHUNK, GROUP_W), BF16)

    def running_max(x, rev):
        row = _iota2(x.shape, 0)
        s = 1
        while s < CHUNK:
            shifted = pltpu.roll(x, CHUNK - s if rev else s, 0)
            valid = (row < CHUNK - s) if rev else (row >= s)
            x = jnp.maximum(x, jnp.where(valid, shifted, -jnp.inf))
            s *= 2
        return x

    def step(is_lat, c, d):
        u_ref, sm_ref, acc_ref = (ul_ref, sml_ref, accl_ref) if is_lat else (uc_ref, smc_ref, accc_ref)
        want_out = is_lat or ctx_out
        rev = d == 1
        mask_b, eye_b, _ = consts[d]
        r = pl.multiple_of(c * CHUNK, CHUNK)
        gates = sm_ref[pl.ds(r, CHUNK), :] + gb_ref[...]
        lf = _log_sigmoid(gates)
        both = _spread(jnp.where(is_f, lf, gates), expand_fi[d])
        lf_b, ig_b = both[:, 0:GROUP_W], both[:, GROUP_W:2 * GROUP_W]
        g_b = _cumsum_b(lf_b, cum3[d])
        g_tot = g_b[0:1] if rev else g_b[CHUNK - 1:CHUNK]
        q = u_ref[pl.ds(r, CHUNK), 0:GROUP_W]
        k = u_ref[pl.ds(r, CHUNK), GROUP_W:2 * GROUP_W] * (HEAD_DIM ** -0.5)
        v = u_ref[pl.ds(r, CHUNK), 2 * GROUP_W:3 * GROUP_W]
        c_in, n_in, m_in = c_ref[d], n_ref[d], m_ref[d]
        if want_out:
            run = running_max(_spread(gates, i_to_f) - _cumsum_b(lf, cum3[d]), rev)
            m_inter = g_b + m_in
            m_tot = jnp.maximum(m_inter, g_b + _spread(run, expand_f[d]))
            g_row = jnp.sum(jnp.where(eye_b, g_b, 0.0), axis=0, keepdims=True)
            ig_row = jnp.sum(jnp.where(eye_b, ig_b, 0.0), axis=0, keepdims=True)
            d_log = jnp.where(mask_b, g_b - g_row + ig_row, -jnp.inf)
            qk = _bdot_nt(q, _blockdiag(k, same_head)) * jnp.exp(d_log - m_tot)
            s_inter = jnp.exp(m_inter - m_tot)
            num = _bdot(qk, _blockdiag(v, same_head)) + s_inter * _bdot(q, c_in)
            den = _seg_sum(qk, same_head) + s_inter * _bdot(q, n_in)
            acc_ref[pl.ds(r, CHUNK), :] += num / jnp.maximum(jnp.abs(den), jnp.exp(-m_tot))
        w_end = g_tot - g_b + ig_b
        b_max = jnp.max(w_end, axis=0, keepdims=True)
        ke = k * jnp.exp(w_end - b_max)
        m_new = jnp.maximum(g_tot + m_in, b_max)
        old, new = jnp.exp(g_tot + m_in - m_new), jnp.exp(b_max - m_new)
        upd = _bdot_tn(ke, jnp.concatenate([v.astype(BF16), ones_b], axis=1))
        c_ref[d] = old * c_in + new * jnp.where(same_head, upd[:, 0:GROUP_W], 0.0)
        n_ref[d] = old * n_in + new * jnp.where(same_head, upd[:, GROUP_W:2 * GROUP_W], 0.0)
        m_ref[d] = m_new

    _scan_both(t_ctx // CHUNK, t_lat // CHUNK, step)

    def post(u_ref, acc_ref, y_ref, t):
        for r, n in _post_tiles(t):
            o = _head_rms(acc_ref[r:r + n, :], same_head, False)
            y_ref[r:r + n, :] = jax.nn.sigmoid(u_ref[r:r + n, 3 * GROUP_W:4 * GROUP_W]) * (o * nw_ref[...])

    post(ul_ref, accl_ref, yl_ref, t_lat)
    if ctx_out:
        post(uc_ref, accc_ref, yc_ref, t_ctx)


def _mlstm_call(u_lat, u_ctx, sm_lat, sm_ctx, gate_b, nw, ctx_out):
    batch, t_lat, _ = u_lat.shape
    t_ctx = u_ctx.shape[1]
    gb = jnp.zeros((SMALL_W,), F32).at[GATE_OFF:GATE_OFF + 4 * N_HEADS].set(gate_b.reshape(-1))[None]
    scratch = ([pltpu.VMEM((2, GROUP_W, GROUP_W), F32), pltpu.VMEM((2, GROUP_W, GROUP_W), F32),
                pltpu.VMEM((2, 1, GROUP_W), F32)]
               + [pltpu.VMEM((t_lat, GROUP_W), F32), pltpu.VMEM((t_ctx, GROUP_W), F32)])
    return _mixer_call(functools.partial(_mlstm_body, ctx_out=ctx_out), "mlstm", batch, t_lat, t_ctx, ctx_out,
                       [u_lat, u_ctx, sm_lat, sm_ctx], [gb, nw], scratch)


def _gla_body(ul_ref, uc_ref, sml_ref, smc_ref, gw_ref, gb_ref, nw_ref, *rest, ctx_out):
    if ctx_out:
        yl_ref, yc_ref, st_ref, accl_ref, accc_ref = rest
    else:
        yl_ref, st_ref, accl_ref, accc_ref = rest
        yc_ref = None
    t_lat, t_ctx = ul_ref.shape[0], uc_ref.shape[0]
    st_ref[...] = jnp.zeros_like(st_ref)
    accl_ref[...] = jnp.zeros_like(accl_ref)
    if ctx_out:
        accc_ref[...] = jnp.zeros_like(accc_ref)
    n_sub = CHUNK // GLA_SUB
    i = lax.broadcasted_iota(I32, (CHUNK, CHUNK), 0)
    j = lax.broadcasted_iota(I32, (CHUNK, CHUNK), 1)
    same = (i // GLA_SUB) == (j // GLA_SUB)
    lcum = ((same & (j <= i)).astype(F32).astype(BF16), (same & (j >= i)).astype(F32).astype(BF16))
    expand = ((lax.broadcasted_iota(I32, (GLA_QK, GROUP_W), 0) // GLA_DK)
              == (lax.broadcasted_iota(I32, (GLA_QK, GROUP_W), 1) // HEAD_DIM)).astype(F32).astype(BF16)
    blockdiag = ((lax.broadcasted_iota(I32, (GROUP_W, GLA_QK), 0) // HEAD_DIM)
                 == (lax.broadcasted_iota(I32, (GROUP_W, GLA_QK), 1) // GLA_DK)).astype(F32)
    sub_row = lax.broadcasted_iota(I32, (CHUNK, 1), 0) % GLA_SUB

    def step(is_lat, c, d):
        u_ref, sm_ref, acc_ref = (ul_ref, sml_ref, accl_ref) if is_lat else (uc_ref, smc_ref, accc_ref)
        want_out = is_lat or ctx_out
        r = pl.multiple_of(c * CHUNK, CHUNK)
        q = u_ref[pl.ds(r, CHUNK), 0:GLA_QK] * (GLA_DK ** -0.5)
        k = u_ref[pl.ds(r, CHUNK), GLA_QK:2 * GLA_QK]
        v = u_ref[pl.ds(r, CHUNK), 2 * GLA_QK:2 * GLA_QK + GROUP_W]
        lr = sm_ref[pl.ds(r, CHUNK), LR_OFF:LR_OFF + GLA_RANK]
        la = _log_sigmoid(_bdot(lr, gw_ref[d]) + gb_ref[d]) / GLA_TAU
        g = sum(jnp.dot(lcum[d], piece, preferred_element_type=F32) for piece in _split3(la))
        o = None
        if want_out:
            prods, vals = [], []
            for delta in range(GLA_SUB):
                if delta == 0:
                    k_s, g_s, v_s = k, g, v
                    e = jnp.ones_like(g)
                else:
                    sh = delta if d == 0 else CHUNK - delta
                    k_s, g_s, v_s = pltpu.roll(k, sh, 0), pltpu.roll(g, sh, 0), pltpu.roll(v, sh, 0)
                    valid = (sub_row >= delta) if d == 0 else (sub_row + delta < GLA_SUB)
                    e = jnp.exp(jnp.where(valid, g - g_s, -jnp.inf))
                prods.append(q * k_s * e)
                vals.append(v_s)
            p = jnp.concatenate(prods, axis=0)
            p_hi = p.astype(BF16)
            p_lo = (p - p_hi.astype(F32)).astype(BF16)
            w = (jnp.dot(p_hi, expand, preferred_element_type=F32)
                 + jnp.dot(p_lo, expand, preferred_element_type=F32))
            o = sum(w[n * CHUNK:(n + 1) * CHUNK] * vals[n] for n in range(GLA_SUB))
        o_inter = [None] * n_sub
        for a in (range(n_sub) if d == 0 else range(n_sub - 1, -1, -1)):
            rows = slice(a * GLA_SUB, (a + 1) * GLA_SUB)
            g_a = g[rows]
            g_end = g_a[GLA_SUB - 1:GLA_SUB] if d == 0 else g_a[0:1]
            st = st_ref[d]
            if want_out:
                o_inter[a] = _bdot_nt(q[rows] * jnp.exp(g_a), st)
            st_ref[d] = st * jnp.exp(g_end) + _bdot_tn(v[rows], k[rows] * jnp.exp(g_end - g_a)) * blockdiag
        if want_out:
            acc_ref[pl.ds(r, CHUNK), :] += o + jnp.concatenate(o_inter, axis=0)

    _scan_both(t_ctx // CHUNK, t_lat // CHUNK, step)

    def post(u_ref, acc_ref, y_ref, t):
        for r, n in _post_tiles(t):
            o = _head_rms(acc_ref[r:r + n, :], _same_head(), False)
            y_ref[r:r + n, :] = o * nw_ref[...] * _silu(u_ref[r:r + n, 2 * GLA_QK + GROUP_W:2 * GLA_QK + 2 * GROUP_W])

    post(ul_ref, accl_ref, yl_ref, t_lat)
    if ctx_out:
        post(uc_ref, accc_ref, yc_ref, t_ctx)


def _gla_call(u_lat, u_ctx, sm_lat, sm_ctx, gate_w, gate_b, nw, ctx_out):
    batch, t_lat, _ = u_lat.shape
    t_ctx = u_ctx.shape[1]
    scratch = [pltpu.VMEM((2, GROUP_W, GLA_QK), F32),
               pltpu.VMEM((t_lat, GROUP_W), F32), pltpu.VMEM((t_ctx, GROUP_W), F32)]
    return _mixer_call(functools.partial(_gla_body, ctx_out=ctx_out), "gla", batch, t_lat, t_ctx, ctx_out,
                       [u_lat, u_ctx, sm_lat, sm_ctx], [gate_w, gate_b[:, None, :], nw], scratch)


def _out_body(yr_ref, yg_ref, ys_ref, ym_ref, h_ref, mod_ref, wo_ref, n2_ref, rw_ref, h1_ref, hn_ref, lt_ref):
    mod = mod_ref[...]
    proj = sum(_bdot(y_ref[...], wo_ref[n * GROUP_W:(n + 1) * GROUP_W, :])
               for n, y_ref in enumerate((yr_ref, yg_ref, ys_ref, ym_ref)))
    h1 = h_ref[...] + mod[2:3, :] * proj
    h1_ref[...] = h1
    xn = _rms(h1, n2_ref[...]) * (1.0 + mod[4:5, :]) + mod[3:4, :]
    hn_ref[...] = xn.astype(BF16)
    x_hi, x_mid, x_lo = _split3(xn)
    w_hi, w_mid, w_lo = rw_ref[0], rw_ref[1], rw_ref[2]
    dot = functools.partial(jnp.dot, preferred_element_type=F32)
    logits = (dot(x_lo, w_hi) + dot(x_mid, w_mid) + dot(x_hi, w_lo)
              + dot(x_mid, w_hi) + dot(x_hi, w_mid) + dot(x_hi, w_hi))
    lt_ref[...] = logits.T


def _out_call(ys, h, mod_l, wo, n2w, rw3, is_ctx):
    batch, t, d = h.shape
    tm = min(512, t)
    y_spec = pl.BlockSpec((None, tm, GROUP_W), lambda b, i: (b, i, 0))
    return pl.pallas_call(
        _out_body,
        grid=(batch, t // tm),
        in_specs=[y_spec, y_spec, y_spec, y_spec,
                  pl.BlockSpec((None, tm, d), lambda b, i: (b, i, 0)),
                  _mod_spec(is_ctx, batch),
                  pl.BlockSpec((d, d), lambda b, i: (0, 0)),
                  pl.BlockSpec((1, d), lambda b, i: (0, 0)),
                  pl.BlockSpec(rw3.shape, lambda b, i: (0, 0, 0))],
        out_specs=[pl.BlockSpec((None, tm, d), lambda b, i: (b, i, 0)),
                   pl.BlockSpec((None, tm, d), lambda b, i: (b, i, 0)),
                   pl.BlockSpec((None, SMALL_W, tm), lambda b, i: (b, 0, i))],
        out_shape=[jax.ShapeDtypeStruct((batch, t, d), F32),
                   jax.ShapeDtypeStruct((batch, t, d), BF16),
                   jax.ShapeDtypeStruct((batch, SMALL_W, t), F32)],
        compiler_params=_cparams("parallel", "parallel"),
        name="out_proj",
    )(*ys, h, mod_l, wo, n2w, rw3)


def _excl_prefix(flags, t):
    blk = min(256, t)
    below = (lax.broadcasted_iota(I32, (blk, blk), 0) < lax.broadcasted_iota(I32, (blk, blk), 1))
    below = below.astype(F32).astype(BF16)
    outs = []
    carry = jnp.zeros((flags.shape[0], 1), F32)
    for n in range(t // blk):
        f = flags[:, n * blk:(n + 1) * blk]
        outs.append(jnp.dot(f.astype(BF16), below, preferred_element_type=F32) + carry)
        carry = carry + jnp.sum(f, axis=1, keepdims=True)
    return jnp.concatenate(outs, axis=1)


def _route_body(lt_ref, hn_ref, xg_ref, affc_ref, post_ref, pos_s, aff_s, *, cap):
    t = hn_ref.shape[0]
    logits = lt_ref[0:N_EXPERTS, :]
    ex = jnp.exp(logits - jnp.max(logits, axis=0, keepdims=True))
    aff = ex / jnp.sum(ex, axis=0, keepdims=True)
    def fix_bit(n, thr):
        cand = thr | (jnp.int32(1) << (30 - n))
        cnt = jnp.sum((aff >= pltpu.bitcast(cand, F32)).astype(F32), axis=1, keepdims=True)
        return jnp.where(cnt >= float(cap), cand, thr)

    thr = lax.fori_loop(0, 31, fix_bit, jnp.zeros((N_EXPERTS, 1), I32))
    above = (aff >= pltpu.bitcast(thr + 1, F32)).astype(F32)
    tied = (aff >= pltpu.bitcast(thr, F32)).astype(F32) - above
    need = float(cap) - jnp.sum(above, axis=1, keepdims=True)
    sel = above + tied * (_excl_prefix(tied, t) < need).astype(F32)
    pos = jnp.where(sel > 0.0, _excl_prefix(sel, t), -1.0)
    pos_s[...] = pos
    aff_s[...] = aff
    post_ref[...] = jnp.concatenate([pos, jnp.full((SMALL_W - N_EXPERTS, t), -1.0, F32)], axis=0).T
    slots = lax.broadcasted_iota(I32, (cap, t), 0).astype(F32)

    def gather(e, carry):
        hit = slots == pos_s[pl.ds(e, 1), :]
        xg_ref[e] = jnp.dot(hit.astype(BF16), hn_ref[...], preferred_element_type=F32).astype(BF16)
        a = jnp.sum(jnp.where(hit, aff_s[pl.ds(e, 1), :], 0.0), axis=1, keepdims=True)
        affc_ref[e] = jnp.broadcast_to(a, (cap, SMALL_W))
        return carry

    lax.fori_loop(0, N_EXPERTS, gather, 0)


def _route_call(lt, hn, cap):
    batch, t, d = hn.shape
    return pl.pallas_call(
        functools.partial(_route_body, cap=cap),
        grid=(batch,),
        in_specs=[pl.BlockSpec((None, SMALL_W, t), lambda b: (b, 0, 0)),
                  pl.BlockSpec((None, t, d), lambda b: (b, 0, 0))],
        out_specs=[pl.BlockSpec((N_EXPERTS, cap, d), lambda b: (0, b, 0)),
                   pl.BlockSpec((N_EXPERTS, cap, SMALL_W), lambda b: (0, b, 0)),
                   pl.BlockSpec((None, t, SMALL_W), lambda b: (b, 0, 0))],
        out_shape=[jax.ShapeDtypeStruct((N_EXPERTS, batch * cap, d), BF16),
                   jax.ShapeDtypeStruct((N_EXPERTS, batch * cap, SMALL_W), F32),
                   jax.ShapeDtypeStruct((batch, t, SMALL_W), F32)],
        scratch_shapes=[pltpu.VMEM((N_EXPERTS, t), F32), pltpu.VMEM((N_EXPERTS, t), F32)],
        compiler_params=_cparams("parallel"),
        name="route_gather",
    )(lt, hn)


FFN_TF = 512
FFN_TM = 512


def _ffn_body(*refs, n_sets):
    ins = refs[:2 * n_sets]
    wg_ref, wu_ref, wd_ref = refs[2 * n_sets:2 * n_sets + 3]
    outs = refs[2 * n_sets + 3:3 * n_sets + 3]
    accs = refs[3 * n_sets + 3:4 * n_sets + 3]
    wg_s, wu_s, wd_s = refs[4 * n_sets + 3:]
    f = pl.program_id(1)
    wg_s[...] = wg_ref[...].astype(BF16)
    wu_s[...] = wu_ref[...].astype(BF16)
    wd_s[...] = wd_ref[...].astype(BF16)
    dot = functools.partial(jnp.dot, preferred_element_type=F32)
    for s in range(n_sets):
        x_ref, a_ref, y_ref, acc_ref = ins[2 * s], ins[2 * s + 1], outs[s], accs[s]
        rows = x_ref.shape[0]

        @pl.when(f == 0)
        def _():
            acc_ref[...] = jnp.zeros_like(acc_ref)

        for r in range(0, rows, FFN_TM):
            n = min(FFN_TM, rows - r)
            x = x_ref[r:r + n, :]
            hid = _silu(dot(x, wg_s[...])) * dot(x, wu_s[...])
            acc_ref[r:r + n, :] += dot(hid.astype(BF16), wd_s[...])

        @pl.when(f == pl.num_programs(1) - 1)
        def _():
            for r in range(0, rows, FFN_TM):
                n = min(FFN_TM, rows - r)
                y_ref[r:r + n, :] = (acc_ref[r:r + n, :] * a_ref[r:r + n, 0:1]).astype(BF16)


def _ffn_call(sets, wg, wu, wd, layer):
    _, n_exp, d, ff = wg.shape
    in_specs, args, out_specs, out_shape, scratch = [], [], [], [], []
    for xg, affc in sets:
        m = xg.shape[1]
        in_specs += [pl.BlockSpec((None, m, d), lambda e, f: (e, 0, 0)),
                     pl.BlockSpec((None, m, SMALL_W), lambda e, f: (e, 0, 0))]
        args += [xg, affc]
        out_specs.append(pl.BlockSpec((None, m, d), lambda e, f: (e, 0, 0)))
        out_shape.append(jax.ShapeDtypeStruct((n_exp, m, d), BF16))
        scratch.append(pltpu.VMEM((m, d), F32))
    in_specs += [pl.BlockSpec((None, None, d, FFN_TF), lambda e, f: (layer, e, 0, f)),
                 pl.BlockSpec((None, None, d, FFN_TF), lambda e, f: (layer, e, 0, f)),
                 pl.BlockSpec((None, None, FFN_TF, d), lambda e, f: (layer, e, f, 0))]
    scratch += [pltpu.VMEM((d, FFN_TF), BF16), pltpu.VMEM((d, FFN_TF), BF16), pltpu.VMEM((FFN_TF, d), BF16)]
    return pl.pallas_call(
        functools.partial(_ffn_body, n_sets=len(sets)),
        grid=(n_exp, ff // FFN_TF),
        in_specs=in_specs, out_specs=out_specs, out_shape=out_shape, scratch_shapes=scratch,
        compiler_params=_cparams("parallel", "arbitrary"),
        name="expert_ffn",
    )(*args, wg, wu, wd)


def _scatter_body(post_ref, y_ref, h1_ref, mod_ref, fw_ref, o_ref, *, final):
    n_exp, cap, d = y_ref.shape
    pos = post_ref[...]
    slots = lax.broadcasted_iota(I32, (pos.shape[0], cap), 1).astype(F32)
    hit = jnp.concatenate([(pos[:, e:e+1] == slots).astype(BF16) for e in range(n_exp)], axis=1)
    upd = jnp.dot(hit, y_ref[...].reshape(n_exp * cap, d), preferred_element_type=F32)
    h2 = h1_ref[...] + mod_ref[5:6, :] * upd
    o_ref[...] = _rms(h2, fw_ref[...]) if final else h2


def _scatter_call(post, y, h1, mod_l, is_ctx, final_w):
    batch, t, d = h1.shape
    n_exp = y.shape[0]
    cap = y.shape[1] // batch
    tm = min(256, t)
    final = final_w is not None
    fw = final_w if final else jnp.ones((1, d), F32)
    return pl.pallas_call(
        functools.partial(_scatter_body, final=final),
        grid=(batch, t // tm),
        in_specs=[pl.BlockSpec((None, tm, SMALL_W), lambda b, i: (b, i, 0)),
                  pl.BlockSpec((n_exp, cap, d), lambda b, i: (0, b, 0)),
                  pl.BlockSpec((None, tm, d), lambda b, i: (b, i, 0)),
                  _mod_spec(is_ctx, batch),
                  pl.BlockSpec((1, d), lambda b, i: (0, 0))],
        out_specs=pl.BlockSpec((None, tm, d), lambda b, i: (b, i, 0)),
        out_shape=jax.ShapeDtypeStruct((batch, t, d), F32),
        compiler_params=_cparams("parallel", "parallel"),
        name="scatter_residual",
    )(post, y, h1, mod_l, fw)


def kernel(x, c, ctx, c_ctx, w_mod, b_mod, norm1_w, norm2_w, w_in, w_out, ret_decay_logit, ret_norm_w,
           gla_gate_w, gla_gate_b, gla_norm_w, ssd_conv_w, ssd_conv_b, ssd_dt_bias, ssd_a_log, ssd_d,
           ssd_norm_w, mlstm_gate_b, mlstm_norm_w, router_w, expert_w_gate, expert_w_up, expert_w_down,
           final_norm_w):
    batch, t_lat, d = x.shape
    t_ctx = ctx.shape[1]
    depth = w_mod.shape[0]
    assert batch + 1 <= ROW_PAD and d == D_MODEL
    assert t_lat % 256 == 0 and t_ctx % 128 == 0
    cs = jnp.concatenate([c, c_ctx[None], jnp.zeros((ROW_PAD - batch - 1, d), F32)], axis=0)
    mod = _mod_call(cs, w_mod, b_mod).reshape(depth, ROW_PAD, 6, d)
    cos, sin = _rope_tables(t_lat)
    cap_lat = EC_CAPACITY * t_lat // N_EXPERTS
    cap_ctx = EC_CAPACITY * t_ctx // N_EXPERTS
    h_lat, h_ctx = x, ctx
    for l in range(depth):
        ctx_out = l < depth - 1
        mod_l = mod[l]
        w_big = _arrange_w_in(w_in[l])
        ul = _in_call(h_lat, mod_l, norm1_w[l][None], w_big, False)
        uc = _in_call(h_ctx, mod_l, norm1_w[l][None], w_big, True)
        ys = [_ret_call(ul[0], uc[0], jax.nn.log_sigmoid(ret_decay_logit[l].astype(F32)), cos, sin,
                        ret_norm_w[l][None], ctx_out),
              _gla_call(ul[1], uc[1], ul[4], uc[4], gla_gate_w[l], gla_gate_b[l], gla_norm_w[l][None], ctx_out),
              _ssd_call(ul[2], uc[2], ul[4], uc[4], ssd_conv_w[l], ssd_conv_b[l], ssd_dt_bias[l], ssd_a_log[l],
                        ssd_d[l], ssd_norm_w[l][None], ctx_out),
              _mlstm_call(ul[3], uc[3], ul[4], uc[4], mlstm_gate_b[l], mlstm_norm_w[l][None], ctx_out)]
        wo = w_out[l].astype(BF16)
        rw3 = jnp.stack(_split3(jnp.pad(router_w[l], ((0, 0), (0, SMALL_W - N_EXPERTS)))))
        n2w = norm2_w[l][None]
        h1_lat, hn_lat, lt_lat = _out_call([y[0] for y in ys], h_lat, mod_l, wo, n2w, rw3, False)
        xg_lat, affc_lat, post_lat = _route_call(lt_lat, hn_lat, cap_lat)
        sets = [(xg_lat, affc_lat)]
        if ctx_out:
            h1_ctx, hn_ctx, lt_ctx = _out_call([y[1] for y in ys], h_ctx, mod_l, wo, n2w, rw3, True)
            xg_ctx, affc_ctx, post_ctx = _route_call(lt_ctx, hn_ctx, cap_ctx)
            sets.append((xg_ctx, affc_ctx))
        y_exp = _ffn_call(sets, expert_w_gate, expert_w_up, expert_w_down, l)
        h_lat = _scatter_call(post_lat, y_exp[0], h1_lat, mod_l, False,
                              None if ctx_out else final_norm_w[None])
        if ctx_out:
            h_ctx = _scatter_call(post_ctx, y_exp[1], h1_ctx, mod_l, True, None)
    return h_lat
```

```python
import functools
import math

import jax
import jax.numpy as jnp
from jax import lax
from jax.experimental import pallas as pl
from jax.experimental.pallas import tpu as pltpu

F32 = jnp.float32
BF16 = jnp.bfloat16
I32 = jnp.int32

D_MODEL = 1024
GRID_W = 64
EPS = 1e-6
ROPE_BASE = 10000.0
GROUP_W = 256
HEAD_DIM = 64
N_HEADS = 4
CHUNK = 64
GLA_DK = 32
GLA_QK = 128
GLA_RANK = 16
GLA_TAU = 16.0
GLA_SUB = 16
SSD_CONV = 5
SSD_BC = 128
N_EXPERTS = 16
EC_CAPACITY = 2
EXPERT_FF = 1536
SMALL_W = 128
LR_OFF, DT_OFF, GATE_OFF = 0, 16, 24
ROW_PAD = 16
VMEM_LIMIT = 56 * 1024 * 1024


def _cparams(*sem):
    return pltpu.CompilerParams(dimension_semantics=sem, vmem_limit_bytes=VMEM_LIMIT)


def _bdot(a, b):
    return jnp.dot(a.astype(BF16), b.astype(BF16), preferred_element_type=F32)


def _bdot_nt(a, b):
    return lax.dot_general(a.astype(BF16), b.astype(BF16), (((1,), (1,)), ((), ())),
                           preferred_element_type=F32)


def _bdot_tn(a, b):
    return lax.dot_general(a.astype(BF16), b.astype(BF16), (((0,), (0,)), ((), ())),
                           preferred_element_type=F32)


def _silu(x):
    return x * jax.nn.sigmoid(x)


def _softplus(x):
    return jnp.maximum(x, 0.0) + jnp.log1p(jnp.exp(-jnp.abs(x)))


def _log_sigmoid(x):
    return -_softplus(-x)


def _rms(x, w):
    return x * lax.rsqrt(jnp.mean(x * x, axis=-1, keepdims=True) + EPS) * w


def _mod_body(cs_ref, w_ref, b_ref, o_ref):
    o_ref[...] = _bdot(_silu(cs_ref[...]), w_ref[...]) + b_ref[...]


def _mod_call(cs, w_mod, b_mod):
    depth, d, n = w_mod.shape
    tn = 1536
    return pl.pallas_call(
        _mod_body,
        grid=(depth, n // tn),
        in_specs=[pl.BlockSpec((ROW_PAD, d), lambda l, j: (0, 0)),
                  pl.BlockSpec((None, d, tn), lambda l, j: (l, 0, j)),
                  pl.BlockSpec((None, 1, tn), lambda l, j: (l, 0, j))],
        out_specs=pl.BlockSpec((None, ROW_PAD, tn), lambda l, j: (l, 0, j)),
        out_shape=jax.ShapeDtypeStruct((depth, ROW_PAD, n), F32),
        compiler_params=_cparams("parallel", "parallel"),
        name="adaln",
    )(cs, w_mod, b_mod.reshape(depth, 1, n))


def _mod_spec(is_ctx, batch):
    if is_ctx:
        return pl.BlockSpec((None, 6, D_MODEL), lambda b, i: (batch, 0, 0))
    return pl.BlockSpec((None, 6, D_MODEL), lambda b, i: (b, 0, 0))


IN_WIDTHS = (1024, 768, 768, 1024, SMALL_W)


def _in_body(h_ref, mod_ref, nw_ref, w_ref, ret_ref, gla_ref, ssd_ref, mls_ref, sm_ref):
    mod = mod_ref[...]
    xn = _rms(h_ref[...], nw_ref[...]) * (1.0 + mod[1:2, :]) + mod[0:1, :]
    u = jnp.dot(xn.astype(BF16), w_ref[...], preferred_element_type=F32)
    off = 0
    for ref, w in zip((ret_ref, gla_ref, ssd_ref, mls_ref, sm_ref), IN_WIDTHS):
        ref[...] = u[:, off:off + w]
        off += w


def _in_call(h, mod_l, nw, w_big, is_ctx):
    batch, t, d = h.shape
    tm = min(512, t)
    n = w_big.shape[1]
    return pl.pallas_call(
        _in_body,
        grid=(batch, t // tm),
        in_specs=[pl.BlockSpec((None, tm, d), lambda b, i: (b, i, 0)),
                  _mod_spec(is_ctx, batch),
                  pl.BlockSpec((1, d), lambda b, i: (0, 0)),
                  pl.BlockSpec((d, n), lambda b, i: (0, 0))],
        out_specs=[pl.BlockSpec((None, tm, w), lambda b, i: (b, i, 0)) for w in IN_WIDTHS],
        out_shape=[jax.ShapeDtypeStruct((batch, t, w), F32) for w in IN_WIDTHS],
        compiler_params=_cparams("parallel", "parallel"),
        name="in_proj",
    )(h, mod_l, nw, w_big)


def _arrange_w_in(w):
    ret_end = 4 * GROUP_W
    gla_main = ret_end + 2 * GLA_QK + 2 * GROUP_W
    gla_end = gla_main + GLA_RANK
    ssd_main = gla_end + GROUP_W + GROUP_W + 2 * SSD_BC
    ssd_end = ssd_main + 2 * N_HEADS
    mls_main = ssd_end + 4 * GROUP_W
    mls_end = mls_main + 4 * N_HEADS
    narrow = (gla_end - gla_main) + (ssd_end - ssd_main) + (mls_end - mls_main)
    pad = jnp.zeros((w.shape[0], SMALL_W - narrow), w.dtype)
    return jnp.concatenate([w[:, :ret_end], w[:, ret_end:gla_main], w[:, gla_end:ssd_main],
                            w[:, ssd_end:mls_main], w[:, gla_main:gla_end], w[:, ssd_main:ssd_end],
                            w[:, mls_main:mls_end], pad], axis=1).astype(BF16)


def _split3(x):
    hi = x.astype(BF16)
    r1 = x - hi.astype(F32)
    mid = r1.astype(BF16)
    return hi, mid, (r1 - mid.astype(F32)).astype(BF16)


def _iota2(shape, axis):
    return lax.broadcasted_iota(I32, shape, axis)


def _block_consts(rev):
    i = _iota2((CHUNK, GROUP_W), 0)
    j = _iota2((CHUNK, GROUP_W), 1) % HEAD_DIM
    return ((j >= i) if rev else (j <= i)), (j == i)


def _same_head():
    return (_iota2((GROUP_W, GROUP_W), 0) // HEAD_DIM) == (_iota2((GROUP_W, GROUP_W), 1) // HEAD_DIM)


def _cum_mat(rev):
    i = _iota2((CHUNK, 3 * CHUNK), 0)
    t = _iota2((CHUNK, 3 * CHUNK), 1) % CHUNK
    return ((t >= i) if rev else (t <= i)).astype(F32).astype(BF16)


def _expand_mat(first_lane, width=GROUP_W):
    return (_iota2((SMALL_W, width), 0) == first_lane + _iota2((SMALL_W, width), 1) // HEAD_DIM
            ).astype(F32).astype(BF16)


def _spread(x, expand):
    n = x.shape[0]
    y = jnp.dot(jnp.concatenate(_split3(x), axis=0), expand, preferred_element_type=F32)
    return (y[0:n] + y[n:2 * n]) + y[2 * n:3 * n]


def _cumsum_b(x_b, cum3):
    return jnp.dot(cum3, jnp.concatenate(_split3(x_b), axis=0), preferred_element_type=F32)


def _tile_heads(x):
    return jnp.concatenate([x] * N_HEADS, axis=0)


def _blockdiag(x, same_head):
    xb = _tile_heads(x.astype(BF16))
    return jnp.where(same_head, xb, jnp.zeros_like(xb))


def _decay_step(q, k, v, g_b, rev, consts, s_ref, want_out):
    mask_b, eye_b, same_head = consts
    g_tot = g_b[0:1] if rev else g_b[CHUNK - 1:CHUNK]
    s_in = s_ref[...]
    o = None
    if want_out:
        g_row = jnp.sum(jnp.where(eye_b, g_b, 0.0), axis=0, keepdims=True)
        dec = jnp.exp(jnp.where(mask_b, g_b - g_row, -jnp.inf))
        scores = _bdot_nt(q, _blockdiag(k, same_head)) * dec
        o = _bdot(scores, _blockdiag(v, same_head)) + _bdot(q * jnp.exp(g_b), s_in)
    kv = _bdot_tn(k * jnp.exp(g_tot - g_b), v)
    s_ref[...] = jnp.exp(g_tot) * s_in + jnp.where(same_head, kv, 0.0)
    return o


def _seg_sum(x, same_head):
    seg = same_head.astype(F32).astype(BF16)
    hi = x.astype(BF16)
    lo = (x - hi.astype(F32)).astype(BF16)
    return jnp.dot(hi, seg, preferred_element_type=F32) + jnp.dot(lo, seg, preferred_element_type=F32)


def _head_rms(x, same_head, center):
    if center:
        x = x - _seg_sum(x, same_head) * (1.0 / HEAD_DIM)
    return x * lax.rsqrt(_seg_sum(x * x, same_head) * (1.0 / HEAD_DIM) + EPS)


def _scan_both(n_ctx, n_lat, step):
    def ctx_it(s, c):
        step(False, s, 0)
        step(False, n_ctx - 1 - s, 1)
        return c

    def lat_it(s, c):
        step(True, s, 0)
        step(True, n_lat - 1 - s, 1)
        return c

    lax.fori_loop(0, n_ctx, ctx_it, 0, unroll=2)
    lax.fori_loop(0, n_lat, lat_it, 0, unroll=2)


def _post_tiles(t):
    tile = min(256, t)
    return [(r, tile) for r in range(0, t, tile)]


def _mixer_call(body, name, batch, t_lat, t_ctx, ctx_out, per_sample, shared, scratch):
    in_specs = [pl.BlockSpec((None,) + a.shape[1:], lambda b, nd=a.ndim: (b,) + (0,) * (nd - 1))
                for a in per_sample]
    in_specs += [pl.BlockSpec(a.shape, lambda b, nd=a.ndim: (0,) * nd) for a in shared]
    args = list(per_sample) + list(shared)
    out_shape = [jax.ShapeDtypeStruct((batch, t_lat, GROUP_W), F32)]
    out_specs = [pl.BlockSpec((None, t_lat, GROUP_W), lambda b: (b, 0, 0))]
    if ctx_out:
        out_shape.append(jax.ShapeDtypeStruct((batch, t_ctx, GROUP_W), F32))
        out_specs.append(pl.BlockSpec((None, t_ctx, GROUP_W), lambda b: (b, 0, 0)))
    return pl.pallas_call(
        body, grid=(batch,), in_specs=in_specs, out_specs=out_specs, out_shape=out_shape,
        scratch_shapes=scratch, compiler_params=_cparams("parallel"), name=name,
    )(*args)


def _rope(x, cos, sin_signed, first_half):
    return x * cos + jnp.where(first_half, pltpu.roll(x, x.shape[1] - 16, 1), pltpu.roll(x, 16, 1)) * sin_signed


def _scan_consts():
    same_head = _same_head()
    return tuple(_block_consts(rev) + (same_head,) for rev in (False, True))


def _ret_body(ul_ref, uc_ref, lg_ref, cos_ref, sin_ref, nw_ref, *rest, ctx_out):
    if ctx_out:
        yl_ref, yc_ref, s_ref, accl_ref, accc_ref = rest
    else:
        yl_ref, s_ref, accl_ref, accc_ref = rest
        yc_ref = None
    t_lat, t_ctx = ul_ref.shape[0], uc_ref.shape[0]
    s_ref[...] = jnp.zeros_like(s_ref)
    accl_ref[...] = jnp.zeros_like(accl_ref)
    if ctx_out:
        accc_ref[...] = jnp.zeros_like(accc_ref)
    first_half = (_iota2((1, GROUP_W), 1) % 32) < 16
    consts = _scan_consts()
    row = _iota2((CHUNK, GROUP_W), 0).astype(F32)
    n_decays = (row + 1.0, CHUNK - row)

    def step(is_lat, c, d):
        u_ref, acc_ref = (ul_ref, accl_ref) if is_lat else (uc_ref, accc_ref)
        want_out = is_lat or ctx_out
        r = pl.multiple_of(c * CHUNK, CHUNK)
        q = u_ref[pl.ds(r, CHUNK), 0:GROUP_W] * (HEAD_DIM ** -0.5)
        k = u_ref[pl.ds(r, CHUNK), GROUP_W:2 * GROUP_W]
        v = u_ref[pl.ds(r, CHUNK), 2 * GROUP_W:3 * GROUP_W]
        if is_lat:
            cos, sin = cos_ref[pl.ds(r, CHUNK), :], sin_ref[pl.ds(r, CHUNK), :]
            q, k = _rope(q, cos, sin, first_half), _rope(k, cos, sin, first_half)
        o = _decay_step(q, k, v, n_decays[d] * lg_ref[d], d == 1, consts[d], s_ref.at[d], want_out)
        if want_out:
            acc_ref[pl.ds(r, CHUNK), :] += o

    _scan_both(t_ctx // CHUNK, t_lat // CHUNK, step)

    def post(u_ref, acc_ref, y_ref, t):
        for r, n in _post_tiles(t):
            o = _head_rms(acc_ref[r:r + n, :], consts[0][2], True)
            y_ref[r:r + n, :] = o * nw_ref[...] * _silu(u_ref[r:r + n, 3 * GROUP_W:4 * GROUP_W])

    post(ul_ref, accl_ref, yl_ref, t_lat)
    if ctx_out:
        post(uc_ref, accc_ref, yc_ref, t_ctx)


def _state_scratch(n, t_lat, t_ctx):
    return ([pltpu.VMEM((2, GROUP_W, GROUP_W), F32)] * n
            + [pltpu.VMEM((t_lat, GROUP_W), F32), pltpu.VMEM((t_ctx, GROUP_W), F32)])


def _ret_call(u_lat, u_ctx, log_gamma, cos, sin, nw, ctx_out):
    batch, t_lat, _ = u_lat.shape
    t_ctx = u_ctx.shape[1]
    lg = jnp.repeat(log_gamma, HEAD_DIM, axis=1)[:, None, :]
    return _mixer_call(functools.partial(_ret_body, ctx_out=ctx_out), "retention", batch, t_lat, t_ctx, ctx_out,
                       [u_lat, u_ctx], [lg, cos, sin, nw], _state_scratch(1, t_lat, t_ctx))


def _rope_tables(t_lat):
    nf = 16
    inv = ROPE_BASE ** (-jnp.arange(nf, dtype=F32) / nf)
    rows = jnp.repeat(jnp.arange(t_lat // GRID_W), GRID_W).astype(F32)
    cols = jnp.tile(jnp.arange(GRID_W), t_lat // GRID_W).astype(F32)
    ang_r = rows[:, None] * inv[None, :]
    ang_c = cols[:, None] * inv[None, :]
    cos_h = jnp.concatenate([jnp.cos(ang_r)] * 2 + [jnp.cos(ang_c)] * 2, axis=1)
    sin_h = jnp.concatenate([-jnp.sin(ang_r), jnp.sin(ang_r), -jnp.sin(ang_c), jnp.sin(ang_c)], axis=1)
    return jnp.tile(cos_h, (1, N_HEADS)), jnp.tile(sin_h, (1, N_HEADS))


def _ssd_body(ul_ref, uc_ref, sml_ref, smc_ref, cw_ref, cb_ref, dtb_ref, an_ref, dsk_ref, nw_ref, *rest,
              ctx_out):
    if ctx_out:
        yl_ref, yc_ref, s_ref, accl_ref, accc_ref, xl_ref, xc_ref, pad_ref = rest
    else:
        yl_ref, s_ref, accl_ref, accc_ref, xl_ref, xc_ref, pad_ref = rest
        yc_ref = None
    t_lat, t_ctx = ul_ref.shape[0], uc_ref.shape[0]
    conv_ch = GROUP_W + 2 * SSD_BC
    half = (SSD_CONV - 1) // 2
    halo = 8

    def conv(u_ref, x_ref, t):
        pad_ref[0:halo, :] = jnp.zeros((halo, conv_ch), F32)
        pad_ref[halo + t:2 * halo + t, :] = jnp.zeros((halo, conv_ch), F32)
        for r, n in _post_tiles(t):
            pad_ref[halo + r:halo + r + n, :] = u_ref[r:r + n, GROUP_W:GROUP_W + conv_ch]
        for r, n in _post_tiles(t):
            acc = jnp.zeros((n, conv_ch), F32) + cb_ref[...]
            for tap in range(SSD_CONV):
                lo = halo + r + tap - half
                acc = acc + pad_ref[lo:lo + n, :] * cw_ref[tap:tap + 1, :]
            act = _silu(acc)
            grp = [act[:, GROUP_W + m * HEAD_DIM:GROUP_W + (m + 1) * HEAD_DIM] for m in range(4)]
            x_ref[r:r + n, 0:GROUP_W] = act[:, 0:GROUP_W]
            x_ref[r:r + n, GROUP_W:2 * GROUP_W] = jnp.concatenate([grp[0], grp[0], grp[1], grp[1]], axis=1)
            x_ref[r:r + n, 2 * GROUP_W:3 * GROUP_W] = jnp.concatenate([grp[2], grp[2], grp[3], grp[3]], axis=1)

    conv(uc_ref, xc_ref, t_ctx)
    conv(ul_ref, xl_ref, t_lat)
    s_ref[...] = jnp.zeros_like(s_ref)
    accl_ref[...] = jnp.zeros_like(accl_ref)
    if ctx_out:
        accc_ref[...] = jnp.zeros_like(accc_ref)
    consts = _scan_consts()
    cum3 = (_cum_mat(False), _cum_mat(True))
    expand = (_expand_mat(DT_OFF), _expand_mat(DT_OFF + N_HEADS))

    def step(is_lat, c, d):
        x_ref, sm_ref, acc_ref = (xl_ref, sml_ref, accl_ref) if is_lat else (xc_ref, smc_ref, accc_ref)
        want_out = is_lat or ctx_out
        r = pl.multiple_of(c * CHUNK, CHUNK)
        dt = _spread(_softplus(sm_ref[pl.ds(r, CHUNK), :] + dtb_ref[...]), expand[d])
        g_b = _cumsum_b(dt * an_ref[d], cum3[d])
        xs = x_ref[pl.ds(r, CHUNK), 0:GROUP_W]
        bm = x_ref[pl.ds(r, CHUNK), GROUP_W:2 * GROUP_W]
        cm = x_ref[pl.ds(r, CHUNK), 2 * GROUP_W:3 * GROUP_W]
        o = _decay_step(cm, bm, xs * dt, g_b, d == 1, consts[d], s_ref.at[d], want_out)
        if want_out:
            acc_ref[pl.ds(r, CHUNK), :] += o

    _scan_both(t_ctx // CHUNK, t_lat // CHUNK, step)

    def post(u_ref, x_ref, acc_ref, y_ref, t):
        for r, n in _post_tiles(t):
            y = acc_ref[r:r + n, :] + dsk_ref[...] * x_ref[r:r + n, 0:GROUP_W]
            y_ref[r:r + n, :] = _rms(y * _silu(u_ref[r:r + n, 0:GROUP_W]), nw_ref[...])

    post(ul_ref, xl_ref, accl_ref, yl_ref, t_lat)
    if ctx_out:
        post(uc_ref, xc_ref, accc_ref, yc_ref, t_ctx)


def _ssd_call(u_lat, u_ctx, sm_lat, sm_ctx, conv_w, conv_b, dt_bias, a_log, d_skip, nw, ctx_out):
    batch, t_lat, _ = u_lat.shape
    t_ctx = u_ctx.shape[1]
    conv_ch = GROUP_W + 2 * SSD_BC
    dtb = jnp.zeros((SMALL_W,), F32).at[DT_OFF:DT_OFF + 2 * N_HEADS].set(dt_bias.reshape(-1))[None]
    a_neg = jnp.repeat(-jnp.exp(a_log.astype(F32)), HEAD_DIM, axis=1)[:, None, :]
    cw = jnp.concatenate([conv_w, jnp.zeros((8 - SSD_CONV, conv_ch), F32)], axis=0)
    dsk = jnp.repeat(d_skip, HEAD_DIM)[None]
    scratch = _state_scratch(1, t_lat, t_ctx) + [
        pltpu.VMEM((t_lat, 3 * GROUP_W), F32), pltpu.VMEM((t_ctx, 3 * GROUP_W), F32),
        pltpu.VMEM((t_lat + 16, conv_ch), F32)]
    return _mixer_call(functools.partial(_ssd_body, ctx_out=ctx_out), "ssd", batch, t_lat, t_ctx, ctx_out,
                       [u_lat, u_ctx, sm_lat, sm_ctx], [cw, conv_b[None], dtb, a_neg, dsk, nw], scratch)


def _mlstm_body(ul_ref, uc_ref, sml_ref, smc_ref, gb_ref, nw_ref, *rest, ctx_out):
    if ctx_out:
        yl_ref, yc_ref, c_ref, n_ref, m_ref, accl_ref, accc_ref = rest
    else:
        yl_ref, c_ref, n_ref, m_ref, accl_ref, accc_ref = rest
        yc_ref = None
    t_lat, t_ctx = ul_ref.shape[0], uc_ref.shape[0]
    c_ref[...] = jnp.zeros_like(c_ref)
    n_ref[...] = jnp.zeros_like(n_ref)
    m_ref[...] = jnp.zeros_like(m_ref)
    accl_ref[...] = jnp.zeros_like(accl_ref)
    if ctx_out:
        accc_ref[...] = jnp.zeros_like(accc_ref)
    consts = _scan_consts()
    same_head = consts[0][2]
    seg = same_head.astype(F32).astype(BF16)
    cum3 = (_cum_mat(False), _cum_mat(True))
    f_lane = tuple(GATE_OFF + 2 * N_HEADS * d + N_HEADS for d in (0, 1))
    expand_fi = tuple(jnp.concatenate([_expand_mat(f_lane[d]), _expand_mat(f_lane[d] - N_HEADS)], axis=1)
                      for d in (0, 1))
    lane = _iota2((1, SMALL_W), 1)
    is_f = (lane >= GATE_OFF) & ((lane - GATE_OFF) % (2 * N_HEADS) >= N_HEADS)
    ones_b = jnp.ones((CHUNK, GROUP_W), BF16)

    def running_max(x, rev):
        row = _iota2(x.shape, 0)
        s = 1
        while s < CHUNK:
            shifted = pltpu.roll(x, CHUNK - s if rev else s, 0)
            valid = (row < CHUNK - s) if rev else (row >= s)
            x = jnp.maximum(x, jnp.where(valid, shifted, -jnp.inf))
            s *= 2
        return x

    def step(is_lat, c, d):
        u_ref, sm_ref, acc_ref = (ul_ref, sml_ref, accl_ref) if is_lat else (uc_ref, smc_ref, accc_ref)
        want_out = is_lat or ctx_out
        rev = d == 1
        mask_b, eye_b, _ = consts[d]
        r = pl.multiple_of(c * CHUNK, CHUNK)
        gates = sm_ref[pl.ds(r, CHUNK), :] + gb_ref[...]
        lf = _log_sigmoid(gates)
        both = _spread(jnp.where(is_f, lf, gates), expand_fi[d])
        lf_b, ig_b = both[:, 0:GROUP_W], both[:, GROUP_W:2 * GROUP_W]
        g_b = _cumsum_b(lf_b, cum3[d])
        g_tot = g_b[0:1] if rev else g_b[CHUNK - 1:CHUNK]
        q = u_ref[pl.ds(r, CHUNK), 0:GROUP_W]
        k = u_ref[pl.ds(r, CHUNK), GROUP_W:2 * GROUP_W] * (HEAD_DIM ** -0.5)
        v = u_ref[pl.ds(r, CHUNK), 2 * GROUP_W:3 * GROUP_W]
        c_in, n_in, m_in = c_ref[d], n_ref[d], m_ref[d]
        if want_out:
            m_inter = g_b + m_in
            m_tot = jnp.maximum(m_inter, g_b + running_max(ig_b - g_b, rev))
            g_row = jnp.sum(jnp.where(eye_b, g_b, 0.0), axis=0, keepdims=True)
            ig_row = jnp.sum(jnp.where(eye_b, ig_b, 0.0), axis=0, keepdims=True)
            d_log = jnp.where(mask_b, g_b - g_row + ig_row, -jnp.inf)
            qk = _bdot_nt(q, _blockdiag(k, same_head)) * jnp.exp(d_log - m_tot)
            s_inter = jnp.exp(m_inter - m_tot)
            num = _bdot(qk, _blockdiag(v, same_head)) + s_inter * _bdot(q, c_in)
            den = _seg_sum(qk, same_head) + s_inter * _bdot(q, n_in)
            acc_ref[pl.ds(r, CHUNK), :] += num / jnp.maximum(jnp.abs(den), jnp.exp(-m_tot))
        w_end = g_tot - g_b + ig_b
        b_max = jnp.max(w_end, axis=0, keepdims=True)
        ke = k * jnp.exp(w_end - b_max)
        m_new = jnp.maximum(g_tot + m_in, b_max)
        old, new = jnp.exp(g_tot + m_in - m_new), jnp.exp(b_max - m_new)
        upd = _bdot_tn(ke, jnp.concatenate([v.astype(BF16), ones_b], axis=1))
        c_ref[d] = old * c_in + new * jnp.where(same_head, upd[:, 0:GROUP_W], 0.0)
        n_ref[d] = old * n_in + new * jnp.where(same_head, upd[:, GROUP_W:2 * GROUP_W], 0.0)
        m_ref[d] = m_new

    _scan_both(t_ctx // CHUNK, t_lat // CHUNK, step)

    def post(u_ref, acc_ref, y_ref, t):
        for r, n in _post_tiles(t):
            o = _head_rms(acc_ref[r:r + n, :], same_head, False)
            y_ref[r:r + n, :] = jax.nn.sigmoid(u_ref[r:r + n, 3 * GROUP_W:4 * GROUP_W]) * (o * nw_ref[...])

    post(ul_ref, accl_ref, yl_ref, t_lat)
    if ctx_out:
        post(uc_ref, accc_ref, yc_ref, t_ctx)


def _mlstm_call(u_lat, u_ctx, sm_lat, sm_ctx, gate_b, nw, ctx_out):
    batch, t_lat, _ = u_lat.shape
    t_ctx = u_ctx.shape[1]
    gb = jnp.zeros((SMALL_W,), F32).at[GATE_OFF:GATE_OFF + 4 * N_HEADS].set(gate_b.reshape(-1))[None]
    scratch = ([pltpu.VMEM((2, GROUP_W, GROUP_W), F32), pltpu.VMEM((2, GROUP_W, GROUP_W), F32),
                pltpu.VMEM((2, 1, GROUP_W), F32)]
               + [pltpu.VMEM((t_lat, GROUP_W), F32), pltpu.VMEM((t_ctx, GROUP_W), F32)])
    return _mixer_call(functools.partial(_mlstm_body, ctx_out=ctx_out), "mlstm", batch, t_lat, t_ctx, ctx_out,
                       [u_lat, u_ctx, sm_lat, sm_ctx], [gb, nw], scratch)


def _gla_body(ul_ref, uc_ref, sml_ref, smc_ref, gw_ref, gb_ref, nw_ref, *rest, ctx_out):
    if ctx_out:
        yl_ref, yc_ref, st_ref, accl_ref, accc_ref, gl_ref, gc_ref = rest
    else:
        yl_ref, st_ref, accl_ref, accc_ref, gl_ref, gc_ref = rest
        yc_ref = None
    t_lat, t_ctx = ul_ref.shape[0], uc_ref.shape[0]
    st_ref[...] = jnp.zeros_like(st_ref)
    n_sub = CHUNK // GLA_SUB
    i = lax.broadcasted_iota(I32, (CHUNK, CHUNK), 0)
    j = lax.broadcasted_iota(I32, (CHUNK, CHUNK), 1)
    same = (i // GLA_SUB) == (j // GLA_SUB)
    lcum = ((same & (j <= i)).astype(F32).astype(BF16), (same & (j >= i)).astype(F32).astype(BF16))
    expand = ((lax.broadcasted_iota(I32, (GLA_QK, GROUP_W), 0) // GLA_DK)
              == (lax.broadcasted_iota(I32, (GLA_QK, GROUP_W), 1) // HEAD_DIM)).astype(F32).astype(BF16)
    blockdiag = ((lax.broadcasted_iota(I32, (GROUP_W, GLA_QK), 0) // HEAD_DIM)
                 == (lax.broadcasted_iota(I32, (GROUP_W, GLA_QK), 1) // GLA_DK)).astype(F32)
    sub_row = lax.broadcasted_iota(I32, (CHUNK, 1), 0) % GLA_SUB

    def roll_sub(x, delta):
        return jnp.concatenate([pltpu.roll(x[a * GLA_SUB:(a + 1) * GLA_SUB], delta, 0) for a in range(n_sub)],
                               axis=0)

    def qkv(u_ref, r):
        return (u_ref[pl.ds(r, CHUNK), 0:GLA_QK] * (GLA_DK ** -0.5), u_ref[pl.ds(r, CHUNK), GLA_QK:2 * GLA_QK],
                u_ref[pl.ds(r, CHUNK), 2 * GLA_QK:2 * GLA_QK + GROUP_W])

    def intra(u_ref, sm_ref, g_ref, acc_ref, want_out, c):
        r = pl.multiple_of(c * CHUNK, CHUNK)
        lr = sm_ref[pl.ds(r, CHUNK), LR_OFF:LR_OFF + GLA_RANK]
        g = []
        for d in (0, 1):
            la = _log_sigmoid(_bdot(lr, gw_ref[d]) + gb_ref[d]) / GLA_TAU
            g.append(sum(jnp.dot(lcum[d], piece, preferred_element_type=F32) for piece in _split3(la)))
            g_ref[d, pl.ds(r, CHUNK), :] = g[d]
        if not want_out:
            return
        q, k, v = qkv(u_ref, r)
        prods, vals = [2.0 * q * k], [v]
        for delta in range(1, GLA_SUB):
            fwd_pair = sub_row >= delta
            e = jnp.exp(jnp.where(fwd_pair, g[0] - roll_sub(g[0], delta), g[1] - roll_sub(g[1], delta)))
            prods.append(q * roll_sub(k, delta) * e)
            vals.append(roll_sub(v, delta))
        w = jnp.dot(jnp.concatenate(prods, axis=0).astype(BF16), expand, preferred_element_type=F32)
        acc_ref[pl.ds(r, CHUNK), :] = sum(w[n * CHUNK:(n + 1) * CHUNK] * vals[n] for n in range(GLA_SUB))

    def intra_ctx(c, carry):
        intra(uc_ref, smc_ref, gc_ref, accc_ref, ctx_out, c)
        return carry

    def intra_lat(c, carry):
        intra(ul_ref, sml_ref, gl_ref, accl_ref, True, c)
        return carry

    lax.fori_loop(0, t_ctx // CHUNK, intra_ctx, 0)
    lax.fori_loop(0, t_lat // CHUNK, intra_lat, 0, unroll=2)

    def step(is_lat, c, d):
        u_ref, g_ref, acc_ref = (ul_ref, gl_ref, accl_ref) if is_lat else (uc_ref, gc_ref, accc_ref)
        want_out = is_lat or ctx_out
        r = pl.multiple_of(c * CHUNK, CHUNK)
        q, k, v = qkv(u_ref, r)
        g = g_ref[d, pl.ds(r, CHUNK), :]
        o_inter = [None] * n_sub
        for a in (range(n_sub) if d == 0 else range(n_sub - 1, -1, -1)):
            rows = slice(a * GLA_SUB, (a + 1) * GLA_SUB)
            g_a = g[rows]
            g_end = g_a[GLA_SUB - 1:GLA_SUB] if d == 0 else g_a[0:1]
            st = st_ref[d]
            if want_out:
                o_inter[a] = _bdot_nt(q[rows] * jnp.exp(g_a), st)
            st_ref[d] = st * jnp.exp(g_end) + _bdot_tn(v[rows], k[rows] * jnp.exp(g_end - g_a)) * blockdiag
        if want_out:
            acc_ref[pl.ds(r, CHUNK), :] += jnp.concatenate(o_inter, axis=0)

    _scan_both(t_ctx // CHUNK, t_lat // CHUNK, step)

    def post(u_ref, acc_ref, y_ref, t):
        for r, n in _post_tiles(t):
            o = _head_rms(acc_ref[r:r + n, :], _same_head(), False)
            y_ref[r:r + n, :] = o * nw_ref[...] * _silu(u_ref[r:r + n, 2 * GLA_QK + GROUP_W:2 * GLA_QK + 2 * GROUP_W])

    post(ul_ref, accl_ref, yl_ref, t_lat)
    if ctx_out:
        post(uc_ref, accc_ref, yc_ref, t_ctx)


def _gla_call(u_lat, u_ctx, sm_lat, sm_ctx, gate_w, gate_b, nw, ctx_out):
    batch, t_lat, _ = u_lat.shape
    t_ctx = u_ctx.shape[1]
    scratch = [pltpu.VMEM((2, GROUP_W, GLA_QK), F32),
               pltpu.VMEM((t_lat, GROUP_W), F32), pltpu.VMEM((t_ctx, GROUP_W), F32),
               pltpu.VMEM((2, t_lat, GLA_QK), F32), pltpu.VMEM((2, t_ctx, GLA_QK), F32)]
    return _mixer_call(functools.partial(_gla_body, ctx_out=ctx_out), "gla", batch, t_lat, t_ctx, ctx_out,
                       [u_lat, u_ctx, sm_lat, sm_ctx], [gate_w, gate_b[:, None, :], nw], scratch)


def _out_body(yr_ref, yg_ref, ys_ref, ym_ref, h_ref, mod_ref, wo_ref, n2_ref, rw_ref, h1_ref, hn_ref, lt_ref):
    mod = mod_ref[...]
    proj = sum(_bdot(y_ref[...], wo_ref[n * GROUP_W:(n + 1) * GROUP_W, :])
               for n, y_ref in enumerate((yr_ref, yg_ref, ys_ref, ym_ref)))
    h1 = h_ref[...] + mod[2:3, :] * proj
    h1_ref[...] = h1
    xn = _rms(h1, n2_ref[...]) * (1.0 + mod[4:5, :]) + mod[3:4, :]
    hn_ref[...] = xn.astype(BF16)
    x_hi, x_mid, _ = _split3(xn)
    parts = (jnp.dot(x_hi, rw_ref[...], preferred_element_type=F32)
             + jnp.dot(x_mid, rw_ref[...], preferred_element_type=F32)).T
    lt_ref[...] = parts[0:N_EXPERTS] + parts[N_EXPERTS:2 * N_EXPERTS] + parts[2 * N_EXPERTS:3 * N_EXPERTS]


def _out_call(ys, h, mod_l, wo, n2w, rw3, is_ctx):
    batch, t, d = h.shape
    tm = min(512, t)
    y_spec = pl.BlockSpec((None, tm, GROUP_W), lambda b, i: (b, i, 0))
    return pl.pallas_call(
        _out_body,
        grid=(batch, t // tm),
        in_specs=[y_spec, y_spec, y_spec, y_spec,
                  pl.BlockSpec((None, tm, d), lambda b, i: (b, i, 0)),
                  _mod_spec(is_ctx, batch),
                  pl.BlockSpec((d, d), lambda b, i: (0, 0)),
                  pl.BlockSpec((1, d), lambda b, i: (0, 0)),
                  pl.BlockSpec(rw3.shape, lambda b, i: (0, 0))],
        out_specs=[pl.BlockSpec((None, tm, d), lambda b, i: (b, i, 0)),
                   pl.BlockSpec((None, tm, d), lambda b, i: (b, i, 0)),
                   pl.BlockSpec((None, N_EXPERTS, tm), lambda b, i: (b, 0, i))],
        out_shape=[jax.ShapeDtypeStruct((batch, t, d), F32),
                   jax.ShapeDtypeStruct((batch, t, d), BF16),
                   jax.ShapeDtypeStruct((batch, N_EXPERTS, t), F32)],
        compiler_params=_cparams("parallel", "parallel"),
        name="out_proj",
    )(*ys, h, mod_l, wo, n2w, rw3)


def _excl_prefix(flags, t):
    blk = min(256, t)
    below = (lax.broadcasted_iota(I32, (blk, blk), 0) < lax.broadcasted_iota(I32, (blk, blk), 1))
    below = below.astype(F32).astype(BF16)
    outs = []
    carry = jnp.zeros((flags.shape[0], 1), F32)
    for n in range(t // blk):
        f = flags[:, n * blk:(n + 1) * blk]
        outs.append(jnp.dot(f.astype(BF16), below, preferred_element_type=F32) + carry)
        carry = carry + jnp.sum(f, axis=1, keepdims=True)
    return jnp.concatenate(outs, axis=1)


def _route_body(lt_ref, hn_ref, xg_ref, affc_ref, post_ref, pos_s, aff_s, *, cap):
    t = hn_ref.shape[0]
    logits = lt_ref[...]
    ex = jnp.exp(logits - jnp.max(logits, axis=0, keepdims=True))
    aff = ex / jnp.sum(ex, axis=0, keepdims=True)
    def fix_bit(n, thr):
        cand = thr | (jnp.int32(1) << (30 - n))
        cnt = jnp.sum((aff >= pltpu.bitcast(cand, F32)).astype(F32), axis=1, keepdims=True)
        return jnp.where(cnt >= float(cap), cand, thr)

    thr = lax.fori_loop(0, 31, fix_bit, jnp.zeros((N_EXPERTS, 1), I32))
    above = (aff >= pltpu.bitcast(thr + 1, F32)).astype(F32)
    tied = (aff >= pltpu.bitcast(thr, F32)).astype(F32) - above
    need = float(cap) - jnp.sum(above, axis=1, keepdims=True)
    sel = above + tied * (_excl_prefix(tied, t) < need).astype(F32)
    pos = jnp.where(sel > 0.0, _excl_prefix(sel, t), -1.0)
    pos_s[...] = pos
    aff_s[...] = aff
    post_ref[...] = jnp.concatenate([pos, jnp.full((SMALL_W - N_EXPERTS, t), -1.0, F32)], axis=0).T
    slots = lax.broadcasted_iota(I32, (cap, t), 0).astype(F32)

    def gather(e, carry):
        hit = slots == pos_s[pl.ds(e, 1), :]
        xg_ref[e] = jnp.dot(hit.astype(BF16), hn_ref[...], preferred_element_type=F32).astype(BF16)
        a = jnp.sum(jnp.where(hit, aff_s[pl.ds(e, 1), :], 0.0), axis=1, keepdims=True)
        affc_ref[e] = jnp.broadcast_to(a, (cap, SMALL_W))
        return carry

    lax.fori_loop(0, N_EXPERTS, gather, 0)


def _route_call(lt, hn, cap):
    batch, t, d = hn.shape
    return pl.pallas_call(
        functools.partial(_route_body, cap=cap),
        grid=(batch,),
        in_specs=[pl.BlockSpec((None, N_EXPERTS, t), lambda b: (b, 0, 0)),
                  pl.BlockSpec((None, t, d), lambda b: (b, 0, 0))],
        out_specs=[pl.BlockSpec((N_EXPERTS, cap, d), lambda b: (0, b, 0)),
                   pl.BlockSpec((N_EXPERTS, cap, SMALL_W), lambda b: (0, b, 0)),
                   pl.BlockSpec((None, t, SMALL_W), lambda b: (b, 0, 0))],
        out_shape=[jax.ShapeDtypeStruct((N_EXPERTS, batch * cap, d), BF16),
                   jax.ShapeDtypeStruct((N_EXPERTS, batch * cap, SMALL_W), F32),
                   jax.ShapeDtypeStruct((batch, t, SMALL_W), F32)],
        scratch_shapes=[pltpu.VMEM((N_EXPERTS, t), F32), pltpu.VMEM((N_EXPERTS, t), F32)],
        compiler_params=_cparams("parallel"),
        name="route_gather",
    )(lt, hn)


FFN_TF = 512
FFN_TM = 512


def _ffn_body(*refs, n_sets):
    ins = refs[:2 * n_sets]
    wg_ref, wu_ref, wd_ref = refs[2 * n_sets:2 * n_sets + 3]
    outs = refs[2 * n_sets + 3:3 * n_sets + 3]
    accs = refs[3 * n_sets + 3:4 * n_sets + 3]
    wg_s, wu_s, wd_s = refs[4 * n_sets + 3:]
    f = pl.program_id(1)
    wg_s[...] = wg_ref[...].astype(BF16)
    wu_s[...] = wu_ref[...].astype(BF16)
    wd_s[...] = wd_ref[...].astype(BF16)
    dot = functools.partial(jnp.dot, preferred_element_type=F32)
    for s in range(n_sets):
        x_ref, a_ref, y_ref, acc_ref = ins[2 * s], ins[2 * s + 1], outs[s], accs[s]
        rows = x_ref.shape[0]

        @pl.when(f == 0)
        def _():
            acc_ref[...] = jnp.zeros_like(acc_ref)

        for r in range(0, rows, FFN_TM):
            n = min(FFN_TM, rows - r)
            x = x_ref[r:r + n, :]
            hid = _silu(dot(x, wg_s[...])) * dot(x, wu_s[...])
            acc_ref[r:r + n, :] += dot(hid.astype(BF16), wd_s[...])

        @pl.when(f == pl.num_programs(1) - 1)
        def _():
            for r in range(0, rows, FFN_TM):
                n = min(FFN_TM, rows - r)
                y_ref[r:r + n, :] = (acc_ref[r:r + n, :] * a_ref[r:r + n, 0:1]).astype(BF16)


def _ffn_call(sets, wg, wu, wd, layer):
    _, n_exp, d, ff = wg.shape
    in_specs, args, out_specs, out_shape, scratch = [], [], [], [], []
    for xg, affc in sets:
        m = xg.shape[1]
        in_specs += [pl.BlockSpec((None, m, d), lambda e, f: (e, 0, 0)),
                     pl.BlockSpec((None, m, SMALL_W), lambda e, f: (e, 0, 0))]
        args += [xg, affc]
        out_specs.append(pl.BlockSpec((None, m, d), lambda e, f: (e, 0, 0)))
        out_shape.append(jax.ShapeDtypeStruct((n_exp, m, d), BF16))
        scratch.append(pltpu.VMEM((m, d), F32))
    in_specs += [pl.BlockSpec((None, None, d, FFN_TF), lambda e, f: (layer, e, 0, f)),
                 pl.BlockSpec((None, None, d, FFN_TF), lambda e, f: (layer, e, 0, f)),
                 pl.BlockSpec((None, None, FFN_TF, d), lambda e, f: (layer, e, f, 0))]
    scratch += [pltpu.VMEM((d, FFN_TF), BF16), pltpu.VMEM((d, FFN_TF), BF16), pltpu.VMEM((FFN_TF, d), BF16)]
    return pl.pallas_call(
        functools.partial(_ffn_body, n_sets=len(sets)),
        grid=(n_exp, ff // FFN_TF),
        in_specs=in_specs, out_specs=out_specs, out_shape=out_shape, scratch_shapes=scratch,
        compiler_params=_cparams("parallel", "arbitrary"),
        name="expert_ffn",
    )(*args, wg, wu, wd)


def _scatter_body(post_ref, y_ref, h1_ref, mod_ref, fw_ref, o_ref, *, final):
    n_exp, cap, d = y_ref.shape
    pos = post_ref[...]
    slots = lax.broadcasted_iota(I32, (pos.shape[0], cap), 1).astype(F32)
    hit = jnp.concatenate([(pos[:, e:e+1] == slots).astype(BF16) for e in range(n_exp)], axis=1)
    upd = jnp.dot(hit, y_ref[...].reshape(n_exp * cap, d), preferred_element_type=F32)
    h2 = h1_ref[...] + mod_ref[5:6, :] * upd
    o_ref[...] = _rms(h2, fw_ref[...]) if final else h2


def _scatter_call(post, y, h1, mod_l, is_ctx, final_w):
    batch, t, d = h1.shape
    n_exp = y.shape[0]
    cap = y.shape[1] // batch
    tm = min(256, t)
    final = final_w is not None
    fw = final_w if final else jnp.ones((1, d), F32)
    return pl.pallas_call(
        functools.partial(_scatter_body, final=final),
        grid=(batch, t // tm),
        in_specs=[pl.BlockSpec((None, tm, SMALL_W), lambda b, i: (b, i, 0)),
                  pl.BlockSpec((n_exp, cap, d), lambda b, i: (0, b, 0)),
                  pl.BlockSpec((None, tm, d), lambda b, i: (b, i, 0)),
                  _mod_spec(is_ctx, batch),
                  pl.BlockSpec((1, d), lambda b, i: (0, 0))],
        out_specs=pl.BlockSpec((None, tm, d), lambda b, i: (b, i, 0)),
        out_shape=jax.ShapeDtypeStruct((batch, t, d), F32),
        compiler_params=_cparams("parallel", "parallel"),
        name="scatter_residual",
    )(post, y, h1, mod_l, fw)


def kernel(x, c, ctx, c_ctx, w_mod, b_mod, norm1_w, norm2_w, w_in, w_out, ret_decay_logit, ret_norm_w,
           gla_gate_w, gla_gate_b, gla_norm_w, ssd_conv_w, ssd_conv_b, ssd_dt_bias, ssd_a_log, ssd_d,
           ssd_norm_w, mlstm_gate_b, mlstm_norm_w, router_w, expert_w_gate, expert_w_up, expert_w_down,
           final_norm_w):
    batch, t_lat, d = x.shape
    t_ctx = ctx.shape[1]
    depth = w_mod.shape[0]
    assert batch + 1 <= ROW_PAD and d == D_MODEL
    assert t_lat % 256 == 0 and t_ctx % 128 == 0
    cs = jnp.concatenate([c, c_ctx[None], jnp.zeros((ROW_PAD - batch - 1, d), F32)], axis=0)
    mod = _mod_call(cs, w_mod, b_mod).reshape(depth, ROW_PAD, 6, d)
    cos, sin = _rope_tables(t_lat)
    cap_lat = EC_CAPACITY * t_lat // N_EXPERTS
    cap_ctx = EC_CAPACITY * t_ctx // N_EXPERTS
    h_lat, h_ctx = x, ctx
    for l in range(depth):
        ctx_out = l < depth - 1
        mod_l = mod[l]
        w_big = _arrange_w_in(w_in[l])
        ul = _in_call(h_lat, mod_l, norm1_w[l][None], w_big, False)
        uc = _in_call(h_ctx, mod_l, norm1_w[l][None], w_big, True)
        ys = [_ret_call(ul[0], uc[0], jax.nn.log_sigmoid(ret_decay_logit[l].astype(F32)), cos, sin,
                        ret_norm_w[l][None], ctx_out),
              _gla_call(ul[1], uc[1], ul[4], uc[4], gla_gate_w[l], gla_gate_b[l], gla_norm_w[l][None], ctx_out),
              _ssd_call(ul[2], uc[2], ul[4], uc[4], ssd_conv_w[l], ssd_conv_b[l], ssd_dt_bias[l], ssd_a_log[l],
                        ssd_d[l], ssd_norm_w[l][None], ctx_out),
              _mlstm_call(ul[3], uc[3], ul[4], uc[4], mlstm_gate_b[l], mlstm_norm_w[l][None], ctx_out)]
        wo = w_out[l].astype(BF16)
        rw3 = jnp.pad(jnp.concatenate(_split3(router_w[l]), axis=1), ((0, 0), (0, SMALL_W - 3 * N_EXPERTS)))
        n2w = norm2_w[l][None]
        h1_lat, hn_lat, lt_lat = _out_call([y[0] for y in ys], h_lat, mod_l, wo, n2w, rw3, False)
        xg_lat, affc_lat, post_lat = _route_call(lt_lat, hn_lat, cap_lat)
        sets = [(xg_lat, affc_lat)]
        if ctx_out:
            h1_ctx, hn_ctx, lt_ctx = _out_call([y[1] for y in ys], h_ctx, mod_l, wo, n2w, rw3, True)
            xg_ctx, affc_ctx, post_ctx = _route_call(lt_ctx, hn_ctx, cap_ctx)
            sets.append((xg_ctx, affc_ctx))
        y_exp = _ffn_call(sets, expert_w_gate, expert_w_up, expert_w_down, l)
        h_lat = _scatter_call(post_lat, y_exp[0], h1_lat, mod_l, False,
                              None if ctx_out else final_norm_w[None])
        if ctx_out:
            h_ctx = _scatter_call(post_ctx, y_exp[1], h1_ctx, mod_l, True, None)
    return h_lat
```

```python
import functools
import math

import jax
import jax.numpy as jnp
from jax import lax
from jax.experimental import pallas as pl
from jax.experimental.pallas import tpu as pltpu

F32 = jnp.float32
BF16 = jnp.bfloat16
I32 = jnp.int32

D_MODEL = 1024
GRID_W = 64
EPS = 1e-6
ROPE_BASE = 10000.0
GROUP_W = 256
HEAD_DIM = 64
N_HEADS = 4
CHUNK = 64
GLA_DK = 32
GLA_QK = 128
GLA_RANK = 16
GLA_TAU = 16.0
GLA_SUB = 16
SSD_CONV = 5
SSD_BC = 128
N_EXPERTS = 16
EC_CAPACITY = 2
EXPERT_FF = 1536
SMALL_W = 128
LR_OFF, DT_OFF, GATE_OFF = 0, 16, 24
ROW_PAD = 16
VMEM_LIMIT = 56 * 1024 * 1024


def _cparams(*sem):
    return pltpu.CompilerParams(dimension_semantics=sem, vmem_limit_bytes=VMEM_LIMIT)


def _bdot(a, b):
    return jnp.dot(a.astype(BF16), b.astype(BF16), preferred_element_type=F32)


def _bdot_nt(a, b):
    return lax.dot_general(a.astype(BF16), b.astype(BF16), (((1,), (1,)), ((), ())),
                           preferred_element_type=F32)


def _bdot_tn(a, b):
    return lax.dot_general(a.astype(BF16), b.astype(BF16), (((0,), (0,)), ((), ())),
                           preferred_element_type=F32)


def _silu(x):
    return x * jax.nn.sigmoid(x)


def _softplus(x):
    return jnp.maximum(x, 0.0) + jnp.log1p(jnp.exp(-jnp.abs(x)))


def _log_sigmoid(x):
    return -_softplus(-x)


def _rms(x, w):
    return x * lax.rsqrt(jnp.mean(x * x, axis=-1, keepdims=True) + EPS) * w


def _mod_body(cs_ref, w_ref, b_ref, o_ref):
    o_ref[...] = _bdot(_silu(cs_ref[...]), w_ref[...]) + b_ref[...]


def _mod_call(cs, w_mod, b_mod):
    depth, d, n = w_mod.shape
    tn = 1536
    return pl.pallas_call(
        _mod_body,
        grid=(depth, n // tn),
        in_specs=[pl.BlockSpec((ROW_PAD, d), lambda l, j: (0, 0)),
                  pl.BlockSpec((None, d, tn), lambda l, j: (l, 0, j)),
                  pl.BlockSpec((None, 1, tn), lambda l, j: (l, 0, j))],
        out_specs=pl.BlockSpec((None, ROW_PAD, tn), lambda l, j: (l, 0, j)),
        out_shape=jax.ShapeDtypeStruct((depth, ROW_PAD, n), F32),
        compiler_params=_cparams("parallel", "parallel"),
        name="adaln",
    )(cs, w_mod, b_mod.reshape(depth, 1, n))


def _mod_spec(is_ctx, batch):
    if is_ctx:
        return pl.BlockSpec((None, 6, D_MODEL), lambda b, i: (batch, 0, 0))
    return pl.BlockSpec((None, 6, D_MODEL), lambda b, i: (b, 0, 0))


IN_WIDTHS = (1024, 768, 768, 1024, SMALL_W)


def _in_body(h_ref, mod_ref, nw_ref, w_ref, ret_ref, gla_ref, ssd_ref, mls_ref, sm_ref):
    mod = mod_ref[...]
    xn = _rms(h_ref[...], nw_ref[...]) * (1.0 + mod[1:2, :]) + mod[0:1, :]
    u = jnp.dot(xn.astype(BF16), w_ref[...], preferred_element_type=F32)
    off = 0
    for ref, w in zip((ret_ref, gla_ref, ssd_ref, mls_ref, sm_ref), IN_WIDTHS):
        ref[...] = u[:, off:off + w]
        off += w


def _in_call(h, mod_l, nw, w_big, is_ctx):
    batch, t, d = h.shape
    tm = min(512, t)
    n = w_big.shape[1]
    return pl.pallas_call(
        _in_body,
        grid=(batch, t // tm),
        in_specs=[pl.BlockSpec((None, tm, d), lambda b, i: (b, i, 0)),
                  _mod_spec(is_ctx, batch),
                  pl.BlockSpec((1, d), lambda b, i: (0, 0)),
                  pl.BlockSpec((d, n), lambda b, i: (0, 0))],
        out_specs=[pl.BlockSpec((None, tm, w), lambda b, i: (b, i, 0)) for w in IN_WIDTHS],
        out_shape=[jax.ShapeDtypeStruct((batch, t, w), F32) for w in IN_WIDTHS],
        compiler_params=_cparams("parallel", "parallel"),
        name="in_proj",
    )(h, mod_l, nw, w_big)


def _arrange_w_in(w):
    ret_end = 4 * GROUP_W
    gla_main = ret_end + 2 * GLA_QK + 2 * GROUP_W
    gla_end = gla_main + GLA_RANK
    ssd_main = gla_end + GROUP_W + GROUP_W + 2 * SSD_BC
    ssd_end = ssd_main + 2 * N_HEADS
    mls_main = ssd_end + 4 * GROUP_W
    mls_end = mls_main + 4 * N_HEADS
    narrow = (gla_end - gla_main) + (ssd_end - ssd_main) + (mls_end - mls_main)
    pad = jnp.zeros((w.shape[0], SMALL_W - narrow), w.dtype)
    return jnp.concatenate([w[:, :ret_end], w[:, ret_end:gla_main], w[:, gla_end:ssd_main],
                            w[:, ssd_end:mls_main], w[:, gla_main:gla_end], w[:, ssd_main:ssd_end],
                            w[:, mls_main:mls_end], pad], axis=1).astype(BF16)


def _split3(x):
    hi = x.astype(BF16)
    r1 = x - hi.astype(F32)
    mid = r1.astype(BF16)
    return hi, mid, (r1 - mid.astype(F32)).astype(BF16)


def _iota2(shape, axis):
    return lax.broadcasted_iota(I32, shape, axis)


def _block_consts(rev):
    i = _iota2((CHUNK, GROUP_W), 0)
    j = _iota2((CHUNK, GROUP_W), 1) % HEAD_DIM
    return ((j >= i) if rev else (j <= i)), (j == i)


def _same_head():
    return (_iota2((GROUP_W, GROUP_W), 0) // HEAD_DIM) == (_iota2((GROUP_W, GROUP_W), 1) // HEAD_DIM)


def _cum_mat(rev):
    i = _iota2((CHUNK, 3 * CHUNK), 0)
    t = _iota2((CHUNK, 3 * CHUNK), 1) % CHUNK
    return ((t >= i) if rev else (t <= i)).astype(F32).astype(BF16)


def _expand_mat(first_lane, width=GROUP_W):
    return (_iota2((SMALL_W, width), 0) == first_lane + _iota2((SMALL_W, width), 1) // HEAD_DIM
            ).astype(F32).astype(BF16)


def _spread(x, expand):
    n = x.shape[0]
    y = jnp.dot(jnp.concatenate(_split3(x), axis=0), expand, preferred_element_type=F32)
    return (y[0:n] + y[n:2 * n]) + y[2 * n:3 * n]


def _cumsum_b(x_b, cum3):
    return jnp.dot(cum3, jnp.concatenate(_split3(x_b), axis=0), preferred_element_type=F32)


def _tile_heads(x):
    return jnp.concatenate([x] * N_HEADS, axis=0)


def _blockdiag(x, same_head):
    xb = _tile_heads(x.astype(BF16))
    return jnp.where(same_head, xb, jnp.zeros_like(xb))


def _lane_scores(lanes, same_head):
    for ln in lanes:
        if ln["want_out"]:
            ln["scores"] = _bdot_nt(ln["q"], _blockdiag(ln["k"], same_head))


def _decay_lanes(lanes, consts, s_ref):
    same_head = consts[0][2]
    for ln in lanes:
        mask_b, eye_b, _ = consts[ln["d"]]
        g_b = ln["g_b"]
        ln["g_tot"] = g_b[0:1] if ln["d"] == 1 else g_b[CHUNK - 1:CHUNK]
        if ln["want_out"]:
            g_row = jnp.sum(jnp.where(eye_b, g_b, 0.0), axis=0, keepdims=True)
            ln["p"] = ln["scores"] * jnp.exp(jnp.where(mask_b, g_b - g_row, -jnp.inf))
            ln["qg"] = ln["q"] * jnp.exp(g_b)
        ln["khat"] = ln["k"] * jnp.exp(ln["g_tot"] - g_b)
    for ln in lanes:
        ln["kv"] = jnp.where(same_head, _bdot_tn(ln["khat"], ln["v"]), 0.0)
    for ln in lanes:
        if ln["want_out"]:
            ln["pv"] = _bdot(ln["p"], _blockdiag(ln["v"], same_head))
    outs = []
    for ln in lanes:
        s_in = s_ref[ln["d"]]
        outs.append(ln["pv"] + _bdot(ln["qg"], s_in) if ln["want_out"] else None)
        s_ref[ln["d"]] = jnp.exp(ln["g_tot"]) * s_in + ln["kv"]
    return outs


def _seg_sum(x, same_head):
    seg = same_head.astype(F32).astype(BF16)
    hi = x.astype(BF16)
    lo = (x - hi.astype(F32)).astype(BF16)
    return jnp.dot(hi, seg, preferred_element_type=F32) + jnp.dot(lo, seg, preferred_element_type=F32)


def _head_rms(x, same_head, center):
    if center:
        x = x - _seg_sum(x, same_head) * (1.0 / HEAD_DIM)
    return x * lax.rsqrt(_seg_sum(x * x, same_head) * (1.0 / HEAD_DIM) + EPS)


def _scan_lanes(n_ctx, n_lat, multi_step):
    def run(is_lat, n):
        def it(s, carry):
            multi_step(is_lat, [(2 * s, 0), (n - 1 - 2 * s, 1), (2 * s + 1, 0), (n - 2 - 2 * s, 1)])
            return carry

        lax.fori_loop(0, n // 2, it, 0)

    run(False, n_ctx)
    run(True, n_lat)


def _post_tiles(t):
    tile = min(256, t)
    return [(r, tile) for r in range(0, t, tile)]


def _mixer_call(body, name, batch, t_lat, t_ctx, ctx_out, per_sample, shared, scratch):
    in_specs = [pl.BlockSpec((None,) + a.shape[1:], lambda b, nd=a.ndim: (b,) + (0,) * (nd - 1))
                for a in per_sample]
    in_specs += [pl.BlockSpec(a.shape, lambda b, nd=a.ndim: (0,) * nd) for a in shared]
    args = list(per_sample) + list(shared)
    out_shape = [jax.ShapeDtypeStruct((batch, t_lat, GROUP_W), F32)]
    out_specs = [pl.BlockSpec((None, t_lat, GROUP_W), lambda b: (b, 0, 0))]
    if ctx_out:
        out_shape.append(jax.ShapeDtypeStruct((batch, t_ctx, GROUP_W), F32))
        out_specs.append(pl.BlockSpec((None, t_ctx, GROUP_W), lambda b: (b, 0, 0)))
    return pl.pallas_call(
        body, grid=(batch,), in_specs=in_specs, out_specs=out_specs, out_shape=out_shape,
        scratch_shapes=scratch, compiler_params=_cparams("parallel"), name=name,
    )(*args)


def _rope(x, cos, sin_signed, first_half):
    return x * cos + jnp.where(first_half, pltpu.roll(x, x.shape[1] - 16, 1), pltpu.roll(x, 16, 1)) * sin_signed


def _scan_consts():
    same_head = _same_head()
    return tuple(_block_consts(rev) + (same_head,) for rev in (False, True))


def _ret_body(ul_ref, uc_ref, lg_ref, cos_ref, sin_ref, nw_ref, *rest, ctx_out):
    if ctx_out:
        yl_ref, yc_ref, s_ref, accl_ref, accc_ref = rest
    else:
        yl_ref, s_ref, accl_ref, accc_ref = rest
        yc_ref = None
    t_lat, t_ctx = ul_ref.shape[0], uc_ref.shape[0]
    s_ref[...] = jnp.zeros_like(s_ref)
    first_half = (_iota2((1, GROUP_W), 1) % 32) < 16
    consts = _scan_consts()
    row = _iota2((CHUNK, GROUP_W), 0).astype(F32)
    n_decays = (row + 1.0, CHUNK - row)

    def multi_step(is_lat, chunks):
        u_ref, acc_ref = (ul_ref, accl_ref) if is_lat else (uc_ref, accc_ref)
        lanes = []
        for c, d in chunks:
            r = pl.multiple_of(c * CHUNK, CHUNK)
            q = u_ref[pl.ds(r, CHUNK), 0:GROUP_W] * (HEAD_DIM ** -0.5)
            k = u_ref[pl.ds(r, CHUNK), GROUP_W:2 * GROUP_W]
            if is_lat:
                cos, sin = cos_ref[pl.ds(r, CHUNK), :], sin_ref[pl.ds(r, CHUNK), :]
                q, k = _rope(q, cos, sin, first_half), _rope(k, cos, sin, first_half)
            lanes.append(dict(d=d, r=r, want_out=is_lat or ctx_out, q=q, k=k,
                              v=u_ref[pl.ds(r, CHUNK), 2 * GROUP_W:3 * GROUP_W], g_b=n_decays[d] * lg_ref[d]))
        _lane_scores(lanes, consts[0][2])
        for ln, o in zip(lanes, _decay_lanes(lanes, consts, s_ref)):
            if ln["want_out"]:
                acc_ref[ln["d"], pl.ds(ln["r"], CHUNK), :] = o

    _scan_lanes(t_ctx // CHUNK, t_lat // CHUNK, multi_step)

    def post(u_ref, acc_ref, y_ref, t):
        for r, n in _post_tiles(t):
            o = _head_rms(acc_ref[0, r:r + n, :] + acc_ref[1, r:r + n, :], consts[0][2], True)
            y_ref[r:r + n, :] = o * nw_ref[...] * _silu(u_ref[r:r + n, 3 * GROUP_W:4 * GROUP_W])

    post(ul_ref, accl_ref, yl_ref, t_lat)
    if ctx_out:
        post(uc_ref, accc_ref, yc_ref, t_ctx)


def _state_scratch(n, t_lat, t_ctx):
    return ([pltpu.VMEM((2, GROUP_W, GROUP_W), F32)] * n
            + [pltpu.VMEM((2, t_lat, GROUP_W), F32), pltpu.VMEM((2, t_ctx, GROUP_W), F32)])


def _ret_call(u_lat, u_ctx, log_gamma, cos, sin, nw, ctx_out):
    batch, t_lat, _ = u_lat.shape
    t_ctx = u_ctx.shape[1]
    lg = jnp.repeat(log_gamma, HEAD_DIM, axis=1)[:, None, :]
    return _mixer_call(functools.partial(_ret_body, ctx_out=ctx_out), "retention", batch, t_lat, t_ctx, ctx_out,
                       [u_lat, u_ctx], [lg, cos, sin, nw], _state_scratch(1, t_lat, t_ctx))


def _rope_tables(t_lat):
    nf = 16
    inv = ROPE_BASE ** (-jnp.arange(nf, dtype=F32) / nf)
    rows = jnp.repeat(jnp.arange(t_lat // GRID_W), GRID_W).astype(F32)
    cols = jnp.tile(jnp.arange(GRID_W), t_lat // GRID_W).astype(F32)
    ang_r = rows[:, None] * inv[None, :]
    ang_c = cols[:, None] * inv[None, :]
    cos_h = jnp.concatenate([jnp.cos(ang_r)] * 2 + [jnp.cos(ang_c)] * 2, axis=1)
    sin_h = jnp.concatenate([-jnp.sin(ang_r), jnp.sin(ang_r), -jnp.sin(ang_c), jnp.sin(ang_c)], axis=1)
    return jnp.tile(cos_h, (1, N_HEADS)), jnp.tile(sin_h, (1, N_HEADS))


def _ssd_body(ul_ref, uc_ref, sml_ref, smc_ref, cw_ref, cb_ref, dtb_ref, an_ref, dsk_ref, nw_ref, *rest,
              ctx_out):
    if ctx_out:
        yl_ref, yc_ref, s_ref, accl_ref, accc_ref, xl_ref, xc_ref, pad_ref = rest
    else:
        yl_ref, s_ref, accl_ref, accc_ref, xl_ref, xc_ref, pad_ref = rest
        yc_ref = None
    t_lat, t_ctx = ul_ref.shape[0], uc_ref.shape[0]
    conv_ch = GROUP_W + 2 * SSD_BC
    half = (SSD_CONV - 1) // 2
    halo = 8

    def conv(u_ref, x_ref, t):
        pad_ref[0:halo, :] = jnp.zeros((halo, conv_ch), F32)
        pad_ref[halo + t:2 * halo + t, :] = jnp.zeros((halo, conv_ch), F32)
        for r, n in _post_tiles(t):
            pad_ref[halo + r:halo + r + n, :] = u_ref[r:r + n, GROUP_W:GROUP_W + conv_ch]
        for r, n in _post_tiles(t):
            acc = jnp.zeros((n, conv_ch), F32) + cb_ref[...]
            for tap in range(SSD_CONV):
                lo = halo + r + tap - half
                acc = acc + pad_ref[lo:lo + n, :] * cw_ref[tap:tap + 1, :]
            act = _silu(acc)
            grp = [act[:, GROUP_W + m * HEAD_DIM:GROUP_W + (m + 1) * HEAD_DIM] for m in range(4)]
            x_ref[r:r + n, 0:GROUP_W] = act[:, 0:GROUP_W]
            x_ref[r:r + n, GROUP_W:2 * GROUP_W] = jnp.concatenate([grp[0], grp[0], grp[1], grp[1]], axis=1)
            x_ref[r:r + n, 2 * GROUP_W:3 * GROUP_W] = jnp.concatenate([grp[2], grp[2], grp[3], grp[3]], axis=1)

    conv(uc_ref, xc_ref, t_ctx)
    conv(ul_ref, xl_ref, t_lat)
    s_ref[...] = jnp.zeros_like(s_ref)
    consts = _scan_consts()
    cum3 = (_cum_mat(False), _cum_mat(True))
    expand = (_expand_mat(DT_OFF), _expand_mat(DT_OFF + N_HEADS))

    def multi_step(is_lat, chunks):
        x_ref, sm_ref, acc_ref = (xl_ref, sml_ref, accl_ref) if is_lat else (xc_ref, smc_ref, accc_ref)
        lanes = []
        for c, d in chunks:
            r = pl.multiple_of(c * CHUNK, CHUNK)
            dt = _spread(_softplus(sm_ref[pl.ds(r, CHUNK), :] + dtb_ref[...]), expand[d])
            lanes.append(dict(d=d, r=r, want_out=is_lat or ctx_out, dt=dt,
                              k=x_ref[pl.ds(r, CHUNK), GROUP_W:2 * GROUP_W],
                              q=x_ref[pl.ds(r, CHUNK), 2 * GROUP_W:3 * GROUP_W]))
        _lane_scores(lanes, consts[0][2])
        for ln in lanes:
            ln["g_b"] = _cumsum_b(ln["dt"] * an_ref[ln["d"]], cum3[ln["d"]])
            ln["v"] = x_ref[pl.ds(ln["r"], CHUNK), 0:GROUP_W] * ln["dt"]
        for ln, o in zip(lanes, _decay_lanes(lanes, consts, s_ref)):
            if ln["want_out"]:
                acc_ref[ln["d"], pl.ds(ln["r"], CHUNK), :] = o

    _scan_lanes(t_ctx // CHUNK, t_lat // CHUNK, multi_step)

    def post(u_ref, x_ref, acc_ref, y_ref, t):
        for r, n in _post_tiles(t):
            y = acc_ref[0, r:r + n, :] + acc_ref[1, r:r + n, :] + dsk_ref[...] * x_ref[r:r + n, 0:GROUP_W]
            y_ref[r:r + n, :] = _rms(y * _silu(u_ref[r:r + n, 0:GROUP_W]), nw_ref[...])

    post(ul_ref, xl_ref, accl_ref, yl_ref, t_lat)
    if ctx_out:
        post(uc_ref, xc_ref, accc_ref, yc_ref, t_ctx)


def _ssd_call(u_lat, u_ctx, sm_lat, sm_ctx, conv_w, conv_b, dt_bias, a_log, d_skip, nw, ctx_out):
    batch, t_lat, _ = u_lat.shape
    t_ctx = u_ctx.shape[1]
    conv_ch = GROUP_W + 2 * SSD_BC
    dtb = jnp.zeros((SMALL_W,), F32).at[DT_OFF:DT_OFF + 2 * N_HEADS].set(dt_bias.reshape(-1))[None]
    a_neg = jnp.repeat(-jnp.exp(a_log.astype(F32)), HEAD_DIM, axis=1)[:, None, :]
    cw = jnp.concatenate([conv_w, jnp.zeros((8 - SSD_CONV, conv_ch), F32)], axis=0)
    dsk = jnp.repeat(d_skip, HEAD_DIM)[None]
    scratch = _state_scratch(1, t_lat, t_ctx) + [
        pltpu.VMEM((t_lat, 3 * GROUP_W), F32), pltpu.VMEM((t_ctx, 3 * GROUP_W), F32),
        pltpu.VMEM((t_lat + 16, conv_ch), F32)]
    return _mixer_call(functools.partial(_ssd_body, ctx_out=ctx_out), "ssd", batch, t_lat, t_ctx, ctx_out,
                       [u_lat, u_ctx, sm_lat, sm_ctx], [cw, conv_b[None], dtb, a_neg, dsk, nw], scratch)


def _mlstm_body(ul_ref, uc_ref, sml_ref, smc_ref, gb_ref, nw_ref, *rest, ctx_out):
    if ctx_out:
        yl_ref, yc_ref, c_ref, n_ref, m_ref, accl_ref, accc_ref = rest
    else:
        yl_ref, c_ref, n_ref, m_ref, accl_ref, accc_ref = rest
        yc_ref = None
    t_lat, t_ctx = ul_ref.shape[0], uc_ref.shape[0]
    c_ref[...] = jnp.zeros_like(c_ref)
    n_ref[...] = jnp.zeros_like(n_ref)
    m_ref[...] = jnp.zeros_like(m_ref)
    consts = _scan_consts()
    same_head = consts[0][2]
    seg = same_head.astype(F32).astype(BF16)
    cum3 = (_cum_mat(False), _cum_mat(True))
    f_lane = tuple(GATE_OFF + 2 * N_HEADS * d + N_HEADS for d in (0, 1))
    expand_fi = tuple(jnp.concatenate([_expand_mat(f_lane[d]), _expand_mat(f_lane[d] - N_HEADS)], axis=1)
                      for d in (0, 1))
    lane = _iota2((1, SMALL_W), 1)
    is_f = (lane >= GATE_OFF) & ((lane - GATE_OFF) % (2 * N_HEADS) >= N_HEADS)
    ones_b = jnp.ones((CHUNK, GROUP_W), BF16)

    def running_max(x, rev):
        row = _iota2(x.shape, 0)
        s = 1
        while s < CHUNK:
            shifted = pltpu.roll(x, CHUNK - s if rev else s, 0)
            valid = (row < CHUNK - s) if rev else (row >= s)
            x = jnp.maximum(x, jnp.where(valid, shifted, -jnp.inf))
            s *= 2
        return x

    def multi_step(is_lat, chunks):
        u_ref, sm_ref, acc_ref = (ul_ref, sml_ref, accl_ref) if is_lat else (uc_ref, smc_ref, accc_ref)
        want_out = is_lat or ctx_out
        lanes = []
        for c, d in chunks:
            r = pl.multiple_of(c * CHUNK, CHUNK)
            gates = sm_ref[pl.ds(r, CHUNK), :] + gb_ref[...]
            both = _spread(jnp.where(is_f, _log_sigmoid(gates), gates), expand_fi[d])
            lanes.append(dict(d=d, r=r, want_out=want_out, lf_b=both[:, 0:GROUP_W], ig_b=both[:, GROUP_W:2 * GROUP_W],
                              q=u_ref[pl.ds(r, CHUNK), 0:GROUP_W],
                              k=u_ref[pl.ds(r, CHUNK), GROUP_W:2 * GROUP_W] * (HEAD_DIM ** -0.5),
                              v=u_ref[pl.ds(r, CHUNK), 2 * GROUP_W:3 * GROUP_W]))
        _lane_scores(lanes, same_head)
        for ln in lanes:
            ln["g_b"] = _cumsum_b(ln["lf_b"], cum3[ln["d"]])
        m_now = [m_ref[0], m_ref[1]]
        for ln in lanes:
            d, g_b, ig_b = ln["d"], ln["g_b"], ln["ig_b"]
            mask_b, eye_b, _ = consts[d]
            g_tot = g_b[0:1] if d == 1 else g_b[CHUNK - 1:CHUNK]
            w_end = g_tot - g_b + ig_b
            b_max = jnp.max(w_end, axis=0, keepdims=True)
            ln["ke"] = ln["k"] * jnp.exp(w_end - b_max)
            m_in = m_now[d]
            m_new = jnp.maximum(g_tot + m_in, b_max)
            ln["old"], ln["new"] = jnp.exp(g_tot + m_in - m_new), jnp.exp(b_max - m_new)
            m_now[d] = m_new
            if want_out:
                m_inter = g_b + m_in
                m_tot = jnp.maximum(m_inter, g_b + running_max(ig_b - g_b, d == 1))
                g_row = jnp.sum(jnp.where(eye_b, g_b, 0.0), axis=0, keepdims=True)
                ig_row = jnp.sum(jnp.where(eye_b, ig_b, 0.0), axis=0, keepdims=True)
                d_log = jnp.where(mask_b, g_b - g_row + ig_row, -jnp.inf)
                ln["qk"] = ln["scores"] * jnp.exp(d_log - m_tot)
                ln["s_inter"] = jnp.exp(m_inter - m_tot)
                ln["floor"] = jnp.exp(-m_tot)
        m_ref[0], m_ref[1] = m_now
        for ln in lanes:
            ln["upd"] = _bdot_tn(ln["ke"], jnp.concatenate([ln["v"].astype(BF16), ones_b], axis=1))
        if want_out:
            for ln in lanes:
                ln["num"] = _bdot(ln["qk"], _blockdiag(ln["v"], same_head))
                ln["den"] = _seg_sum(ln["qk"], same_head)
        for ln in lanes:
            d = ln["d"]
            c_in, n_in = c_ref[d], n_ref[d]
            if want_out:
                num = ln["num"] + ln["s_inter"] * _bdot(ln["q"], c_in)
                den = ln["den"] + ln["s_inter"] * _bdot(ln["q"], n_in)
                acc_ref[d, pl.ds(ln["r"], CHUNK), :] = num / jnp.maximum(jnp.abs(den), ln["floor"])
            c_ref[d] = ln["old"] * c_in + ln["new"] * jnp.where(same_head, ln["upd"][:, 0:GROUP_W], 0.0)
            n_ref[d] = ln["old"] * n_in + ln["new"] * jnp.where(same_head, ln["upd"][:, GROUP_W:2 * GROUP_W], 0.0)

    _scan_lanes(t_ctx // CHUNK, t_lat // CHUNK, multi_step)

    def post(u_ref, acc_ref, y_ref, t):
        for r, n in _post_tiles(t):
            o = _head_rms(acc_ref[0, r:r + n, :] + acc_ref[1, r:r + n, :], same_head, False)
            y_ref[r:r + n, :] = jax.nn.sigmoid(u_ref[r:r + n, 3 * GROUP_W:4 * GROUP_W]) * (o * nw_ref[...])

    post(ul_ref, accl_ref, yl_ref, t_lat)
    if ctx_out:
        post(uc_ref, accc_ref, yc_ref, t_ctx)


def _mlstm_call(u_lat, u_ctx, sm_lat, sm_ctx, gate_b, nw, ctx_out):
    batch, t_lat, _ = u_lat.shape
    t_ctx = u_ctx.shape[1]
    gb = jnp.zeros((SMALL_W,), F32).at[GATE_OFF:GATE_OFF + 4 * N_HEADS].set(gate_b.reshape(-1))[None]
    scratch = ([pltpu.VMEM((2, GROUP_W, GROUP_W), F32), pltpu.VMEM((2, GROUP_W, GROUP_W), F32),
                pltpu.VMEM((2, 1, GROUP_W), F32)]
               + [pltpu.VMEM((2, t_lat, GROUP_W), F32), pltpu.VMEM((2, t_ctx, GROUP_W), F32)])
    return _mixer_call(functools.partial(_mlstm_body, ctx_out=ctx_out), "mlstm", batch, t_lat, t_ctx, ctx_out,
                       [u_lat, u_ctx, sm_lat, sm_ctx], [gb, nw], scratch)


def _gla_body(ul_ref, uc_ref, sml_ref, smc_ref, gw_ref, gb_ref, nw_ref, *rest, ctx_out):
    if ctx_out:
        yl_ref, yc_ref, st_ref, accl_ref, accc_ref, gl_ref, gc_ref = rest
    else:
        yl_ref, st_ref, accl_ref, accc_ref, gl_ref, gc_ref = rest
        yc_ref = None
    t_lat, t_ctx = ul_ref.shape[0], uc_ref.shape[0]
    st_ref[...] = jnp.zeros_like(st_ref)
    n_sub = CHUNK // GLA_SUB
    i = lax.broadcasted_iota(I32, (CHUNK, CHUNK), 0)
    j = lax.broadcasted_iota(I32, (CHUNK, CHUNK), 1)
    same = (i // GLA_SUB) == (j // GLA_SUB)
    lcum = ((same & (j <= i)).astype(F32).astype(BF16), (same & (j >= i)).astype(F32).astype(BF16))
    expand = ((lax.broadcasted_iota(I32, (GLA_QK, GROUP_W), 0) // GLA_DK)
              == (lax.broadcasted_iota(I32, (GLA_QK, GROUP_W), 1) // HEAD_DIM)).astype(F32).astype(BF16)
    blockdiag = ((lax.broadcasted_iota(I32, (GROUP_W, GLA_QK), 0) // HEAD_DIM)
                 == (lax.broadcasted_iota(I32, (GROUP_W, GLA_QK), 1) // GLA_DK)).astype(F32)
    sub_row = lax.broadcasted_iota(I32, (CHUNK, 1), 0) % GLA_SUB

    def roll_sub(x, delta):
        return jnp.concatenate([pltpu.roll(x[a * GLA_SUB:(a + 1) * GLA_SUB], delta, 0) for a in range(n_sub)],
                               axis=0)

    def qkv(u_ref, r):
        return (u_ref[pl.ds(r, CHUNK), 0:GLA_QK] * (GLA_DK ** -0.5), u_ref[pl.ds(r, CHUNK), GLA_QK:2 * GLA_QK],
                u_ref[pl.ds(r, CHUNK), 2 * GLA_QK:2 * GLA_QK + GROUP_W])

    def intra(u_ref, sm_ref, g_ref, acc_ref, want_out, c):
        r = pl.multiple_of(c * CHUNK, CHUNK)
        lr = sm_ref[pl.ds(r, CHUNK), LR_OFF:LR_OFF + GLA_RANK]
        g = []
        for d in (0, 1):
            la = _log_sigmoid(_bdot(lr, gw_ref[d]) + gb_ref[d]) / GLA_TAU
            g.append(sum(jnp.dot(lcum[d], piece, preferred_element_type=F32) for piece in _split3(la)))
            g_ref[d, pl.ds(r, CHUNK), :] = g[d]
        if not want_out:
            return
        q, k, v = qkv(u_ref, r)
        prods, vals = [2.0 * q * k], [v]
        for delta in range(1, GLA_SUB):
            fwd_pair = sub_row >= delta
            e = jnp.exp(jnp.where(fwd_pair, g[0] - roll_sub(g[0], delta), g[1] - roll_sub(g[1], delta)))
            prods.append(q * roll_sub(k, delta) * e)
            vals.append(roll_sub(v, delta))
        w = jnp.dot(jnp.concatenate(prods, axis=0).astype(BF16), expand, preferred_element_type=F32)
        acc_ref[pl.ds(r, CHUNK), :] = sum(w[n * CHUNK:(n + 1) * CHUNK] * vals[n] for n in range(GLA_SUB))

    def intra_ctx(c, carry):
        intra(uc_ref, smc_ref, gc_ref, accc_ref, ctx_out, c)
        return carry

    def intra_lat(c, carry):
        intra(ul_ref, sml_ref, gl_ref, accl_ref, True, c)
        return carry

    lax.fori_loop(0, t_ctx // CHUNK, intra_ctx, 0)
    lax.fori_loop(0, t_lat // CHUNK, intra_lat, 0, unroll=2)

    def multi_step(is_lat, chunks):
        u_ref, g_ref, acc_ref = (ul_ref, gl_ref, accl_ref) if is_lat else (uc_ref, gc_ref, accc_ref)
        want_out = is_lat or ctx_out
        lanes = []
        for c, d in chunks:
            r = pl.multiple_of(c * CHUNK, CHUNK)
            q, k, v = qkv(u_ref, r)
            g = g_ref[d, pl.ds(r, CHUNK), :]
            subs = []
            for a in (range(n_sub) if d == 0 else range(n_sub - 1, -1, -1)):
                rows = slice(a * GLA_SUB, (a + 1) * GLA_SUB)
                g_a = g[rows]
                g_end = g_a[GLA_SUB - 1:GLA_SUB] if d == 0 else g_a[0:1]
                subs.append(dict(a=a, decay=jnp.exp(g_end),
                                 kv=_bdot_tn(v[rows], k[rows] * jnp.exp(g_end - g_a)) * blockdiag,
                                 qg=q[rows] * jnp.exp(g_a) if want_out else None))
            lanes.append(dict(d=d, r=r, subs=subs, out=[None] * n_sub))
        for step_lanes in (lanes[0:2], lanes[2:4]):
            for n in range(n_sub):
                for ln in step_lanes:
                    sub, st = ln["subs"][n], st_ref[ln["d"]]
                    if want_out:
                        ln["out"][sub["a"]] = _bdot_nt(sub["qg"], st)
                    st_ref[ln["d"]] = st * sub["decay"] + sub["kv"]
        if want_out:
            for ln in lanes:
                acc_ref[pl.ds(ln["r"], CHUNK), :] += jnp.concatenate(ln["out"], axis=0)

    _scan_lanes(t_ctx // CHUNK, t_lat // CHUNK, multi_step)

    def post(u_ref, acc_ref, y_ref, t):
        for r, n in _post_tiles(t):
            o = _head_rms(acc_ref[r:r + n, :], _same_head(), False)
            y_ref[r:r + n, :] = o * nw_ref[...] * _silu(u_ref[r:r + n, 2 * GLA_QK + GROUP_W:2 * GLA_QK + 2 * GROUP_W])

    post(ul_ref, accl_ref, yl_ref, t_lat)
    if ctx_out:
        post(uc_ref, accc_ref, yc_ref, t_ctx)


def _gla_call(u_lat, u_ctx, sm_lat, sm_ctx, gate_w, gate_b, nw, ctx_out):
    batch, t_lat, _ = u_lat.shape
    t_ctx = u_ctx.shape[1]
    scratch = [pltpu.VMEM((2, GROUP_W, GLA_QK), F32),
               pltpu.VMEM((t_lat, GROUP_W), F32), pltpu.VMEM((t_ctx, GROUP_W), F32),
               pltpu.VMEM((2, t_lat, GLA_QK), F32), pltpu.VMEM((2, t_ctx, GLA_QK), F32)]
    return _mixer_call(functools.partial(_gla_body, ctx_out=ctx_out), "gla", batch, t_lat, t_ctx, ctx_out,
                       [u_lat, u_ctx, sm_lat, sm_ctx], [gate_w, gate_b[:, None, :], nw], scratch)


def _out_body(yr_ref, yg_ref, ys_ref, ym_ref, h_ref, mod_ref, wo_ref, n2_ref, rw_ref, h1_ref, hn_ref, lt_ref):
    mod = mod_ref[...]
    proj = sum(_bdot(y_ref[...], wo_ref[n * GROUP_W:(n + 1) * GROUP_W, :])
               for n, y_ref in enumerate((yr_ref, yg_ref, ys_ref, ym_ref)))
    h1 = h_ref[...] + mod[2:3, :] * proj
    h1_ref[...] = h1
    xn = _rms(h1, n2_ref[...]) * (1.0 + mod[4:5, :]) + mod[3:4, :]
    hn_ref[...] = xn.astype(BF16)
    x_hi, x_mid, _ = _split3(xn)
    parts = (jnp.dot(x_hi, rw_ref[...], preferred_element_type=F32)
             + jnp.dot(x_mid, rw_ref[...], preferred_element_type=F32)).T
    lt_ref[...] = parts[0:N_EXPERTS] + parts[N_EXPERTS:2 * N_EXPERTS] + parts[2 * N_EXPERTS:3 * N_EXPERTS]


def _out_call(ys, h, mod_l, wo, n2w, rw3, is_ctx):
    batch, t, d = h.shape
    tm = min(512, t)
    y_spec = pl.BlockSpec((None, tm, GROUP_W), lambda b, i: (b, i, 0))
    return pl.pallas_call(
        _out_body,
        grid=(batch, t // tm),
        in_specs=[y_spec, y_spec, y_spec, y_spec,
                  pl.BlockSpec((None, tm, d), lambda b, i: (b, i, 0)),
                  _mod_spec(is_ctx, batch),
                  pl.BlockSpec((d, d), lambda b, i: (0, 0)),
                  pl.BlockSpec((1, d), lambda b, i: (0, 0)),
                  pl.BlockSpec(rw3.shape, lambda b, i: (0, 0))],
        out_specs=[pl.BlockSpec((None, tm, d), lambda b, i: (b, i, 0)),
                   pl.BlockSpec((None, tm, d), lambda b, i: (b, i, 0)),
                   pl.BlockSpec((None, N_EXPERTS, tm), lambda b, i: (b, 0, i))],
        out_shape=[jax.ShapeDtypeStruct((batch, t, d), F32),
                   jax.ShapeDtypeStruct((batch, t, d), BF16),
                   jax.ShapeDtypeStruct((batch, N_EXPERTS, t), F32)],
        compiler_params=_cparams("parallel", "parallel"),
        name="out_proj",
    )(*ys, h, mod_l, wo, n2w, rw3)


def _excl_prefix(flags, t):
    blk = min(256, t)
    below = (lax.broadcasted_iota(I32, (blk, blk), 0) < lax.broadcasted_iota(I32, (blk, blk), 1))
    below = below.astype(F32).astype(BF16)
    outs = []
    carry = jnp.zeros((flags.shape[0], 1), F32)
    for n in range(t // blk):
        f = flags[:, n * blk:(n + 1) * blk]
        outs.append(jnp.dot(f.astype(BF16), below, preferred_element_type=F32) + carry)
        carry = carry + jnp.sum(f, axis=1, keepdims=True)
    return jnp.concatenate(outs, axis=1)


def _route_body(lt_ref, hn_ref, xg_ref, affc_ref, post_ref, pos_s, aff_s, *, cap):
    t = hn_ref.shape[0]
    logits = lt_ref[...]
    ex = jnp.exp(logits - jnp.max(logits, axis=0, keepdims=True))
    aff = ex / jnp.sum(ex, axis=0, keepdims=True)
    def fix_bit(n, thr):
        cand = thr | (jnp.int32(1) << (30 - n))
        cnt = jnp.sum((aff >= pltpu.bitcast(cand, F32)).astype(F32), axis=1, keepdims=True)
        return jnp.where(cnt >= float(cap), cand, thr)

    thr = lax.fori_loop(0, 31, fix_bit, jnp.zeros((N_EXPERTS, 1), I32))
    above = (aff >= pltpu.bitcast(thr + 1, F32)).astype(F32)
    tied = (aff >= pltpu.bitcast(thr, F32)).astype(F32) - above
    need = float(cap) - jnp.sum(above, axis=1, keepdims=True)
    sel = above + tied * (_excl_prefix(tied, t) < need).astype(F32)
    pos = jnp.where(sel > 0.0, _excl_prefix(sel, t), -1.0)
    pos_s[...] = pos
    aff_s[...] = aff
    post_ref[...] = jnp.concatenate([pos, jnp.full((SMALL_W - N_EXPERTS, t), -1.0, F32)], axis=0).T
    slots = lax.broadcasted_iota(I32, (cap, t), 0).astype(F32)

    def gather(e, carry):
        hit = slots == pos_s[pl.ds(e, 1), :]
        xg_ref[e] = jnp.dot(hit.astype(BF16), hn_ref[...], preferred_element_type=F32).astype(BF16)
        a = jnp.sum(jnp.where(hit, aff_s[pl.ds(e, 1), :], 0.0), axis=1, keepdims=True)
        affc_ref[e] = jnp.broadcast_to(a, (cap, SMALL_W))
        return carry

    lax.fori_loop(0, N_EXPERTS, gather, 0)


def _route_call(lt, hn, cap):
    batch, t, d = hn.shape
    return pl.pallas_call(
        functools.partial(_route_body, cap=cap),
        grid=(batch,),
        in_specs=[pl.BlockSpec((None, N_EXPERTS, t), lambda b: (b, 0, 0)),
                  pl.BlockSpec((None, t, d), lambda b: (b, 0, 0))],
        out_specs=[pl.BlockSpec((N_EXPERTS, cap, d), lambda b: (0, b, 0)),
                   pl.BlockSpec((N_EXPERTS, cap, SMALL_W), lambda b: (0, b, 0)),
                   pl.BlockSpec((None, t, SMALL_W), lambda b: (b, 0, 0))],
        out_shape=[jax.ShapeDtypeStruct((N_EXPERTS, batch * cap, d), BF16),
                   jax.ShapeDtypeStruct((N_EXPERTS, batch * cap, SMALL_W), F32),
                   jax.ShapeDtypeStruct((batch, t, SMALL_W), F32)],
        scratch_shapes=[pltpu.VMEM((N_EXPERTS, t), F32), pltpu.VMEM((N_EXPERTS, t), F32)],
        compiler_params=_cparams("parallel"),
        name="route_gather",
    )(lt, hn)


FFN_TF = 512
FFN_TM = 512


def _ffn_body(*refs, n_sets):
    ins = refs[:2 * n_sets]
    wg_ref, wu_ref, wd_ref = refs[2 * n_sets:2 * n_sets + 3]
    outs = refs[2 * n_sets + 3:3 * n_sets + 3]
    accs = refs[3 * n_sets + 3:4 * n_sets + 3]
    wg_s, wu_s, wd_s = refs[4 * n_sets + 3:]
    f = pl.program_id(1)
    wg_s[...] = wg_ref[...].astype(BF16)
    wu_s[...] = wu_ref[...].astype(BF16)
    wd_s[...] = wd_ref[...].astype(BF16)
    dot = functools.partial(jnp.dot, preferred_element_type=F32)
    for s in range(n_sets):
        x_ref, a_ref, y_ref, acc_ref = ins[2 * s], ins[2 * s + 1], outs[s], accs[s]
        rows = x_ref.shape[0]

        @pl.when(f == 0)
        def _():
            acc_ref[...] = jnp.zeros_like(acc_ref)

        for r in range(0, rows, FFN_TM):
            n = min(FFN_TM, rows - r)
            x = x_ref[r:r + n, :]
            hid = _silu(dot(x, wg_s[...])) * dot(x, wu_s[...])
            acc_ref[r:r + n, :] += dot(hid.astype(BF16), wd_s[...])

        @pl.when(f == pl.num_programs(1) - 1)
        def _():
            for r in range(0, rows, FFN_TM):
                n = min(FFN_TM, rows - r)
                y_ref[r:r + n, :] = (acc_ref[r:r + n, :] * a_ref[r:r + n, 0:1]).astype(BF16)


def _ffn_call(sets, wg, wu, wd, layer):
    _, n_exp, d, ff = wg.shape
    in_specs, args, out_specs, out_shape, scratch = [], [], [], [], []
    for xg, affc in sets:
        m = xg.shape[1]
        in_specs += [pl.BlockSpec((None, m, d), lambda e, f: (e, 0, 0)),
                     pl.BlockSpec((None, m, SMALL_W), lambda e, f: (e, 0, 0))]
        args += [xg, affc]
        out_specs.append(pl.BlockSpec((None, m, d), lambda e, f: (e, 0, 0)))
        out_shape.append(jax.ShapeDtypeStruct((n_exp, m, d), BF16))
        scratch.append(pltpu.VMEM((m, d), F32))
    in_specs += [pl.BlockSpec((None, None, d, FFN_TF), lambda e, f: (layer, e, 0, f)),
                 pl.BlockSpec((None, None, d, FFN_TF), lambda e, f: (layer, e, 0, f)),
                 pl.BlockSpec((None, None, FFN_TF, d), lambda e, f: (layer, e, f, 0))]
    scratch += [pltpu.VMEM((d, FFN_TF), BF16), pltpu.VMEM((d, FFN_TF), BF16), pltpu.VMEM((FFN_TF, d), BF16)]
    return pl.pallas_call(
        functools.partial(_ffn_body, n_sets=len(sets)),
        grid=(n_exp, ff // FFN_TF),
        in_specs=in_specs, out_specs=out_specs, out_shape=out_shape, scratch_shapes=scratch,
        compiler_params=_cparams("parallel", "arbitrary"),
        name="expert_ffn",
    )(*args, wg, wu, wd)


def _scatter_body(post_ref, y_ref, h1_ref, mod_ref, fw_ref, o_ref, *, final):
    n_exp, cap, d = y_ref.shape
    pos = post_ref[...]
    slots = lax.broadcasted_iota(I32, (pos.shape[0], cap), 1).astype(F32)
    hit = jnp.concatenate([(pos[:, e:e+1] == slots).astype(BF16) for e in range(n_exp)], axis=1)
    upd = jnp.dot(hit, y_ref[...].reshape(n_exp * cap, d), preferred_element_type=F32)
    h2 = h1_ref[...] + mod_ref[5:6, :] * upd
    o_ref[...] = _rms(h2, fw_ref[...]) if final else h2


def _scatter_call(post, y, h1, mod_l, is_ctx, final_w):
    batch, t, d = h1.shape
    n_exp = y.shape[0]
    cap = y.shape[1] // batch
    tm = min(256, t)
    final = final_w is not None
    fw = final_w if final else jnp.ones((1, d), F32)
    return pl.pallas_call(
        functools.partial(_scatter_body, final=final),
        grid=(batch, t // tm),
        in_specs=[pl.BlockSpec((None, tm, SMALL_W), lambda b, i: (b, i, 0)),
                  pl.BlockSpec((n_exp, cap, d), lambda b, i: (0, b, 0)),
                  pl.BlockSpec((None, tm, d), lambda b, i: (b, i, 0)),
                  _mod_spec(is_ctx, batch),
                  pl.BlockSpec((1, d), lambda b, i: (0, 0))],
        out_specs=pl.BlockSpec((None, tm, d), lambda b, i: (b, i, 0)),
        out_shape=jax.ShapeDtypeStruct((batch, t, d), F32),
        compiler_params=_cparams("parallel", "parallel"),
        name="scatter_residual",
    )(post, y, h1, mod_l, fw)


def kernel(x, c, ctx, c_ctx, w_mod, b_mod, norm1_w, norm2_w, w_in, w_out, ret_decay_logit, ret_norm_w,
           gla_gate_w, gla_gate_b, gla_norm_w, ssd_conv_w, ssd_conv_b, ssd_dt_bias, ssd_a_log, ssd_d,
           ssd_norm_w, mlstm_gate_b, mlstm_norm_w, router_w, expert_w_gate, expert_w_up, expert_w_down,
           final_norm_w):
    batch, t_lat, d = x.shape
    t_ctx = ctx.shape[1]
    depth = w_mod.shape[0]
    assert batch + 1 <= ROW_PAD and d == D_MODEL
    assert t_lat % 256 == 0 and t_ctx % 128 == 0
    cs = jnp.concatenate([c, c_ctx[None], jnp.zeros((ROW_PAD - batch - 1, d), F32)], axis=0)
    mod = _mod_call(cs, w_mod, b_mod).reshape(depth, ROW_PAD, 6, d)
    cos, sin = _rope_tables(t_lat)
    cap_lat = EC_CAPACITY * t_lat // N_EXPERTS
    cap_ctx = EC_CAPACITY * t_ctx // N_EXPERTS
    h_lat, h_ctx = x, ctx
    for l in range(depth):
        ctx_out = l < depth - 1
        mod_l = mod[l]
        w_big = _arrange_w_in(w_in[l])
        ul = _in_call(h_lat, mod_l, norm1_w[l][None], w_big, False)
        uc = _in_call(h_ctx, mod_l, norm1_w[l][None], w_big, True)
        ys = [_ret_call(ul[0], uc[0], jax.nn.log_sigmoid(ret_decay_logit[l].astype(F32)), cos, sin,
                        ret_norm_w[l][None], ctx_out),
              _gla_call(ul[1], uc[1], ul[4], uc[4], gla_gate_w[l], gla_gate_b[l], gla_norm_w[l][None], ctx_out),
              _ssd_call(ul[2], uc[2], ul[4], uc[4], ssd_conv_w[l], ssd_conv_b[l], ssd_dt_bias[l], ssd_a_log[l],
                        ssd_d[l], ssd_norm_w[l][None], ctx_out),
              _mlstm_call(ul[3], uc[3], ul[4], uc[4], mlstm_gate_b[l], mlstm_norm_w[l][None], ctx_out)]
        wo = w_out[l].astype(BF16)
        rw3 = jnp.pad(jnp.concatenate(_split3(router_w[l]), axis=1), ((0, 0), (0, SMALL_W - 3 * N_EXPERTS)))
        n2w = norm2_w[l][None]
        h1_lat, hn_lat, lt_lat = _out_call([y[0] for y in ys], h_lat, mod_l, wo, n2w, rw3, False)
        xg_lat, affc_lat, post_lat = _route_call(lt_lat, hn_lat, cap_lat)
        sets = [(xg_lat, affc_lat)]
        if ctx_out:
            h1_ctx, hn_ctx, lt_ctx = _out_call([y[1] for y in ys], h_ctx, mod_l, wo, n2w, rw3, True)
            xg_ctx, affc_ctx, post_ctx = _route_call(lt_ctx, hn_ctx, cap_ctx)
            sets.append((xg_ctx, affc_ctx))
        y_exp = _ffn_call(sets, expert_w_gate, expert_w_up, expert_w_down, l)
        h_lat = _scatter_call(post_lat, y_exp[0], h1_lat, mod_l, False,
                              None if ctx_out else final_norm_w[None])
        if ctx_out:
            h_ctx = _scatter_call(post_ctx, y_exp[1], h1_ctx, mod_l, True, None)
    return h_lat
```

```python
import functools
import math

import jax
import jax.numpy as jnp
from jax import lax
from jax.experimental import pallas as pl
from jax.experimental.pallas import tpu as pltpu

F32 = jnp.float32
BF16 = jnp.bfloat16
I32 = jnp.int32

D_MODEL = 1024
GRID_W = 64
EPS = 1e-6
ROPE_BASE = 10000.0
GROUP_W = 256
HEAD_DIM = 64
N_HEADS = 4
CHUNK = 64
GLA_DK = 32
GLA_QK = 128
GLA_RANK = 16
GLA_TAU = 16.0
GLA_SUB = 16
SSD_CONV = 5
SSD_BC = 128
N_EXPERTS = 16
EC_CAPACITY = 2
EXPERT_FF = 1536
SMALL_W = 128
LR_OFF, DT_OFF, GATE_OFF = 0, 16, 24
ROW_PAD = 16
VMEM_LIMIT = 56 * 1024 * 1024


def _cparams(*sem):
    return pltpu.CompilerParams(dimension_semantics=sem, vmem_limit_bytes=VMEM_LIMIT)


def _bdot(a, b):
    return jnp.dot(a.astype(BF16), b.astype(BF16), preferred_element_type=F32)


def _bdot_nt(a, b):
    return lax.dot_general(a.astype(BF16), b.astype(BF16), (((1,), (1,)), ((), ())),
                           preferred_element_type=F32)


def _bdot_tn(a, b):
    return lax.dot_general(a.astype(BF16), b.astype(BF16), (((0,), (0,)), ((), ())),
                           preferred_element_type=F32)


def _silu(x):
    return x * jax.nn.sigmoid(x)


def _softplus(x):
    return jnp.maximum(x, 0.0) + jnp.log1p(jnp.exp(-jnp.abs(x)))


def _log_sigmoid(x):
    return -_softplus(-x)


def _rms(x, w):
    return x * lax.rsqrt(jnp.mean(x * x, axis=-1, keepdims=True) + EPS) * w


def _mod_body(cs_ref, w_ref, b_ref, o_ref):
    o_ref[...] = _bdot(_silu(cs_ref[...]), w_ref[...]) + b_ref[...]


def _mod_call(cs, w_mod, b_mod):
    depth, d, n = w_mod.shape
    tn = 1536
    return pl.pallas_call(
        _mod_body,
        grid=(depth, n // tn),
        in_specs=[pl.BlockSpec((ROW_PAD, d), lambda l, j: (0, 0)),
                  pl.BlockSpec((None, d, tn), lambda l, j: (l, 0, j)),
                  pl.BlockSpec((None, 1, tn), lambda l, j: (l, 0, j))],
        out_specs=pl.BlockSpec((None, ROW_PAD, tn), lambda l, j: (l, 0, j)),
        out_shape=jax.ShapeDtypeStruct((depth, ROW_PAD, n), F32),
        compiler_params=_cparams("parallel", "parallel"),
        name="adaln",
    )(cs, w_mod, b_mod.reshape(depth, 1, n))


def _mod_spec(is_ctx, batch):
    if is_ctx:
        return pl.BlockSpec((None, 6, D_MODEL), lambda b, i: (batch, 0, 0))
    return pl.BlockSpec((None, 6, D_MODEL), lambda b, i: (b, 0, 0))


IN_WIDTHS = (1024, 768, 768, 1024, SMALL_W)


def _in_body(h_ref, mod_ref, nw_ref, w_ref, ret_ref, gla_ref, ssd_ref, mls_ref, sm_ref):
    mod = mod_ref[...]
    xn = _rms(h_ref[...], nw_ref[...]) * (1.0 + mod[1:2, :]) + mod[0:1, :]
    u = jnp.dot(xn.astype(BF16), w_ref[...], preferred_element_type=F32)
    off = 0
    for ref, w in zip((ret_ref, gla_ref, ssd_ref, mls_ref, sm_ref), IN_WIDTHS):
        ref[...] = u[:, off:off + w]
        off += w


def _in_call(h, mod_l, nw, w_big, is_ctx):
    batch, t, d = h.shape
    tm = min(512, t)
    n = w_big.shape[1]
    return pl.pallas_call(
        _in_body,
        grid=(batch, t // tm),
        in_specs=[pl.BlockSpec((None, tm, d), lambda b, i: (b, i, 0)),
                  _mod_spec(is_ctx, batch),
                  pl.BlockSpec((1, d), lambda b, i: (0, 0)),
                  pl.BlockSpec((d, n), lambda b, i: (0, 0))],
        out_specs=[pl.BlockSpec((None, tm, w), lambda b, i: (b, i, 0)) for w in IN_WIDTHS],
        out_shape=[jax.ShapeDtypeStruct((batch, t, w), F32) for w in IN_WIDTHS],
        compiler_params=_cparams("parallel", "parallel"),
        name="in_proj",
    )(h, mod_l, nw, w_big)


def _arrange_w_in(w):
    ret_end = 4 * GROUP_W
    gla_main = ret_end + 2 * GLA_QK + 2 * GROUP_W
    gla_end = gla_main + GLA_RANK
    ssd_main = gla_end + GROUP_W + GROUP_W + 2 * SSD_BC
    ssd_end = ssd_main + 2 * N_HEADS
    mls_main = ssd_end + 4 * GROUP_W
    mls_end = mls_main + 4 * N_HEADS
    narrow = (gla_end - gla_main) + (ssd_end - ssd_main) + (mls_end - mls_main)
    pad = jnp.zeros((w.shape[0], SMALL_W - narrow), w.dtype)
    return jnp.concatenate([w[:, :ret_end], w[:, ret_end:gla_main], w[:, gla_end:ssd_main],
                            w[:, ssd_end:mls_main], w[:, gla_main:gla_end], w[:, ssd_main:ssd_end],
                            w[:, mls_main:mls_end], pad], axis=1).astype(BF16)


def _split3(x):
    hi = x.astype(BF16)
    r1 = x - hi.astype(F32)
    mid = r1.astype(BF16)
    return hi, mid, (r1 - mid.astype(F32)).astype(BF16)


def _iota2(shape, axis):
    return lax.broadcasted_iota(I32, shape, axis)


def _block_consts(rev):
    i = _iota2((CHUNK, GROUP_W), 0)
    j = _iota2((CHUNK, GROUP_W), 1) % HEAD_DIM
    return ((j >= i) if rev else (j <= i)), (j == i)


def _same_head():
    return (_iota2((GROUP_W, GROUP_W), 0) // HEAD_DIM) == (_iota2((GROUP_W, GROUP_W), 1) // HEAD_DIM)


def _cum_mat(rev):
    i = _iota2((CHUNK, 3 * CHUNK), 0)
    t = _iota2((CHUNK, 3 * CHUNK), 1) % CHUNK
    return ((t >= i) if rev else (t <= i)).astype(F32).astype(BF16)


def _expand_mat(first_lane, width=GROUP_W):
    return (_iota2((SMALL_W, width), 0) == first_lane + _iota2((SMALL_W, width), 1) // HEAD_DIM
            ).astype(F32).astype(BF16)


def _spread(x, expand):
    n = x.shape[0]
    y = jnp.dot(jnp.concatenate(_split3(x), axis=0), expand, preferred_element_type=F32)
    return (y[0:n] + y[n:2 * n]) + y[2 * n:3 * n]


def _cumsum_b(x_b, cum3):
    return jnp.dot(cum3, jnp.concatenate(_split3(x_b), axis=0), preferred_element_type=F32)


def _tile_heads(x):
    return jnp.concatenate([x] * N_HEADS, axis=0)


def _blockdiag(x, same_head):
    xb = _tile_heads(x.astype(BF16))
    return jnp.where(same_head, xb, jnp.zeros_like(xb))


def _lane_scores(lanes, same_head):
    for ln in lanes:
        if ln["want_out"]:
            ln["scores"] = _bdot_nt(ln["q"], _blockdiag(ln["k"], same_head))


def _decay_lanes(lanes, consts, s_ref):
    same_head = consts[0][2]
    for ln in lanes:
        mask_b, eye_b, _ = consts[ln["d"]]
        g_b = ln["g_b"]
        ln["g_tot"] = g_b[0:1] if ln["d"] == 1 else g_b[CHUNK - 1:CHUNK]
        if ln["want_out"]:
            g_row = jnp.sum(jnp.where(eye_b, g_b, 0.0), axis=0, keepdims=True)
            ln["p"] = ln["scores"] * jnp.exp(jnp.where(mask_b, g_b - g_row, -jnp.inf))
            ln["qg"] = ln["q"] * jnp.exp(g_b)
        ln["khat"] = ln["k"] * jnp.exp(ln["g_tot"] - g_b)
    for ln in lanes:
        ln["kv"] = jnp.where(same_head, _bdot_tn(ln["khat"], ln["v"]), 0.0)
    for ln in lanes:
        if ln["want_out"]:
            ln["pv"] = _bdot(ln["p"], _blockdiag(ln["v"], same_head))
    outs = []
    for ln in lanes:
        s_in = s_ref[ln["d"]]
        outs.append(ln["pv"] + _bdot(ln["qg"], s_in) if ln["want_out"] else None)
        s_ref[ln["d"]] = jnp.exp(ln["g_tot"]) * s_in + ln["kv"]
    return outs


def _seg_sum(x, same_head):
    seg = same_head.astype(F32).astype(BF16)
    hi = x.astype(BF16)
    lo = (x - hi.astype(F32)).astype(BF16)
    return jnp.dot(hi, seg, preferred_element_type=F32) + jnp.dot(lo, seg, preferred_element_type=F32)


def _head_rms(x, same_head, center):
    if center:
        x = x - _seg_sum(x, same_head) * (1.0 / HEAD_DIM)
    return x * lax.rsqrt(_seg_sum(x * x, same_head) * (1.0 / HEAD_DIM) + EPS)


def _scan_lanes(n_ctx, n_lat, multi_step):
    def run(is_lat, n):
        def it(s, carry):
            multi_step(is_lat, [(2 * s, 0), (n - 1 - 2 * s, 1), (2 * s + 1, 0), (n - 2 - 2 * s, 1)])
            return carry

        lax.fori_loop(0, n // 2, it, 0)

    run(False, n_ctx)
    run(True, n_lat)


def _post_tiles(t):
    tile = min(256, t)
    return [(r, tile) for r in range(0, t, tile)]


def _mixer_call(body, name, batch, t_lat, t_ctx, ctx_out, per_sample, shared, scratch):
    in_specs = [pl.BlockSpec((None,) + a.shape[1:], lambda b, nd=a.ndim: (b,) + (0,) * (nd - 1))
                for a in per_sample]
    in_specs += [pl.BlockSpec(a.shape, lambda b, nd=a.ndim: (0,) * nd) for a in shared]
    args = list(per_sample) + list(shared)
    out_shape = [jax.ShapeDtypeStruct((batch, t_lat, GROUP_W), F32)]
    out_specs = [pl.BlockSpec((None, t_lat, GROUP_W), lambda b: (b, 0, 0))]
    if ctx_out:
        out_shape.append(jax.ShapeDtypeStruct((batch, t_ctx, GROUP_W), F32))
        out_specs.append(pl.BlockSpec((None, t_ctx, GROUP_W), lambda b: (b, 0, 0)))
    return pl.pallas_call(
        body, grid=(batch,), in_specs=in_specs, out_specs=out_specs, out_shape=out_shape,
        scratch_shapes=scratch, compiler_params=_cparams("parallel"), name=name,
    )(*args)


def _rope(x, cos, sin_signed, first_half):
    return x * cos + jnp.where(first_half, pltpu.roll(x, x.shape[1] - 16, 1), pltpu.roll(x, 16, 1)) * sin_signed


def _scan_consts():
    same_head = _same_head()
    return tuple(_block_consts(rev) + (same_head,) for rev in (False, True))


def _ret_body(ul_ref, uc_ref, lg_ref, cos_ref, sin_ref, nw_ref, *rest, ctx_out):
    if ctx_out:
        yl_ref, yc_ref, s_ref, accl_ref, accc_ref = rest
    else:
        yl_ref, s_ref, accl_ref, accc_ref = rest
        yc_ref = None
    t_lat, t_ctx = ul_ref.shape[0], uc_ref.shape[0]
    s_ref[...] = jnp.zeros_like(s_ref)
    first_half = (_iota2((1, GROUP_W), 1) % 32) < 16
    consts = _scan_consts()
    row = _iota2((CHUNK, GROUP_W), 0).astype(F32)
    n_decays = (row + 1.0, CHUNK - row)

    def multi_step(is_lat, chunks):
        u_ref, acc_ref = (ul_ref, accl_ref) if is_lat else (uc_ref, accc_ref)
        lanes = []
        for c, d in chunks:
            r = pl.multiple_of(c * CHUNK, CHUNK)
            q = u_ref[pl.ds(r, CHUNK), 0:GROUP_W] * (HEAD_DIM ** -0.5)
            k = u_ref[pl.ds(r, CHUNK), GROUP_W:2 * GROUP_W]
            if is_lat:
                cos, sin = cos_ref[pl.ds(r, CHUNK), :], sin_ref[pl.ds(r, CHUNK), :]
                q, k = _rope(q, cos, sin, first_half), _rope(k, cos, sin, first_half)
            lanes.append(dict(d=d, r=r, want_out=is_lat or ctx_out, q=q, k=k,
                              v=u_ref[pl.ds(r, CHUNK), 2 * GROUP_W:3 * GROUP_W], g_b=n_decays[d] * lg_ref[d]))
        _lane_scores(lanes, consts[0][2])
        for ln, o in zip(lanes, _decay_lanes(lanes, consts, s_ref)):
            if ln["want_out"]:
                acc_ref[ln["d"], pl.ds(ln["r"], CHUNK), :] = o

    _scan_lanes(t_ctx // CHUNK, t_lat // CHUNK, multi_step)

    def post(u_ref, acc_ref, y_ref, t):
        for r, n in _post_tiles(t):
            o = _head_rms(acc_ref[0, r:r + n, :] + acc_ref[1, r:r + n, :], consts[0][2], True)
            y_ref[r:r + n, :] = o * nw_ref[...] * _silu(u_ref[r:r + n, 3 * GROUP_W:4 * GROUP_W])

    post(ul_ref, accl_ref, yl_ref, t_lat)
    if ctx_out:
        post(uc_ref, accc_ref, yc_ref, t_ctx)


def _state_scratch(n, t_lat, t_ctx):
    return ([pltpu.VMEM((2, GROUP_W, GROUP_W), F32)] * n
            + [pltpu.VMEM((2, t_lat, GROUP_W), F32), pltpu.VMEM((2, t_ctx, GROUP_W), F32)])


def _ret_call(u_lat, u_ctx, log_gamma, cos, sin, nw, ctx_out):
    batch, t_lat, _ = u_lat.shape
    t_ctx = u_ctx.shape[1]
    lg = jnp.repeat(log_gamma, HEAD_DIM, axis=1)[:, None, :]
    return _mixer_call(functools.partial(_ret_body, ctx_out=ctx_out), "retention", batch, t_lat, t_ctx, ctx_out,
                       [u_lat, u_ctx], [lg, cos, sin, nw], _state_scratch(1, t_lat, t_ctx))


def _rope_tables(t_lat):
    nf = 16
    inv = ROPE_BASE ** (-jnp.arange(nf, dtype=F32) / nf)
    rows = jnp.repeat(jnp.arange(t_lat // GRID_W), GRID_W).astype(F32)
    cols = jnp.tile(jnp.arange(GRID_W), t_lat // GRID_W).astype(F32)
    ang_r = rows[:, None] * inv[None, :]
    ang_c = cols[:, None] * inv[None, :]
    cos_h = jnp.concatenate([jnp.cos(ang_r)] * 2 + [jnp.cos(ang_c)] * 2, axis=1)
    sin_h = jnp.concatenate([-jnp.sin(ang_r), jnp.sin(ang_r), -jnp.sin(ang_c), jnp.sin(ang_c)], axis=1)
    return jnp.tile(cos_h, (1, N_HEADS)), jnp.tile(sin_h, (1, N_HEADS))


def _ssd_body(ul_ref, uc_ref, sml_ref, smc_ref, cw_ref, cb_ref, dtb_ref, an_ref, dsk_ref, nw_ref, *rest,
              ctx_out):
    if ctx_out:
        yl_ref, yc_ref, s_ref, accl_ref, accc_ref, xl_ref, xc_ref, pad_ref = rest
    else:
        yl_ref, s_ref, accl_ref, accc_ref, xl_ref, xc_ref, pad_ref = rest
        yc_ref = None
    t_lat, t_ctx = ul_ref.shape[0], uc_ref.shape[0]
    conv_ch = GROUP_W + 2 * SSD_BC
    half = (SSD_CONV - 1) // 2
    halo = 8

    def conv(u_ref, x_ref, t):
        pad_ref[0:halo, :] = jnp.zeros((halo, conv_ch), F32)
        pad_ref[halo + t:2 * halo + t, :] = jnp.zeros((halo, conv_ch), F32)
        for r, n in _post_tiles(t):
            pad_ref[halo + r:halo + r + n, :] = u_ref[r:r + n, GROUP_W:GROUP_W + conv_ch]
        for r, n in _post_tiles(t):
            acc = jnp.zeros((n, conv_ch), F32) + cb_ref[...]
            for tap in range(SSD_CONV):
                lo = halo + r + tap - half
                acc = acc + pad_ref[lo:lo + n, :] * cw_ref[tap:tap + 1, :]
            act = _silu(acc)
            grp = [act[:, GROUP_W + m * HEAD_DIM:GROUP_W + (m + 1) * HEAD_DIM] for m in range(4)]
            x_ref[r:r + n, 0:GROUP_W] = act[:, 0:GROUP_W]
            x_ref[r:r + n, GROUP_W:2 * GROUP_W] = jnp.concatenate([grp[0], grp[0], grp[1], grp[1]], axis=1)
            x_ref[r:r + n, 2 * GROUP_W:3 * GROUP_W] = jnp.concatenate([grp[2], grp[2], grp[3], grp[3]], axis=1)

    conv(uc_ref, xc_ref, t_ctx)
    conv(ul_ref, xl_ref, t_lat)
    s_ref[...] = jnp.zeros_like(s_ref)
    consts = _scan_consts()
    cum3 = (_cum_mat(False), _cum_mat(True))
    expand = (_expand_mat(DT_OFF), _expand_mat(DT_OFF + N_HEADS))

    def multi_step(is_lat, chunks):
        x_ref, sm_ref, acc_ref = (xl_ref, sml_ref, accl_ref) if is_lat else (xc_ref, smc_ref, accc_ref)
        lanes = []
        for c, d in chunks:
            r = pl.multiple_of(c * CHUNK, CHUNK)
            dt = _spread(_softplus(sm_ref[pl.ds(r, CHUNK), :] + dtb_ref[...]), expand[d])
            lanes.append(dict(d=d, r=r, want_out=is_lat or ctx_out, dt=dt,
                              k=x_ref[pl.ds(r, CHUNK), GROUP_W:2 * GROUP_W],
                              q=x_ref[pl.ds(r, CHUNK), 2 * GROUP_W:3 * GROUP_W]))
        _lane_scores(lanes, consts[0][2])
        for ln in lanes:
            ln["g_b"] = _cumsum_b(ln["dt"] * an_ref[ln["d"]], cum3[ln["d"]])
            ln["v"] = x_ref[pl.ds(ln["r"], CHUNK), 0:GROUP_W] * ln["dt"]
        for ln, o in zip(lanes, _decay_lanes(lanes, consts, s_ref)):
            if ln["want_out"]:
                acc_ref[ln["d"], pl.ds(ln["r"], CHUNK), :] = o

    _scan_lanes(t_ctx // CHUNK, t_lat // CHUNK, multi_step)

    def post(u_ref, x_ref, acc_ref, y_ref, t):
        for r, n in _post_tiles(t):
            y = acc_ref[0, r:r + n, :] + acc_ref[1, r:r + n, :] + dsk_ref[...] * x_ref[r:r + n, 0:GROUP_W]
            y_ref[r:r + n, :] = _rms(y * _silu(u_ref[r:r + n, 0:GROUP_W]), nw_ref[...])

    post(ul_ref, xl_ref, accl_ref, yl_ref, t_lat)
    if ctx_out:
        post(uc_ref, xc_ref, accc_ref, yc_ref, t_ctx)


def _ssd_call(u_lat, u_ctx, sm_lat, sm_ctx, conv_w, conv_b, dt_bias, a_log, d_skip, nw, ctx_out):
    batch, t_lat, _ = u_lat.shape
    t_ctx = u_ctx.shape[1]
    conv_ch = GROUP_W + 2 * SSD_BC
    dtb = jnp.zeros((SMALL_W,), F32).at[DT_OFF:DT_OFF + 2 * N_HEADS].set(dt_bias.reshape(-1))[None]
    a_neg = jnp.repeat(-jnp.exp(a_log.astype(F32)), HEAD_DIM, axis=1)[:, None, :]
    cw = jnp.concatenate([conv_w, jnp.zeros((8 - SSD_CONV, conv_ch), F32)], axis=0)
    dsk = jnp.repeat(d_skip, HEAD_DIM)[None]
    scratch = _state_scratch(1, t_lat, t_ctx) + [
        pltpu.VMEM((t_lat, 3 * GROUP_W), F32), pltpu.VMEM((t_ctx, 3 * GROUP_W), F32),
        pltpu.VMEM((t_lat + 16, conv_ch), F32)]
    return _mixer_call(functools.partial(_ssd_body, ctx_out=ctx_out), "ssd", batch, t_lat, t_ctx, ctx_out,
                       [u_lat, u_ctx, sm_lat, sm_ctx], [cw, conv_b[None], dtb, a_neg, dsk, nw], scratch)


def _mlstm_body(ul_ref, uc_ref, sml_ref, smc_ref, gb_ref, nw_ref, *rest, ctx_out):
    if ctx_out:
        yl_ref, yc_ref, c_ref, n_ref, m_ref, accl_ref, accc_ref = rest
    else:
        yl_ref, c_ref, n_ref, m_ref, accl_ref, accc_ref = rest
        yc_ref = None
    t_lat, t_ctx = ul_ref.shape[0], uc_ref.shape[0]
    c_ref[...] = jnp.zeros_like(c_ref)
    n_ref[...] = jnp.zeros_like(n_ref)
    m_ref[...] = jnp.zeros_like(m_ref)
    consts = _scan_consts()
    same_head = consts[0][2]
    seg = same_head.astype(F32).astype(BF16)
    cum3 = (_cum_mat(False), _cum_mat(True))
    f_lane = tuple(GATE_OFF + 2 * N_HEADS * d + N_HEADS for d in (0, 1))
    expand_fi = tuple(jnp.concatenate([_expand_mat(f_lane[d]), _expand_mat(f_lane[d] - N_HEADS)], axis=1)
                      for d in (0, 1))
    lane = _iota2((1, SMALL_W), 1)
    is_f = (lane >= GATE_OFF) & ((lane - GATE_OFF) % (2 * N_HEADS) >= N_HEADS)

    def running_max(x, rev):
        row = _iota2(x.shape, 0)
        s = 1
        while s < CHUNK:
            shifted = pltpu.roll(x, CHUNK - s if rev else s, 0)
            valid = (row < CHUNK - s) if rev else (row >= s)
            x = jnp.maximum(x, jnp.where(valid, shifted, -jnp.inf))
            s *= 2
        return x

    def multi_step(is_lat, chunks):
        u_ref, sm_ref, acc_ref = (ul_ref, sml_ref, accl_ref) if is_lat else (uc_ref, smc_ref, accc_ref)
        want_out = is_lat or ctx_out
        lanes = []
        for c, d in chunks:
            r = pl.multiple_of(c * CHUNK, CHUNK)
            gates = sm_ref[pl.ds(r, CHUNK), :] + gb_ref[...]
            both = _spread(jnp.where(is_f, _log_sigmoid(gates), gates), expand_fi[d])
            lanes.append(dict(d=d, r=r, want_out=want_out, lf_b=both[:, 0:GROUP_W], ig_b=both[:, GROUP_W:2 * GROUP_W],
                              q=u_ref[pl.ds(r, CHUNK), 0:GROUP_W],
                              k=u_ref[pl.ds(r, CHUNK), GROUP_W:2 * GROUP_W] * (HEAD_DIM ** -0.5),
                              v=u_ref[pl.ds(r, CHUNK), 2 * GROUP_W:3 * GROUP_W]))
        _lane_scores(lanes, same_head)
        for ln in lanes:
            ln["g_b"] = _cumsum_b(ln["lf_b"], cum3[ln["d"]])
        m_now, n_now = [m_ref[0], m_ref[1]], [n_ref[0], n_ref[1]]
        for ln in lanes:
            d, g_b, ig_b = ln["d"], ln["g_b"], ln["ig_b"]
            mask_b, eye_b, _ = consts[d]
            g_tot = g_b[0:1] if d == 1 else g_b[CHUNK - 1:CHUNK]
            w_end = g_tot - g_b + ig_b
            b_max = jnp.max(w_end, axis=0, keepdims=True)
            ln["ke"] = ln["k"] * jnp.exp(w_end - b_max)
            m_in, n_in = m_now[d], n_now[d]
            m_new = jnp.maximum(g_tot + m_in, b_max)
            ln["old"], ln["new"] = jnp.exp(g_tot + m_in - m_new), jnp.exp(b_max - m_new)
            m_now[d] = m_new
            n_now[d] = ln["old"] * n_in + ln["new"] * jnp.sum(ln["ke"], axis=0, keepdims=True)
            if want_out:
                m_inter = g_b + m_in
                m_tot = jnp.maximum(m_inter, g_b + running_max(ig_b - g_b, d == 1))
                g_row = jnp.sum(jnp.where(eye_b, g_b, 0.0), axis=0, keepdims=True)
                ig_row = jnp.sum(jnp.where(eye_b, ig_b, 0.0), axis=0, keepdims=True)
                d_log = jnp.where(mask_b, g_b - g_row + ig_row, -jnp.inf)
                ln["qk"] = ln["scores"] * jnp.exp(d_log - m_tot)
                ln["s_inter"] = jnp.exp(m_inter - m_tot)
                ln["floor"] = jnp.exp(-m_tot)
                ln["den_terms"] = ln["qk"] + ln["s_inter"] * (ln["q"] * n_in)
        m_ref[0], m_ref[1] = m_now
        n_ref[0], n_ref[1] = n_now
        for ln in lanes:
            ln["kv"] = jnp.where(same_head, _bdot_tn(ln["ke"], ln["v"]), 0.0)
        if want_out:
            for ln in lanes:
                ln["num"] = _bdot(ln["qk"], _blockdiag(ln["v"], same_head))
                ln["den"] = _seg_sum(ln["den_terms"], same_head)
        for ln in lanes:
            d = ln["d"]
            c_in = c_ref[d]
            if want_out:
                num = ln["num"] + ln["s_inter"] * _bdot(ln["q"], c_in)
                acc_ref[d, pl.ds(ln["r"], CHUNK), :] = num / jnp.maximum(jnp.abs(ln["den"]), ln["floor"])
            c_ref[d] = ln["old"] * c_in + ln["new"] * ln["kv"]

    _scan_lanes(t_ctx // CHUNK, t_lat // CHUNK, multi_step)

    def post(u_ref, acc_ref, y_ref, t):
        for r, n in _post_tiles(t):
            o = _head_rms(acc_ref[0, r:r + n, :] + acc_ref[1, r:r + n, :], same_head, False)
            y_ref[r:r + n, :] = jax.nn.sigmoid(u_ref[r:r + n, 3 * GROUP_W:4 * GROUP_W]) * (o * nw_ref[...])

    post(ul_ref, accl_ref, yl_ref, t_lat)
    if ctx_out:
        post(uc_ref, accc_ref, yc_ref, t_ctx)


def _mlstm_call(u_lat, u_ctx, sm_lat, sm_ctx, gate_b, nw, ctx_out):
    batch, t_lat, _ = u_lat.shape
    t_ctx = u_ctx.shape[1]
    gb = jnp.zeros((SMALL_W,), F32).at[GATE_OFF:GATE_OFF + 4 * N_HEADS].set(gate_b.reshape(-1))[None]
    scratch = ([pltpu.VMEM((2, GROUP_W, GROUP_W), F32), pltpu.VMEM((2, 1, GROUP_W), F32),
                pltpu.VMEM((2, 1, GROUP_W), F32)]
               + [pltpu.VMEM((2, t_lat, GROUP_W), F32), pltpu.VMEM((2, t_ctx, GROUP_W), F32)])
    return _mixer_call(functools.partial(_mlstm_body, ctx_out=ctx_out), "mlstm", batch, t_lat, t_ctx, ctx_out,
                       [u_lat, u_ctx, sm_lat, sm_ctx], [gb, nw], scratch)


def _gla_body(ul_ref, uc_ref, sml_ref, smc_ref, gw_ref, gb_ref, nw_ref, *rest, ctx_out):
    if ctx_out:
        yl_ref, yc_ref, st_ref, accl_ref, accc_ref, gl_ref, gc_ref = rest
    else:
        yl_ref, st_ref, accl_ref, accc_ref, gl_ref, gc_ref = rest
        yc_ref = None
    t_lat, t_ctx = ul_ref.shape[0], uc_ref.shape[0]
    st_ref[...] = jnp.zeros_like(st_ref)
    n_sub = CHUNK // GLA_SUB
    i = lax.broadcasted_iota(I32, (CHUNK, CHUNK), 0)
    j = lax.broadcasted_iota(I32, (CHUNK, CHUNK), 1)
    same = (i // GLA_SUB) == (j // GLA_SUB)
    lcum = ((same & (j <= i)).astype(F32).astype(BF16), (same & (j >= i)).astype(F32).astype(BF16))
    expand = ((lax.broadcasted_iota(I32, (GLA_QK, GROUP_W), 0) // GLA_DK)
              == (lax.broadcasted_iota(I32, (GLA_QK, GROUP_W), 1) // HEAD_DIM)).astype(F32).astype(BF16)
    blockdiag = ((lax.broadcasted_iota(I32, (GROUP_W, GLA_QK), 0) // HEAD_DIM)
                 == (lax.broadcasted_iota(I32, (GROUP_W, GLA_QK), 1) // GLA_DK)).astype(F32)
    sub_row = lax.broadcasted_iota(I32, (CHUNK, 1), 0) % GLA_SUB

    def roll_sub(x, delta):
        return jnp.concatenate([pltpu.roll(x[a * GLA_SUB:(a + 1) * GLA_SUB], delta, 0) for a in range(n_sub)],
                               axis=0)

    def qkv(u_ref, r):
        return (u_ref[pl.ds(r, CHUNK), 0:GLA_QK] * (GLA_DK ** -0.5), u_ref[pl.ds(r, CHUNK), GLA_QK:2 * GLA_QK],
                u_ref[pl.ds(r, CHUNK), 2 * GLA_QK:2 * GLA_QK + GROUP_W])

    def intra(u_ref, sm_ref, g_ref, acc_ref, want_out, c):
        r = pl.multiple_of(c * CHUNK, CHUNK)
        lr = sm_ref[pl.ds(r, CHUNK), LR_OFF:LR_OFF + GLA_RANK]
        g = []
        for d in (0, 1):
            la = _log_sigmoid(_bdot(lr, gw_ref[d]) + gb_ref[d]) / GLA_TAU
            g.append(sum(jnp.dot(lcum[d], piece, preferred_element_type=F32) for piece in _split3(la)))
            g_ref[d, pl.ds(r, CHUNK), :] = g[d]
        if not want_out:
            return
        q, k, v = qkv(u_ref, r)
        prods, vals = [2.0 * q * k], [v]
        for delta in range(1, GLA_SUB):
            fwd_pair = sub_row >= delta
            e = jnp.exp(jnp.where(fwd_pair, g[0] - roll_sub(g[0], delta), g[1] - roll_sub(g[1], delta)))
            prods.append(q * roll_sub(k, delta) * e)
            vals.append(roll_sub(v, delta))
        w = jnp.dot(jnp.concatenate(prods, axis=0).astype(BF16), expand, preferred_element_type=F32)
        acc_ref[pl.ds(r, CHUNK), :] = sum(w[n * CHUNK:(n + 1) * CHUNK] * vals[n] for n in range(GLA_SUB))

    def intra_ctx(c, carry):
        intra(uc_ref, smc_ref, gc_ref, accc_ref, ctx_out, c)
        return carry

    def intra_lat(c, carry):
        intra(ul_ref, sml_ref, gl_ref, accl_ref, True, c)
        return carry

    lax.fori_loop(0, t_ctx // CHUNK, intra_ctx, 0)
    lax.fori_loop(0, t_lat // CHUNK, intra_lat, 0, unroll=2)

    def multi_step(is_lat, chunks):
        u_ref, g_ref, acc_ref = (ul_ref, gl_ref, accl_ref) if is_lat else (uc_ref, gc_ref, accc_ref)
        want_out = is_lat or ctx_out
        lanes = []
        for c, d in chunks:
            r = pl.multiple_of(c * CHUNK, CHUNK)
            q, k, v = qkv(u_ref, r)
            g = g_ref[d, pl.ds(r, CHUNK), :]
            subs = []
            for a in (range(n_sub) if d == 0 else range(n_sub - 1, -1, -1)):
                rows = slice(a * GLA_SUB, (a + 1) * GLA_SUB)
                g_a = g[rows]
                g_end = g_a[GLA_SUB - 1:GLA_SUB] if d == 0 else g_a[0:1]
                subs.append(dict(a=a, decay=jnp.exp(g_end),
                                 kv=_bdot_tn(v[rows], k[rows] * jnp.exp(g_end - g_a)) * blockdiag,
                                 qg=q[rows] * jnp.exp(g_a) if want_out else None))
            lanes.append(dict(d=d, r=r, subs=subs, out=[None] * n_sub))
        for step_lanes in (lanes[0:2], lanes[2:4]):
            for n in range(n_sub):
                for ln in step_lanes:
                    sub, st = ln["subs"][n], st_ref[ln["d"]]
                    if want_out:
                        ln["out"][sub["a"]] = _bdot_nt(sub["qg"], st)
                    st_ref[ln["d"]] = st * sub["decay"] + sub["kv"]
        if want_out:
            for ln in lanes:
                acc_ref[pl.ds(ln["r"], CHUNK), :] += jnp.concatenate(ln["out"], axis=0)

    _scan_lanes(t_ctx // CHUNK, t_lat // CHUNK, multi_step)

    def post(u_ref, acc_ref, y_ref, t):
        for r, n in _post_tiles(t):
            o = _head_rms(acc_ref[r:r + n, :], _same_head(), False)
            y_ref[r:r + n, :] = o * nw_ref[...] * _silu(u_ref[r:r + n, 2 * GLA_QK + GROUP_W:2 * GLA_QK + 2 * GROUP_W])

    post(ul_ref, accl_ref, yl_ref, t_lat)
    if ctx_out:
        post(uc_ref, accc_ref, yc_ref, t_ctx)


def _gla_call(u_lat, u_ctx, sm_lat, sm_ctx, gate_w, gate_b, nw, ctx_out):
    batch, t_lat, _ = u_lat.shape
    t_ctx = u_ctx.shape[1]
    scratch = [pltpu.VMEM((2, GROUP_W, GLA_QK), F32),
               pltpu.VMEM((t_lat, GROUP_W), F32), pltpu.VMEM((t_ctx, GROUP_W), F32),
               pltpu.VMEM((2, t_lat, GLA_QK), F32), pltpu.VMEM((2, t_ctx, GLA_QK), F32)]
    return _mixer_call(functools.partial(_gla_body, ctx_out=ctx_out), "gla", batch, t_lat, t_ctx, ctx_out,
                       [u_lat, u_ctx, sm_lat, sm_ctx], [gate_w, gate_b[:, None, :], nw], scratch)


def _out_body(yr_ref, yg_ref, ys_ref, ym_ref, h_ref, mod_ref, wo_ref, n2_ref, rw_ref, h1_ref, hn_ref, lt_ref):
    mod = mod_ref[...]
    proj = sum(_bdot(y_ref[...], wo_ref[n * GROUP_W:(n + 1) * GROUP_W, :])
               for n, y_ref in enumerate((yr_ref, yg_ref, ys_ref, ym_ref)))
    h1 = h_ref[...] + mod[2:3, :] * proj
    h1_ref[...] = h1
    xn = _rms(h1, n2_ref[...]) * (1.0 + mod[4:5, :]) + mod[3:4, :]
    hn_ref[...] = xn.astype(BF16)
    x_hi, x_mid, _ = _split3(xn)
    parts = (jnp.dot(x_hi, rw_ref[...], preferred_element_type=F32)
             + jnp.dot(x_mid, rw_ref[...], preferred_element_type=F32)).T
    lt_ref[...] = parts[0:N_EXPERTS] + parts[N_EXPERTS:2 * N_EXPERTS] + parts[2 * N_EXPERTS:3 * N_EXPERTS]


def _out_call(ys, h, mod_l, wo, n2w, rw3, is_ctx):
    batch, t, d = h.shape
    tm = min(512, t)
    y_spec = pl.BlockSpec((None, tm, GROUP_W), lambda b, i: (b, i, 0))
    return pl.pallas_call(
        _out_body,
        grid=(batch, t // tm),
        in_specs=[y_spec, y_spec, y_spec, y_spec,
                  pl.BlockSpec((None, tm, d), lambda b, i: (b, i, 0)),
                  _mod_spec(is_ctx, batch),
                  pl.BlockSpec((d, d), lambda b, i: (0, 0)),
                  pl.BlockSpec((1, d), lambda b, i: (0, 0)),
                  pl.BlockSpec(rw3.shape, lambda b, i: (0, 0))],
        out_specs=[pl.BlockSpec((None, tm, d), lambda b, i: (b, i, 0)),
                   pl.BlockSpec((None, tm, d), lambda b, i: (b, i, 0)),
                   pl.BlockSpec((None, N_EXPERTS, tm), lambda b, i: (b, 0, i))],
        out_shape=[jax.ShapeDtypeStruct((batch, t, d), F32),
                   jax.ShapeDtypeStruct((batch, t, d), BF16),
                   jax.ShapeDtypeStruct((batch, N_EXPERTS, t), F32)],
        compiler_params=_cparams("parallel", "parallel"),
        name="out_proj",
    )(*ys, h, mod_l, wo, n2w, rw3)


def _excl_prefix(flags, t):
    blk = min(256, t)
    below = (lax.broadcasted_iota(I32, (blk, blk), 0) < lax.broadcasted_iota(I32, (blk, blk), 1))
    below = below.astype(F32).astype(BF16)
    outs = []
    carry = jnp.zeros((flags.shape[0], 1), F32)
    for n in range(t // blk):
        f = flags[:, n * blk:(n + 1) * blk]
        outs.append(jnp.dot(f.astype(BF16), below, preferred_element_type=F32) + carry)
        carry = carry + jnp.sum(f, axis=1, keepdims=True)
    return jnp.concatenate(outs, axis=1)


def _route_body(lt_ref, hn_ref, xg_ref, affc_ref, post_ref, pos_s, aff_s, *, cap):
    t = hn_ref.shape[0]
    logits = lt_ref[...]
    ex = jnp.exp(logits - jnp.max(logits, axis=0, keepdims=True))
    aff = ex / jnp.sum(ex, axis=0, keepdims=True)
    def count_at_least(cand):
        return jnp.sum((aff >= pltpu.bitcast(cand, F32)).astype(F32), axis=1, keepdims=True)

    def fix_two_bits(n, thr):
        lo = 28 - 2 * n
        best = thr
        for digit in (1, 2, 3):
            cand = thr | (jnp.int32(digit) << lo)
            best = jnp.where(count_at_least(cand) >= float(cap), cand, best)
        return best

    thr = lax.fori_loop(0, 15, fix_two_bits, jnp.zeros((N_EXPERTS, 1), I32))
    above = (aff >= pltpu.bitcast(thr + 1, F32)).astype(F32)
    tied = (aff >= pltpu.bitcast(thr, F32)).astype(F32) - above
    need = float(cap) - jnp.sum(above, axis=1, keepdims=True)
    sel = above + tied * (_excl_prefix(tied, t) < need).astype(F32)
    pos = jnp.where(sel > 0.0, _excl_prefix(sel, t), -1.0)
    pos_s[...] = pos
    aff_s[...] = aff
    post_ref[...] = jnp.concatenate([pos, jnp.full((SMALL_W - N_EXPERTS, t), -1.0, F32)], axis=0).T
    slots = lax.broadcasted_iota(I32, (cap, t), 0).astype(F32)

    def gather(e, carry):
        hit = slots == pos_s[pl.ds(e, 1), :]
        xg_ref[e] = jnp.dot(hit.astype(BF16), hn_ref[...], preferred_element_type=F32).astype(BF16)
        a = jnp.sum(jnp.where(hit, aff_s[pl.ds(e, 1), :], 0.0), axis=1, keepdims=True)
        affc_ref[e] = jnp.broadcast_to(a, (cap, SMALL_W))
        return carry

    lax.fori_loop(0, N_EXPERTS, gather, 0, unroll=2)


def _route_call(lt, hn, cap):
    batch, t, d = hn.shape
    return pl.pallas_call(
        functools.partial(_route_body, cap=cap),
        grid=(batch,),
        in_specs=[pl.BlockSpec((None, N_EXPERTS, t), lambda b: (b, 0, 0)),
                  pl.BlockSpec((None, t, d), lambda b: (b, 0, 0))],
        out_specs=[pl.BlockSpec((N_EXPERTS, cap, d), lambda b: (0, b, 0)),
                   pl.BlockSpec((N_EXPERTS, cap, SMALL_W), lambda b: (0, b, 0)),
                   pl.BlockSpec((None, t, SMALL_W), lambda b: (b, 0, 0))],
        out_shape=[jax.ShapeDtypeStruct((N_EXPERTS, batch * cap, d), BF16),
                   jax.ShapeDtypeStruct((N_EXPERTS, batch * cap, SMALL_W), F32),
                   jax.ShapeDtypeStruct((batch, t, SMALL_W), F32)],
        scratch_shapes=[pltpu.VMEM((N_EXPERTS, t), F32), pltpu.VMEM((N_EXPERTS, t), F32)],
        compiler_params=_cparams("parallel"),
        name="route_gather",
    )(lt, hn)


FFN_TF = 512
FFN_TM = 512


def _ffn_body(*refs, n_sets):
    ins = refs[:2 * n_sets]
    wg_ref, wu_ref, wd_ref = refs[2 * n_sets:2 * n_sets + 3]
    outs = refs[2 * n_sets + 3:3 * n_sets + 3]
    accs = refs[3 * n_sets + 3:4 * n_sets + 3]
    wg_s, wu_s, wd_s = refs[4 * n_sets + 3:]
    f = pl.program_id(1)
    wg_s[...] = wg_ref[...].astype(BF16)
    wu_s[...] = wu_ref[...].astype(BF16)
    wd_s[...] = wd_ref[...].astype(BF16)
    dot = functools.partial(jnp.dot, preferred_element_type=F32)
    for s in range(n_sets):
        x_ref, a_ref, y_ref, acc_ref = ins[2 * s], ins[2 * s + 1], outs[s], accs[s]
        rows = x_ref.shape[0]

        @pl.when(f == 0)
        def _():
            acc_ref[...] = jnp.zeros_like(acc_ref)

        for r in range(0, rows, FFN_TM):
            n = min(FFN_TM, rows - r)
            x = x_ref[r:r + n, :]
            hid = _silu(dot(x, wg_s[...])) * dot(x, wu_s[...])
            acc_ref[r:r + n, :] += dot(hid.astype(BF16), wd_s[...])

        @pl.when(f == pl.num_programs(1) - 1)
        def _():
            for r in range(0, rows, FFN_TM):
                n = min(FFN_TM, rows - r)
                y_ref[r:r + n, :] = (acc_ref[r:r + n, :] * a_ref[r:r + n, 0:1]).astype(BF16)


def _ffn_call(sets, wg, wu, wd, layer):
    _, n_exp, d, ff = wg.shape
    in_specs, args, out_specs, out_shape, scratch = [], [], [], [], []
    for xg, affc in sets:
        m = xg.shape[1]
        in_specs += [pl.BlockSpec((None, m, d), lambda e, f: (e, 0, 0)),
                     pl.BlockSpec((None, m, SMALL_W), lambda e, f: (e, 0, 0))]
        args += [xg, affc]
        out_specs.append(pl.BlockSpec((None, m, d), lambda e, f: (e, 0, 0)))
        out_shape.append(jax.ShapeDtypeStruct((n_exp, m, d), BF16))
        scratch.append(pltpu.VMEM((m, d), F32))
    in_specs += [pl.BlockSpec((None, None, d, FFN_TF), lambda e, f: (layer, e, 0, f)),
                 pl.BlockSpec((None, None, d, FFN_TF), lambda e, f: (layer, e, 0, f)),
                 pl.BlockSpec((None, None, FFN_TF, d), lambda e, f: (layer, e, f, 0))]
    scratch += [pltpu.VMEM((d, FFN_TF), BF16), pltpu.VMEM((d, FFN_TF), BF16), pltpu.VMEM((FFN_TF, d), BF16)]
    return pl.pallas_call(
        functools.partial(_ffn_body, n_sets=len(sets)),
        grid=(n_exp, ff // FFN_TF),
        in_specs=in_specs, out_specs=out_specs, out_shape=out_shape, scratch_shapes=scratch,
        compiler_params=_cparams("parallel", "arbitrary"),
        name="expert_ffn",
    )(*args, wg, wu, wd)


def _scatter_body(post_ref, y_ref, h1_ref, mod_ref, fw_ref, o_ref, *, final):
    n_exp, cap, d = y_ref.shape
    pos = post_ref[...]
    slots = lax.broadcasted_iota(I32, (pos.shape[0], cap), 1).astype(F32)
    hit = jnp.concatenate([(pos[:, e:e+1] == slots).astype(BF16) for e in range(n_exp)], axis=1)
    upd = jnp.dot(hit, y_ref[...].reshape(n_exp * cap, d), preferred_element_type=F32)
    h2 = h1_ref[...] + mod_ref[5:6, :] * upd
    o_ref[...] = _rms(h2, fw_ref[...]) if final else h2


def _scatter_call(post, y, h1, mod_l, is_ctx, final_w):
    batch, t, d = h1.shape
    n_exp = y.shape[0]
    cap = y.shape[1] // batch
    tm = min(256, t)
    final = final_w is not None
    fw = final_w if final else jnp.ones((1, d), F32)
    return pl.pallas_call(
        functools.partial(_scatter_body, final=final),
        grid=(batch, t // tm),
        in_specs=[pl.BlockSpec((None, tm, SMALL_W), lambda b, i: (b, i, 0)),
                  pl.BlockSpec((n_exp, cap, d), lambda b, i: (0, b, 0)),
                  pl.BlockSpec((None, tm, d), lambda b, i: (b, i, 0)),
                  _mod_spec(is_ctx, batch),
                  pl.BlockSpec((1, d), lambda b, i: (0, 0))],
        out_specs=pl.BlockSpec((None, tm, d), lambda b, i: (b, i, 0)),
        out_shape=jax.ShapeDtypeStruct((batch, t, d), F32),
        compiler_params=_cparams("parallel", "parallel"),
        name="scatter_residual",
    )(post, y, h1, mod_l, fw)


def kernel(x, c, ctx, c_ctx, w_mod, b_mod, norm1_w, norm2_w, w_in, w_out, ret_decay_logit, ret_norm_w,
           gla_gate_w, gla_gate_b, gla_norm_w, ssd_conv_w, ssd_conv_b, ssd_dt_bias, ssd_a_log, ssd_d,
           ssd_norm_w, mlstm_gate_b, mlstm_norm_w, router_w, expert_w_gate, expert_w_up, expert_w_down,
           final_norm_w):
    batch, t_lat, d = x.shape
    t_ctx = ctx.shape[1]
    depth = w_mod.shape[0]
    assert batch + 1 <= ROW_PAD and d == D_MODEL
    assert t_lat % 256 == 0 and t_ctx % 128 == 0
    cs = jnp.concatenate([c, c_ctx[None], jnp.zeros((ROW_PAD - batch - 1, d), F32)], axis=0)
    mod = _mod_call(cs, w_mod, b_mod).reshape(depth, ROW_PAD, 6, d)
    cos, sin = _rope_tables(t_lat)
    cap_lat = EC_CAPACITY * t_lat // N_EXPERTS
    cap_ctx = EC_CAPACITY * t_ctx // N_EXPERTS
    h_lat, h_ctx = x, ctx
    for l in range(depth):
        ctx_out = l < depth - 1
        mod_l = mod[l]
        w_big = _arrange_w_in(w_in[l])
        ul = _in_call(h_lat, mod_l, norm1_w[l][None], w_big, False)
        uc = _in_call(h_ctx, mod_l, norm1_w[l][None], w_big, True)
        ys = [_ret_call(ul[0], uc[0], jax.nn.log_sigmoid(ret_decay_logit[l].astype(F32)), cos, sin,
                        ret_norm_w[l][None], ctx_out),
              _gla_call(ul[1], uc[1], ul[4], uc[4], gla_gate_w[l], gla_gate_b[l], gla_norm_w[l][None], ctx_out),
              _ssd_call(ul[2], uc[2], ul[4], uc[4], ssd_conv_w[l], ssd_conv_b[l], ssd_dt_bias[l], ssd_a_log[l],
                        ssd_d[l], ssd_norm_w[l][None], ctx_out),
              _mlstm_call(ul[3], uc[3], ul[4], uc[4], mlstm_gate_b[l], mlstm_norm_w[l][None], ctx_out)]
        wo = w_out[l].astype(BF16)
        rw3 = jnp.pad(jnp.concatenate(_split3(router_w[l]), axis=1), ((0, 0), (0, SMALL_W - 3 * N_EXPERTS)))
        n2w = norm2_w[l][None]
        h1_lat, hn_lat, lt_lat = _out_call([y[0] for y in ys], h_lat, mod_l, wo, n2w, rw3, False)
        xg_lat, affc_lat, post_lat = _route_call(lt_lat, hn_lat, cap_lat)
        sets = [(xg_lat, affc_lat)]
        if ctx_out:
            h1_ctx, hn_ctx, lt_ctx = _out_call([y[1] for y in ys], h_ctx, mod_l, wo, n2w, rw3, True)
            xg_ctx, affc_ctx, post_ctx = _route_call(lt_ctx, hn_ctx, cap_ctx)
            sets.append((xg_ctx, affc_ctx))
        y_exp = _ffn_call(sets, expert_w_gate, expert_w_up, expert_w_down, l)
        h_lat = _scatter_call(post_lat, y_exp[0], h1_lat, mod_l, False,
                              None if ctx_out else final_norm_w[None])
        if ctx_out:
            h_ctx = _scatter_call(post_ctx, y_exp[1], h1_ctx, mod_l, True, None)
    return h_lat
```

```python
import functools
import math

import jax
import jax.numpy as jnp
from jax import lax
from jax.experimental import pallas as pl
from jax.experimental.pallas import tpu as pltpu

F32 = jnp.float32
BF16 = jnp.bfloat16
I32 = jnp.int32

D_MODEL = 1024
GRID_W = 64
EPS = 1e-6
ROPE_BASE = 10000.0
GROUP_W = 256
HEAD_DIM = 64
N_HEADS = 4
CHUNK = 64
GLA_DK = 32
GLA_QK = 128
GLA_RANK = 16
GLA_TAU = 16.0
GLA_SUB = 16
SSD_CONV = 5
SSD_BC = 128
N_EXPERTS = 16
EC_CAPACITY = 2
EXPERT_FF = 1536
SMALL_W = 128
LR_OFF, DT_OFF, GATE_OFF = 0, 16, 24
ROW_PAD = 16
VMEM_LIMIT = 56 * 1024 * 1024


def _cparams(*sem):
    return pltpu.CompilerParams(dimension_semantics=sem, vmem_limit_bytes=VMEM_LIMIT)


def _bdot(a, b):
    return jnp.dot(a.astype(BF16), b.astype(BF16), preferred_element_type=F32)


def _bdot_nt(a, b):
    return lax.dot_general(a.astype(BF16), b.astype(BF16), (((1,), (1,)), ((), ())),
                           preferred_element_type=F32)


def _bdot_tn(a, b):
    return lax.dot_general(a.astype(BF16), b.astype(BF16), (((0,), (0,)), ((), ())),
                           preferred_element_type=F32)


def _silu(x):
    return x * jax.nn.sigmoid(x)


def _softplus(x):
    return jnp.maximum(x, 0.0) + jnp.log1p(jnp.exp(-jnp.abs(x)))


def _log_sigmoid(x):
    return -_softplus(-x)


def _rms(x, w):
    return x * lax.rsqrt(jnp.mean(x * x, axis=-1, keepdims=True) + EPS) * w


def _mod_body(cs_ref, w_ref, b_ref, o_ref):
    o_ref[...] = _bdot(_silu(cs_ref[...]), w_ref[...]) + b_ref[...]


def _mod_call(cs, w_mod, b_mod):
    depth, d, n = w_mod.shape
    tn = 1536
    return pl.pallas_call(
        _mod_body,
        grid=(depth, n // tn),
        in_specs=[pl.BlockSpec((ROW_PAD, d), lambda l, j: (0, 0)),
                  pl.BlockSpec((None, d, tn), lambda l, j: (l, 0, j)),
                  pl.BlockSpec((None, 1, tn), lambda l, j: (l, 0, j))],
        out_specs=pl.BlockSpec((None, ROW_PAD, tn), lambda l, j: (l, 0, j)),
        out_shape=jax.ShapeDtypeStruct((depth, ROW_PAD, n), F32),
        compiler_params=_cparams("parallel", "parallel"),
        name="adaln",
    )(cs, w_mod, b_mod.reshape(depth, 1, n))


def _mod_spec(is_ctx, batch):
    if is_ctx:
        return pl.BlockSpec((None, 6, D_MODEL), lambda b, i: (batch, 0, 0))
    return pl.BlockSpec((None, 6, D_MODEL), lambda b, i: (b, 0, 0))


IN_WIDTHS = (1024, 768, 768, 1024, SMALL_W)


def _in_body(h_ref, mod_ref, nw_ref, w_ref, ret_ref, gla_ref, ssd_ref, mls_ref, sm_ref):
    mod = mod_ref[...]
    xn = _rms(h_ref[...], nw_ref[...]) * (1.0 + mod[1:2, :]) + mod[0:1, :]
    u = jnp.dot(xn.astype(BF16), w_ref[...], preferred_element_type=F32)
    off = 0
    for ref, w in zip((ret_ref, gla_ref, ssd_ref, mls_ref, sm_ref), IN_WIDTHS):
        ref[...] = u[:, off:off + w]
        off += w


def _in_call(h, mod_l, nw, w_big, is_ctx):
    batch, t, d = h.shape
    tm = min(512, t)
    n = w_big.shape[1]
    return pl.pallas_call(
        _in_body,
        grid=(batch, t // tm),
        in_specs=[pl.BlockSpec((None, tm, d), lambda b, i: (b, i, 0)),
                  _mod_spec(is_ctx, batch),
                  pl.BlockSpec((1, d), lambda b, i: (0, 0)),
                  pl.BlockSpec((d, n), lambda b, i: (0, 0))],
        out_specs=[pl.BlockSpec((None, tm, w), lambda b, i: (b, i, 0)) for w in IN_WIDTHS],
        out_shape=[jax.ShapeDtypeStruct((batch, t, w), F32) for w in IN_WIDTHS],
        compiler_params=_cparams("parallel", "parallel"),
        name="in_proj",
    )(h, mod_l, nw, w_big)


def _arrange_w_in(w):
    ret_end = 4 * GROUP_W
    gla_main = ret_end + 2 * GLA_QK + 2 * GROUP_W
    gla_end = gla_main + GLA_RANK
    ssd_main = gla_end + GROUP_W + GROUP_W + 2 * SSD_BC
    ssd_end = ssd_main + 2 * N_HEADS
    mls_main = ssd_end + 4 * GROUP_W
    mls_end = mls_main + 4 * N_HEADS
    narrow = (gla_end - gla_main) + (ssd_end - ssd_main) + (mls_end - mls_main)
    pad = jnp.zeros((w.shape[0], SMALL_W - narrow), w.dtype)
    return jnp.concatenate([w[:, :ret_end], w[:, ret_end:gla_main], w[:, gla_end:ssd_main],
                            w[:, ssd_end:mls_main], w[:, gla_main:gla_end], w[:, ssd_main:ssd_end],
                            w[:, mls_main:mls_end], pad], axis=1).astype(BF16)


def _split3(x):
    hi = x.astype(BF16)
    r1 = x - hi.astype(F32)
    mid = r1.astype(BF16)
    return hi, mid, (r1 - mid.astype(F32)).astype(BF16)


def _iota2(shape, axis):
    return lax.broadcasted_iota(I32, shape, axis)


def _block_consts(rev):
    i = _iota2((CHUNK, GROUP_W), 0)
    j = _iota2((CHUNK, GROUP_W), 1) % HEAD_DIM
    return ((j >= i) if rev else (j <= i)), (j == i)


def _same_head():
    return (_iota2((GROUP_W, GROUP_W), 0) // HEAD_DIM) == (_iota2((GROUP_W, GROUP_W), 1) // HEAD_DIM)


def _cum_mat(rev):
    i = _iota2((CHUNK, 3 * CHUNK), 0)
    t = _iota2((CHUNK, 3 * CHUNK), 1) % CHUNK
    return ((t >= i) if rev else (t <= i)).astype(F32).astype(BF16)


def _expand_mat(first_lane, width=GROUP_W):
    return (_iota2((SMALL_W, width), 0) == first_lane + _iota2((SMALL_W, width), 1) // HEAD_DIM
            ).astype(F32).astype(BF16)


def _spread(x, expand):
    n = x.shape[0]
    y = jnp.dot(jnp.concatenate(_split3(x), axis=0), expand, preferred_element_type=F32)
    return (y[0:n] + y[n:2 * n]) + y[2 * n:3 * n]


def _cumsum_b(x_b, cum3):
    return jnp.dot(cum3, jnp.concatenate(_split3(x_b), axis=0), preferred_element_type=F32)


def _tile_heads(x):
    return jnp.concatenate([x] * N_HEADS, axis=0)


def _blockdiag(x, same_head):
    xb = _tile_heads(x.astype(BF16))
    return jnp.where(same_head, xb, jnp.zeros_like(xb))


def _lane_scores(lanes, same_head):
    for ln in lanes:
        if ln["want_out"]:
            ln["scores"] = _bdot_nt(ln["q"], _blockdiag(ln["k"], same_head))


def _decay_lanes(lanes, consts, s_ref):
    same_head = consts[0][2]
    for ln in lanes:
        mask_b, eye_b, _ = consts[ln["d"]]
        g_b = ln["g_b"]
        ln["g_tot"] = g_b[0:1] if ln["d"] == 1 else g_b[CHUNK - 1:CHUNK]
        if ln["want_out"]:
            g_row = jnp.sum(jnp.where(eye_b, g_b, 0.0), axis=0, keepdims=True)
            ln["p"] = ln["scores"] * jnp.exp(jnp.where(mask_b, g_b - g_row, -jnp.inf))
            ln["qg"] = ln["q"] * jnp.exp(g_b)
        ln["khat"] = ln["k"] * jnp.exp(ln["g_tot"] - g_b)
    for ln in lanes:
        ln["kv"] = jnp.where(same_head, _bdot_tn(ln["khat"], ln["v"]), 0.0)
    for ln in lanes:
        if ln["want_out"]:
            ln["pv"] = _bdot(ln["p"], _blockdiag(ln["v"], same_head))
    outs = []
    for ln in lanes:
        s_in = s_ref[ln["d"]]
        outs.append(ln["pv"] + _bdot(ln["qg"], s_in) if ln["want_out"] else None)
        s_ref[ln["d"]] = jnp.exp(ln["g_tot"]) * s_in + ln["kv"]
    return outs


def _seg_sum(x, same_head):
    seg = same_head.astype(F32).astype(BF16)
    hi = x.astype(BF16)
    lo = (x - hi.astype(F32)).astype(BF16)
    return jnp.dot(hi, seg, preferred_element_type=F32) + jnp.dot(lo, seg, preferred_element_type=F32)


def _head_rms(x, same_head, center):
    if center:
        x = x - _seg_sum(x, same_head) * (1.0 / HEAD_DIM)
    return x * lax.rsqrt(_seg_sum(x * x, same_head) * (1.0 / HEAD_DIM) + EPS)


SCAN_STEPS = 4


def _scan_lanes(n_ctx, n_lat, multi_step):
    def run(is_lat, n):
        def it(s, carry):
            lanes = []
            for j in range(SCAN_STEPS):
                lanes += [(SCAN_STEPS * s + j, 0), (n - 1 - SCAN_STEPS * s - j, 1)]
            multi_step(is_lat, lanes)
            return carry

        lax.fori_loop(0, n // SCAN_STEPS, it, 0)

    run(False, n_ctx)
    run(True, n_lat)


def _post_tiles(t):
    tile = min(256, t)
    return [(r, tile) for r in range(0, t, tile)]


def _mixer_call(body, name, batch, t_lat, t_ctx, ctx_out, per_sample, shared, scratch):
    in_specs = [pl.BlockSpec((None,) + a.shape[1:], lambda b, nd=a.ndim: (b,) + (0,) * (nd - 1))
                for a in per_sample]
    in_specs += [pl.BlockSpec(a.shape, lambda b, nd=a.ndim: (0,) * nd) for a in shared]
    args = list(per_sample) + list(shared)
    out_shape = [jax.ShapeDtypeStruct((batch, t_lat, GROUP_W), F32)]
    out_specs = [pl.BlockSpec((None, t_lat, GROUP_W), lambda b: (b, 0, 0))]
    if ctx_out:
        out_shape.append(jax.ShapeDtypeStruct((batch, t_ctx, GROUP_W), F32))
        out_specs.append(pl.BlockSpec((None, t_ctx, GROUP_W), lambda b: (b, 0, 0)))
    return pl.pallas_call(
        body, grid=(batch,), in_specs=in_specs, out_specs=out_specs, out_shape=out_shape,
        scratch_shapes=scratch, compiler_params=_cparams("parallel"), name=name,
    )(*args)


def _rope(x, cos, sin_signed, first_half):
    return x * cos + jnp.where(first_half, pltpu.roll(x, x.shape[1] - 16, 1), pltpu.roll(x, 16, 1)) * sin_signed


def _scan_consts():
    same_head = _same_head()
    return tuple(_block_consts(rev) + (same_head,) for rev in (False, True))


def _ret_body(ul_ref, uc_ref, lg_ref, cos_ref, sin_ref, nw_ref, *rest, ctx_out):
    if ctx_out:
        yl_ref, yc_ref, s_ref, accl_ref, accc_ref = rest
    else:
        yl_ref, s_ref, accl_ref, accc_ref = rest
        yc_ref = None
    t_lat, t_ctx = ul_ref.shape[0], uc_ref.shape[0]
    s_ref[...] = jnp.zeros_like(s_ref)
    first_half = (_iota2((1, GROUP_W), 1) % 32) < 16
    consts = _scan_consts()
    row = _iota2((CHUNK, GROUP_W), 0).astype(F32)
    n_decays = (row + 1.0, CHUNK - row)

    def multi_step(is_lat, chunks):
        u_ref, acc_ref = (ul_ref, accl_ref) if is_lat else (uc_ref, accc_ref)
        lanes = []
        for c, d in chunks:
            r = pl.multiple_of(c * CHUNK, CHUNK)
            q = u_ref[pl.ds(r, CHUNK), 0:GROUP_W] * (HEAD_DIM ** -0.5)
            k = u_ref[pl.ds(r, CHUNK), GROUP_W:2 * GROUP_W]
            if is_lat:
                cos, sin = cos_ref[pl.ds(r, CHUNK), :], sin_ref[pl.ds(r, CHUNK), :]
                q, k = _rope(q, cos, sin, first_half), _rope(k, cos, sin, first_half)
            lanes.append(dict(d=d, r=r, want_out=is_lat or ctx_out, q=q, k=k,
                              v=u_ref[pl.ds(r, CHUNK), 2 * GROUP_W:3 * GROUP_W], g_b=n_decays[d] * lg_ref[d]))
        _lane_scores(lanes, consts[0][2])
        for ln, o in zip(lanes, _decay_lanes(lanes, consts, s_ref)):
            if ln["want_out"]:
                acc_ref[ln["d"], pl.ds(ln["r"], CHUNK), :] = o

    _scan_lanes(t_ctx // CHUNK, t_lat // CHUNK, multi_step)

    def post(u_ref, acc_ref, y_ref, t):
        for r, n in _post_tiles(t):
            o = _head_rms(acc_ref[0, r:r + n, :] + acc_ref[1, r:r + n, :], consts[0][2], True)
            y_ref[r:r + n, :] = o * nw_ref[...] * _silu(u_ref[r:r + n, 3 * GROUP_W:4 * GROUP_W])

    post(ul_ref, accl_ref, yl_ref, t_lat)
    if ctx_out:
        post(uc_ref, accc_ref, yc_ref, t_ctx)


def _state_scratch(n, t_lat, t_ctx):
    return ([pltpu.VMEM((2, GROUP_W, GROUP_W), F32)] * n
            + [pltpu.VMEM((2, t_lat, GROUP_W), F32), pltpu.VMEM((2, t_ctx, GROUP_W), F32)])


def _ret_call(u_lat, u_ctx, log_gamma, cos, sin, nw, ctx_out):
    batch, t_lat, _ = u_lat.shape
    t_ctx = u_ctx.shape[1]
    lg = jnp.repeat(log_gamma, HEAD_DIM, axis=1)[:, None, :]
    return _mixer_call(functools.partial(_ret_body, ctx_out=ctx_out), "retention", batch, t_lat, t_ctx, ctx_out,
                       [u_lat, u_ctx], [lg, cos, sin, nw], _state_scratch(1, t_lat, t_ctx))


def _rope_tables(t_lat):
    nf = 16
    inv = ROPE_BASE ** (-jnp.arange(nf, dtype=F32) / nf)
    rows = jnp.repeat(jnp.arange(t_lat // GRID_W), GRID_W).astype(F32)
    cols = jnp.tile(jnp.arange(GRID_W), t_lat // GRID_W).astype(F32)
    ang_r = rows[:, None] * inv[None, :]
    ang_c = cols[:, None] * inv[None, :]
    cos_h = jnp.concatenate([jnp.cos(ang_r)] * 2 + [jnp.cos(ang_c)] * 2, axis=1)
    sin_h = jnp.concatenate([-jnp.sin(ang_r), jnp.sin(ang_r), -jnp.sin(ang_c), jnp.sin(ang_c)], axis=1)
    return jnp.tile(cos_h, (1, N_HEADS)), jnp.tile(sin_h, (1, N_HEADS))


def _ssd_body(ul_ref, uc_ref, sml_ref, smc_ref, cw_ref, cb_ref, dtb_ref, an_ref, dsk_ref, nw_ref, *rest,
              ctx_out):
    if ctx_out:
        yl_ref, yc_ref, s_ref, accl_ref, accc_ref, xl_ref, xc_ref, pad_ref = rest
    else:
        yl_ref, s_ref, accl_ref, accc_ref, xl_ref, xc_ref, pad_ref = rest
        yc_ref = None
    t_lat, t_ctx = ul_ref.shape[0], uc_ref.shape[0]
    conv_ch = GROUP_W + 2 * SSD_BC
    half = (SSD_CONV - 1) // 2
    halo = 8

    def conv(u_ref, x_ref, t):
        pad_ref[0:halo, :] = jnp.zeros((halo, conv_ch), F32)
        pad_ref[halo + t:2 * halo + t, :] = jnp.zeros((halo, conv_ch), F32)
        for r, n in _post_tiles(t):
            pad_ref[halo + r:halo + r + n, :] = u_ref[r:r + n, GROUP_W:GROUP_W + conv_ch]
        for r, n in _post_tiles(t):
            acc = jnp.zeros((n, conv_ch), F32) + cb_ref[...]
            for tap in range(SSD_CONV):
                lo = halo + r + tap - half
                acc = acc + pad_ref[lo:lo + n, :] * cw_ref[tap:tap + 1, :]
            act = _silu(acc)
            grp = [act[:, GROUP_W + m * HEAD_DIM:GROUP_W + (m + 1) * HEAD_DIM] for m in range(4)]
            x_ref[r:r + n, 0:GROUP_W] = act[:, 0:GROUP_W]
            x_ref[r:r + n, GROUP_W:2 * GROUP_W] = jnp.concatenate([grp[0], grp[0], grp[1], grp[1]], axis=1)
            x_ref[r:r + n, 2 * GROUP_W:3 * GROUP_W] = jnp.concatenate([grp[2], grp[2], grp[3], grp[3]], axis=1)

    conv(uc_ref, xc_ref, t_ctx)
    conv(ul_ref, xl_ref, t_lat)
    s_ref[...] = jnp.zeros_like(s_ref)
    consts = _scan_consts()
    cum3 = (_cum_mat(False), _cum_mat(True))
    expand = (_expand_mat(DT_OFF), _expand_mat(DT_OFF + N_HEADS))

    def multi_step(is_lat, chunks):
        x_ref, sm_ref, acc_ref = (xl_ref, sml_ref, accl_ref) if is_lat else (xc_ref, smc_ref, accc_ref)
        lanes = []
        for c, d in chunks:
            r = pl.multiple_of(c * CHUNK, CHUNK)
            dt = _spread(_softplus(sm_ref[pl.ds(r, CHUNK), :] + dtb_ref[...]), expand[d])
            lanes.append(dict(d=d, r=r, want_out=is_lat or ctx_out, dt=dt,
                              k=x_ref[pl.ds(r, CHUNK), GROUP_W:2 * GROUP_W],
                              q=x_ref[pl.ds(r, CHUNK), 2 * GROUP_W:3 * GROUP_W]))
        _lane_scores(lanes, consts[0][2])
        for ln in lanes:
            ln["g_b"] = _cumsum_b(ln["dt"] * an_ref[ln["d"]], cum3[ln["d"]])
            ln["v"] = x_ref[pl.ds(ln["r"], CHUNK), 0:GROUP_W] * ln["dt"]
        for ln, o in zip(lanes, _decay_lanes(lanes, consts, s_ref)):
            if ln["want_out"]:
                acc_ref[ln["d"], pl.ds(ln["r"], CHUNK), :] = o

    _scan_lanes(t_ctx // CHUNK, t_lat // CHUNK, multi_step)

    def post(u_ref, x_ref, acc_ref, y_ref, t):
        for r, n in _post_tiles(t):
            y = acc_ref[0, r:r + n, :] + acc_ref[1, r:r + n, :] + dsk_ref[...] * x_ref[r:r + n, 0:GROUP_W]
            y_ref[r:r + n, :] = _rms(y * _silu(u_ref[r:r + n, 0:GROUP_W]), nw_ref[...])

    post(ul_ref, xl_ref, accl_ref, yl_ref, t_lat)
    if ctx_out:
        post(uc_ref, xc_ref, accc_ref, yc_ref, t_ctx)


def _ssd_call(u_lat, u_ctx, sm_lat, sm_ctx, conv_w, conv_b, dt_bias, a_log, d_skip, nw, ctx_out):
    batch, t_lat, _ = u_lat.shape
    t_ctx = u_ctx.shape[1]
    conv_ch = GROUP_W + 2 * SSD_BC
    dtb = jnp.zeros((SMALL_W,), F32).at[DT_OFF:DT_OFF + 2 * N_HEADS].set(dt_bias.reshape(-1))[None]
    a_neg = jnp.repeat(-jnp.exp(a_log.astype(F32)), HEAD_DIM, axis=1)[:, None, :]
    cw = jnp.concatenate([conv_w, jnp.zeros((8 - SSD_CONV, conv_ch), F32)], axis=0)
    dsk = jnp.repeat(d_skip, HEAD_DIM)[None]
    scratch = _state_scratch(1, t_lat, t_ctx) + [
        pltpu.VMEM((t_lat, 3 * GROUP_W), F32), pltpu.VMEM((t_ctx, 3 * GROUP_W), F32),
        pltpu.VMEM((t_lat + 16, conv_ch), F32)]
    return _mixer_call(functools.partial(_ssd_body, ctx_out=ctx_out), "ssd", batch, t_lat, t_ctx, ctx_out,
                       [u_lat, u_ctx, sm_lat, sm_ctx], [cw, conv_b[None], dtb, a_neg, dsk, nw], scratch)


def _mlstm_body(ul_ref, uc_ref, sml_ref, smc_ref, gb_ref, nw_ref, *rest, ctx_out):
    if ctx_out:
        yl_ref, yc_ref, c_ref, n_ref, m_ref, accl_ref, accc_ref = rest
    else:
        yl_ref, c_ref, n_ref, m_ref, accl_ref, accc_ref = rest
        yc_ref = None
    t_lat, t_ctx = ul_ref.shape[0], uc_ref.shape[0]
    c_ref[...] = jnp.zeros_like(c_ref)
    n_ref[...] = jnp.zeros_like(n_ref)
    m_ref[...] = jnp.zeros_like(m_ref)
    consts = _scan_consts()
    same_head = consts[0][2]
    seg = same_head.astype(F32).astype(BF16)
    cum3 = (_cum_mat(False), _cum_mat(True))
    f_lane = tuple(GATE_OFF + 2 * N_HEADS * d + N_HEADS for d in (0, 1))
    expand_fi = tuple(jnp.concatenate([_expand_mat(f_lane[d]), _expand_mat(f_lane[d] - N_HEADS)], axis=1)
                      for d in (0, 1))
    lane = _iota2((1, SMALL_W), 1)
    is_f = (lane >= GATE_OFF) & ((lane - GATE_OFF) % (2 * N_HEADS) >= N_HEADS)

    def running_max(x, rev):
        row = _iota2(x.shape, 0)
        s = 1
        while s < CHUNK:
            shifted = pltpu.roll(x, CHUNK - s if rev else s, 0)
            valid = (row < CHUNK - s) if rev else (row >= s)
            x = jnp.maximum(x, jnp.where(valid, shifted, -jnp.inf))
            s *= 2
        return x

    def multi_step(is_lat, chunks):
        u_ref, sm_ref, acc_ref = (ul_ref, sml_ref, accl_ref) if is_lat else (uc_ref, smc_ref, accc_ref)
        want_out = is_lat or ctx_out
        lanes = []
        for c, d in chunks:
            r = pl.multiple_of(c * CHUNK, CHUNK)
            gates = sm_ref[pl.ds(r, CHUNK), :] + gb_ref[...]
            both = _spread(jnp.where(is_f, _log_sigmoid(gates), gates), expand_fi[d])
            lanes.append(dict(d=d, r=r, want_out=want_out, lf_b=both[:, 0:GROUP_W], ig_b=both[:, GROUP_W:2 * GROUP_W],
                              q=u_ref[pl.ds(r, CHUNK), 0:GROUP_W],
                              k=u_ref[pl.ds(r, CHUNK), GROUP_W:2 * GROUP_W] * (HEAD_DIM ** -0.5),
                              v=u_ref[pl.ds(r, CHUNK), 2 * GROUP_W:3 * GROUP_W]))
        _lane_scores(lanes, same_head)
        for ln in lanes:
            ln["g_b"] = _cumsum_b(ln["lf_b"], cum3[ln["d"]])
        m_now, n_now = [m_ref[0], m_ref[1]], [n_ref[0], n_ref[1]]
        for ln in lanes:
            d, g_b, ig_b = ln["d"], ln["g_b"], ln["ig_b"]
            mask_b, eye_b, _ = consts[d]
            g_tot = g_b[0:1] if d == 1 else g_b[CHUNK - 1:CHUNK]
            w_end = g_tot - g_b + ig_b
            b_max = jnp.max(w_end, axis=0, keepdims=True)
            ln["ke"] = ln["k"] * jnp.exp(w_end - b_max)
            m_in, n_in = m_now[d], n_now[d]
            m_new = jnp.maximum(g_tot + m_in, b_max)
            ln["old"], ln["new"] = jnp.exp(g_tot + m_in - m_new), jnp.exp(b_max - m_new)
            m_now[d] = m_new
            n_now[d] = ln["old"] * n_in + ln["new"] * jnp.sum(ln["ke"], axis=0, keepdims=True)
            if want_out:
                m_inter = g_b + m_in
                m_tot = jnp.maximum(m_inter, g_b + running_max(ig_b - g_b, d == 1))
                g_row = jnp.sum(jnp.where(eye_b, g_b, 0.0), axis=0, keepdims=True)
                ig_row = jnp.sum(jnp.where(eye_b, ig_b, 0.0), axis=0, keepdims=True)
                d_log = jnp.where(mask_b, g_b - g_row + ig_row, -jnp.inf)
                ln["qk"] = ln["scores"] * jnp.exp(d_log - m_tot)
                ln["s_inter"] = jnp.exp(m_inter - m_tot)
                ln["floor"] = jnp.exp(-m_tot)
                ln["den_terms"] = ln["qk"] + ln["s_inter"] * (ln["q"] * n_in)
        m_ref[0], m_ref[1] = m_now
        n_ref[0], n_ref[1] = n_now
        for ln in lanes:
            ln["kv"] = jnp.where(same_head, _bdot_tn(ln["ke"], ln["v"]), 0.0)
        if want_out:
            for ln in lanes:
                ln["num"] = _bdot(ln["qk"], _blockdiag(ln["v"], same_head))
                ln["den"] = _seg_sum(ln["den_terms"], same_head)
        for ln in lanes:
            d = ln["d"]
            c_in = c_ref[d]
            if want_out:
                num = ln["num"] + ln["s_inter"] * _bdot(ln["q"], c_in)
                acc_ref[d, pl.ds(ln["r"], CHUNK), :] = num / jnp.maximum(jnp.abs(ln["den"]), ln["floor"])
            c_ref[d] = ln["old"] * c_in + ln["new"] * ln["kv"]

    _scan_lanes(t_ctx // CHUNK, t_lat // CHUNK, multi_step)

    def post(u_ref, acc_ref, y_ref, t):
        for r, n in _post_tiles(t):
            o = _head_rms(acc_ref[0, r:r + n, :] + acc_ref[1, r:r + n, :], same_head, False)
            y_ref[r:r + n, :] = jax.nn.sigmoid(u_ref[r:r + n, 3 * GROUP_W:4 * GROUP_W]) * (o * nw_ref[...])

    post(ul_ref, accl_ref, yl_ref, t_lat)
    if ctx_out:
        post(uc_ref, accc_ref, yc_ref, t_ctx)


def _mlstm_call(u_lat, u_ctx, sm_lat, sm_ctx, gate_b, nw, ctx_out):
    batch, t_lat, _ = u_lat.shape
    t_ctx = u_ctx.shape[1]
    gb = jnp.zeros((SMALL_W,), F32).at[GATE_OFF:GATE_OFF + 4 * N_HEADS].set(gate_b.reshape(-1))[None]
    scratch = ([pltpu.VMEM((2, GROUP_W, GROUP_W), F32), pltpu.VMEM((2, 1, GROUP_W), F32),
                pltpu.VMEM((2, 1, GROUP_W), F32)]
               + [pltpu.VMEM((2, t_lat, GROUP_W), F32), pltpu.VMEM((2, t_ctx, GROUP_W), F32)])
    return _mixer_call(functools.partial(_mlstm_body, ctx_out=ctx_out), "mlstm", batch, t_lat, t_ctx, ctx_out,
                       [u_lat, u_ctx, sm_lat, sm_ctx], [gb, nw], scratch)


def _gla_body(ul_ref, uc_ref, sml_ref, smc_ref, gw_ref, gb_ref, nw_ref, *rest, ctx_out):
    if ctx_out:
        yl_ref, yc_ref, st_ref, accl_ref, accc_ref, gl_ref, gc_ref = rest
    else:
        yl_ref, st_ref, accl_ref, accc_ref, gl_ref, gc_ref = rest
        yc_ref = None
    t_lat, t_ctx = ul_ref.shape[0], uc_ref.shape[0]
    st_ref[...] = jnp.zeros_like(st_ref)
    n_sub = CHUNK // GLA_SUB
    i = lax.broadcasted_iota(I32, (CHUNK, CHUNK), 0)
    j = lax.broadcasted_iota(I32, (CHUNK, CHUNK), 1)
    same = (i // GLA_SUB) == (j // GLA_SUB)
    lcum = ((same & (j <= i)).astype(F32).astype(BF16), (same & (j >= i)).astype(F32).astype(BF16))
    expand = ((lax.broadcasted_iota(I32, (GLA_QK, GROUP_W), 0) // GLA_DK)
              == (lax.broadcasted_iota(I32, (GLA_QK, GROUP_W), 1) // HEAD_DIM)).astype(F32).astype(BF16)
    blockdiag = ((lax.broadcasted_iota(I32, (GROUP_W, GLA_QK), 0) // HEAD_DIM)
                 == (lax.broadcasted_iota(I32, (GROUP_W, GLA_QK), 1) // GLA_DK)).astype(F32)
    sub_row = lax.broadcasted_iota(I32, (CHUNK, 1), 0) % GLA_SUB

    def roll_sub(x, delta):
        return jnp.concatenate([pltpu.roll(x[a * GLA_SUB:(a + 1) * GLA_SUB], delta, 0) for a in range(n_sub)],
                               axis=0)

    def qkv(u_ref, r):
        return (u_ref[pl.ds(r, CHUNK), 0:GLA_QK] * (GLA_DK ** -0.5), u_ref[pl.ds(r, CHUNK), GLA_QK:2 * GLA_QK],
                u_ref[pl.ds(r, CHUNK), 2 * GLA_QK:2 * GLA_QK + GROUP_W])

    def intra(u_ref, sm_ref, g_ref, acc_ref, want_out, c):
        r = pl.multiple_of(c * CHUNK, CHUNK)
        lr = sm_ref[pl.ds(r, CHUNK), LR_OFF:LR_OFF + GLA_RANK]
        g = []
        for d in (0, 1):
            la = _log_sigmoid(_bdot(lr, gw_ref[d]) + gb_ref[d]) / GLA_TAU
            g.append(sum(jnp.dot(lcum[d], piece, preferred_element_type=F32) for piece in _split3(la)))
            g_ref[d, pl.ds(r, CHUNK), :] = g[d]
        if not want_out:
            return
        q, k, v = qkv(u_ref, r)
        prods, vals = [2.0 * q * k], [v]
        for delta in range(1, GLA_SUB):
            fwd_pair = sub_row >= delta
            e = jnp.exp(jnp.where(fwd_pair, g[0] - roll_sub(g[0], delta), g[1] - roll_sub(g[1], delta)))
            prods.append(q * roll_sub(k, delta) * e)
            vals.append(roll_sub(v, delta))
        w = jnp.dot(jnp.concatenate(prods, axis=0).astype(BF16), expand, preferred_element_type=F32)
        acc_ref[pl.ds(r, CHUNK), :] = sum(w[n * CHUNK:(n + 1) * CHUNK] * vals[n] for n in range(GLA_SUB))

    def intra_ctx(c, carry):
        intra(uc_ref, smc_ref, gc_ref, accc_ref, ctx_out, c)
        return carry

    def intra_lat(c, carry):
        intra(ul_ref, sml_ref, gl_ref, accl_ref, True, c)
        return carry

    lax.fori_loop(0, t_ctx // CHUNK, intra_ctx, 0)
    lax.fori_loop(0, t_lat // CHUNK, intra_lat, 0, unroll=4)

    def multi_step(is_lat, chunks):
        u_ref, g_ref, acc_ref = (ul_ref, gl_ref, accl_ref) if is_lat else (uc_ref, gc_ref, accc_ref)
        want_out = is_lat or ctx_out
        lanes = []
        for c, d in chunks:
            r = pl.multiple_of(c * CHUNK, CHUNK)
            q, k, v = qkv(u_ref, r)
            g = g_ref[d, pl.ds(r, CHUNK), :]
            subs = []
            for a in (range(n_sub) if d == 0 else range(n_sub - 1, -1, -1)):
                rows = slice(a * GLA_SUB, (a + 1) * GLA_SUB)
                g_a = g[rows]
                g_end = g_a[GLA_SUB - 1:GLA_SUB] if d == 0 else g_a[0:1]
                subs.append(dict(a=a, decay=jnp.exp(g_end),
                                 kv=_bdot_tn(v[rows], k[rows] * jnp.exp(g_end - g_a)) * blockdiag,
                                 qg=q[rows] * jnp.exp(g_a) if want_out else None))
            lanes.append(dict(d=d, r=r, subs=subs, out=[None] * n_sub))
        for step_lanes in (lanes[n:n + 2] for n in range(0, len(lanes), 2)):
            for n in range(n_sub):
                for ln in step_lanes:
                    sub, st = ln["subs"][n], st_ref[ln["d"]]
                    if want_out:
                        ln["out"][sub["a"]] = _bdot_nt(sub["qg"], st)
                    st_ref[ln["d"]] = st * sub["decay"] + sub["kv"]
        if want_out:
            for ln in lanes:
                acc_ref[pl.ds(ln["r"], CHUNK), :] += jnp.concatenate(ln["out"], axis=0)

    _scan_lanes(t_ctx // CHUNK, t_lat // CHUNK, multi_step)

    def post(u_ref, acc_ref, y_ref, t):
        for r, n in _post_tiles(t):
            o = _head_rms(acc_ref[r:r + n, :], _same_head(), False)
            y_ref[r:r + n, :] = o * nw_ref[...] * _silu(u_ref[r:r + n, 2 * GLA_QK + GROUP_W:2 * GLA_QK + 2 * GROUP_W])

    post(ul_ref, accl_ref, yl_ref, t_lat)
    if ctx_out:
        post(uc_ref, accc_ref, yc_ref, t_ctx)


def _gla_call(u_lat, u_ctx, sm_lat, sm_ctx, gate_w, gate_b, nw, ctx_out):
    batch, t_lat, _ = u_lat.shape
    t_ctx = u_ctx.shape[1]
    scratch = [pltpu.VMEM((2, GROUP_W, GLA_QK), F32),
               pltpu.VMEM((t_lat, GROUP_W), F32), pltpu.VMEM((t_ctx, GROUP_W), F32),
               pltpu.VMEM((2, t_lat, GLA_QK), F32), pltpu.VMEM((2, t_ctx, GLA_QK), F32)]
    return _mixer_call(functools.partial(_gla_body, ctx_out=ctx_out), "gla", batch, t_lat, t_ctx, ctx_out,
                       [u_lat, u_ctx, sm_lat, sm_ctx], [gate_w, gate_b[:, None, :], nw], scratch)


def _out_body(yr_ref, yg_ref, ys_ref, ym_ref, h_ref, mod_ref, wo_ref, n2_ref, rw_ref, h1_ref, hn_ref, lt_ref):
    mod = mod_ref[...]
    proj = sum(_bdot(y_ref[...], wo_ref[n * GROUP_W:(n + 1) * GROUP_W, :])
               for n, y_ref in enumerate((yr_ref, yg_ref, ys_ref, ym_ref)))
    h1 = h_ref[...] + mod[2:3, :] * proj
    h1_ref[...] = h1
    xn = _rms(h1, n2_ref[...]) * (1.0 + mod[4:5, :]) + mod[3:4, :]
    hn_ref[...] = xn.astype(BF16)
    x_hi, x_mid, _ = _split3(xn)
    parts = (jnp.dot(x_hi, rw_ref[...], preferred_element_type=F32)
             + jnp.dot(x_mid, rw_ref[...], preferred_element_type=F32)).T
    lt_ref[...] = parts[0:N_EXPERTS] + parts[N_EXPERTS:2 * N_EXPERTS] + parts[2 * N_EXPERTS:3 * N_EXPERTS]


def _out_call(ys, h, mod_l, wo, n2w, rw3, is_ctx):
    batch, t, d = h.shape
    tm = min(512, t)
    y_spec = pl.BlockSpec((None, tm, GROUP_W), lambda b, i: (b, i, 0))
    return pl.pallas_call(
        _out_body,
        grid=(batch, t // tm),
        in_specs=[y_spec, y_spec, y_spec, y_spec,
                  pl.BlockSpec((None, tm, d), lambda b, i: (b, i, 0)),
                  _mod_spec(is_ctx, batch),
                  pl.BlockSpec((d, d), lambda b, i: (0, 0)),
                  pl.BlockSpec((1, d), lambda b, i: (0, 0)),
                  pl.BlockSpec(rw3.shape, lambda b, i: (0, 0))],
        out_specs=[pl.BlockSpec((None, tm, d), lambda b, i: (b, i, 0)),
                   pl.BlockSpec((None, tm, d), lambda b, i: (b, i, 0)),
                   pl.BlockSpec((None, N_EXPERTS, tm), lambda b, i: (b, 0, i))],
        out_shape=[jax.ShapeDtypeStruct((batch, t, d), F32),
                   jax.ShapeDtypeStruct((batch, t, d), BF16),
                   jax.ShapeDtypeStruct((batch, N_EXPERTS, t), F32)],
        compiler_params=_cparams("parallel", "parallel"),
        name="out_proj",
    )(*ys, h, mod_l, wo, n2w, rw3)


def _excl_prefix(flags, t):
    blk = min(256, t)
    below = (lax.broadcasted_iota(I32, (blk, blk), 0) < lax.broadcasted_iota(I32, (blk, blk), 1))
    below = below.astype(F32).astype(BF16)
    outs = []
    carry = jnp.zeros((flags.shape[0], 1), F32)
    for n in range(t // blk):
        f = flags[:, n * blk:(n + 1) * blk]
        outs.append(jnp.dot(f.astype(BF16), below, preferred_element_type=F32) + carry)
        carry = carry + jnp.sum(f, axis=1, keepdims=True)
    return jnp.concatenate(outs, axis=1)


def _route_body(lt_ref, hn_ref, xg_ref, affc_ref, post_ref, pos_s, aff_s, *, cap):
    t = hn_ref.shape[0]
    logits = lt_ref[...]
    ex = jnp.exp(logits - jnp.max(logits, axis=0, keepdims=True))
    aff = ex / jnp.sum(ex, axis=0, keepdims=True)
    def count_at_least(cand):
        return jnp.sum((aff >= pltpu.bitcast(cand, F32)).astype(F32), axis=1, keepdims=True)

    def fix_two_bits(n, thr):
        lo = 28 - 2 * n
        best = thr
        for digit in (1, 2, 3):
            cand = thr | (jnp.int32(digit) << lo)
            best = jnp.where(count_at_least(cand) >= float(cap), cand, best)
        return best

    thr = lax.fori_loop(0, 15, fix_two_bits, jnp.zeros((N_EXPERTS, 1), I32))
    above = (aff >= pltpu.bitcast(thr + 1, F32)).astype(F32)
    tied = (aff >= pltpu.bitcast(thr, F32)).astype(F32) - above
    need = float(cap) - jnp.sum(above, axis=1, keepdims=True)
    sel = above + tied * (_excl_prefix(tied, t) < need).astype(F32)
    pos = jnp.where(sel > 0.0, _excl_prefix(sel, t), -1.0)
    pos_s[...] = pos
    aff_s[...] = aff
    post_ref[...] = jnp.concatenate([pos, jnp.full((SMALL_W - N_EXPERTS, t), -1.0, F32)], axis=0).T
    slots = lax.broadcasted_iota(I32, (cap, t), 0).astype(F32)

    def gather(e, carry):
        hit = slots == pos_s[pl.ds(e, 1), :]
        xg_ref[e] = jnp.dot(hit.astype(BF16), hn_ref[...], preferred_element_type=F32).astype(BF16)
        a = jnp.sum(jnp.where(hit, aff_s[pl.ds(e, 1), :], 0.0), axis=1, keepdims=True)
        affc_ref[e] = jnp.broadcast_to(a, (cap, SMALL_W))
        return carry

    lax.fori_loop(0, N_EXPERTS, gather, 0, unroll=2)


def _route_call(lt, hn, cap):
    batch, t, d = hn.shape
    return pl.pallas_call(
        functools.partial(_route_body, cap=cap),
        grid=(batch,),
        in_specs=[pl.BlockSpec((None, N_EXPERTS, t), lambda b: (b, 0, 0)),
                  pl.BlockSpec((None, t, d), lambda b: (b, 0, 0))],
        out_specs=[pl.BlockSpec((N_EXPERTS, cap, d), lambda b: (0, b, 0)),
                   pl.BlockSpec((N_EXPERTS, cap, SMALL_W), lambda b: (0, b, 0)),
                   pl.BlockSpec((None, t, SMALL_W), lambda b: (b, 0, 0))],
        out_shape=[jax.ShapeDtypeStruct((N_EXPERTS, batch * cap, d), BF16),
                   jax.ShapeDtypeStruct((N_EXPERTS, batch * cap, SMALL_W), F32),
                   jax.ShapeDtypeStruct((batch, t, SMALL_W), F32)],
        scratch_shapes=[pltpu.VMEM((N_EXPERTS, t), F32), pltpu.VMEM((N_EXPERTS, t), F32)],
        compiler_params=_cparams("parallel"),
        name="route_gather",
    )(lt, hn)


FFN_TF = 512
FFN_TM = 512


def _ffn_body(*refs, n_sets):
    ins = refs[:2 * n_sets]
    wg_ref, wu_ref, wd_ref = refs[2 * n_sets:2 * n_sets + 3]
    outs = refs[2 * n_sets + 3:3 * n_sets + 3]
    accs = refs[3 * n_sets + 3:4 * n_sets + 3]
    f = pl.program_id(1)
    dot = functools.partial(jnp.dot, preferred_element_type=F32)
    for s in range(n_sets):
        x_ref, a_ref, y_ref, acc_ref = ins[2 * s], ins[2 * s + 1], outs[s], accs[s]
        rows = x_ref.shape[0]

        @pl.when(f == 0)
        def _():
            acc_ref[...] = jnp.zeros_like(acc_ref)

        for r in range(0, rows, FFN_TM):
            n = min(FFN_TM, rows - r)
            x = x_ref[r:r + n, :]
            hid = _silu(dot(x, wg_ref[...].astype(BF16))) * dot(x, wu_ref[...].astype(BF16))
            acc_ref[r:r + n, :] += dot(hid.astype(BF16), wd_ref[...].astype(BF16))

        @pl.when(f == pl.num_programs(1) - 1)
        def _():
            for r in range(0, rows, FFN_TM):
                n = min(FFN_TM, rows - r)
                y_ref[r:r + n, :] = (acc_ref[r:r + n, :] * a_ref[r:r + n, 0:1]).astype(BF16)


def _ffn_call(sets, wg, wu, wd, layer):
    _, n_exp, d, ff = wg.shape
    in_specs, args, out_specs, out_shape, scratch = [], [], [], [], []
    for xg, affc in sets:
        m = xg.shape[1]
        in_specs += [pl.BlockSpec((None, m, d), lambda e, f: (e, 0, 0)),
                     pl.BlockSpec((None, m, SMALL_W), lambda e, f: (e, 0, 0))]
        args += [xg, affc]
        out_specs.append(pl.BlockSpec((None, m, d), lambda e, f: (e, 0, 0)))
        out_shape.append(jax.ShapeDtypeStruct((n_exp, m, d), BF16))
        scratch.append(pltpu.VMEM((m, d), F32))
    in_specs += [pl.BlockSpec((None, None, d, FFN_TF), lambda e, f: (layer, e, 0, f)),
                 pl.BlockSpec((None, None, d, FFN_TF), lambda e, f: (layer, e, 0, f)),
                 pl.BlockSpec((None, None, FFN_TF, d), lambda e, f: (layer, e, f, 0))]
    return pl.pallas_call(
        functools.partial(_ffn_body, n_sets=len(sets)),
        grid=(n_exp, ff // FFN_TF),
        in_specs=in_specs, out_specs=out_specs, out_shape=out_shape, scratch_shapes=scratch,
        compiler_params=_cparams("parallel", "arbitrary"),
        name="expert_ffn",
    )(*args, wg, wu, wd)


def _scatter_body(post_ref, y_ref, h1_ref, mod_ref, fw_ref, o_ref, *, final):
    n_exp, cap, d = y_ref.shape
    pos = post_ref[...]
    slots = lax.broadcasted_iota(I32, (pos.shape[0], cap), 1).astype(F32)
    hit = jnp.concatenate([(pos[:, e:e+1] == slots).astype(BF16) for e in range(n_exp)], axis=1)
    upd = jnp.dot(hit, y_ref[...].reshape(n_exp * cap, d), preferred_element_type=F32)
    h2 = h1_ref[...] + mod_ref[5:6, :] * upd
    o_ref[...] = _rms(h2, fw_ref[...]) if final else h2


def _scatter_call(post, y, h1, mod_l, is_ctx, final_w):
    batch, t, d = h1.shape
    n_exp = y.shape[0]
    cap = y.shape[1] // batch
    tm = min(256, t)
    final = final_w is not None
    fw = final_w if final else jnp.ones((1, d), F32)
    return pl.pallas_call(
        functools.partial(_scatter_body, final=final),
        grid=(batch, t // tm),
        in_specs=[pl.BlockSpec((None, tm, SMALL_W), lambda b, i: (b, i, 0)),
                  pl.BlockSpec((n_exp, cap, d), lambda b, i: (0, b, 0)),
                  pl.BlockSpec((None, tm, d), lambda b, i: (b, i, 0)),
                  _mod_spec(is_ctx, batch),
                  pl.BlockSpec((1, d), lambda b, i: (0, 0))],
        out_specs=pl.BlockSpec((None, tm, d), lambda b, i: (b, i, 0)),
        out_shape=jax.ShapeDtypeStruct((batch, t, d), F32),
        compiler_params=_cparams("parallel", "parallel"),
        name="scatter_residual",
    )(post, y, h1, mod_l, fw)


def kernel(x, c, ctx, c_ctx, w_mod, b_mod, norm1_w, norm2_w, w_in, w_out, ret_decay_logit, ret_norm_w,
           gla_gate_w, gla_gate_b, gla_norm_w, ssd_conv_w, ssd_conv_b, ssd_dt_bias, ssd_a_log, ssd_d,
           ssd_norm_w, mlstm_gate_b, mlstm_norm_w, router_w, expert_w_gate, expert_w_up, expert_w_down,
           final_norm_w):
    batch, t_lat, d = x.shape
    t_ctx = ctx.shape[1]
    depth = w_mod.shape[0]
    assert batch + 1 <= ROW_PAD and d == D_MODEL
    assert t_lat % 256 == 0 and t_ctx % 128 == 0 and t_ctx % (CHUNK * SCAN_STEPS) == 0
    cs = jnp.concatenate([c, c_ctx[None], jnp.zeros((ROW_PAD - batch - 1, d), F32)], axis=0)
    mod = _mod_call(cs, w_mod, b_mod).reshape(depth, ROW_PAD, 6, d)
    cos, sin = _rope_tables(t_lat)
    cap_lat = EC_CAPACITY * t_lat // N_EXPERTS
    cap_ctx = EC_CAPACITY * t_ctx // N_EXPERTS
    h_lat, h_ctx = x, ctx
    for l in range(depth):
        ctx_out = l < depth - 1
        mod_l = mod[l]
        w_big = _arrange_w_in(w_in[l])
        ul = _in_call(h_lat, mod_l, norm1_w[l][None], w_big, False)
        uc = _in_call(h_ctx, mod_l, norm1_w[l][None], w_big, True)
        ys = [_ret_call(ul[0], uc[0], jax.nn.log_sigmoid(ret_decay_logit[l].astype(F32)), cos, sin,
                        ret_norm_w[l][None], ctx_out),
              _gla_call(ul[1], uc[1], ul[4], uc[4], gla_gate_w[l], gla_gate_b[l], gla_norm_w[l][None], ctx_out),
              _ssd_call(ul[2], uc[2], ul[4], uc[4], ssd_conv_w[l], ssd_conv_b[l], ssd_dt_bias[l], ssd_a_log[l],
                        ssd_d[l], ssd_norm_w[l][None], ctx_out),
              _mlstm_call(ul[3], uc[3], ul[4], uc[4], mlstm_gate_b[l], mlstm_norm_w[l][None], ctx_out)]
        wo = w_out[l].astype(BF16)
        rw3 = jnp.pad(jnp.concatenate(_split3(router_w[l]), axis=1), ((0, 0), (0, SMALL_W - 3 * N_EXPERTS)))
        n2w = norm2_w[l][None]
        h1_lat, hn_lat, lt_lat = _out_call([y[0] for y in ys], h_lat, mod_l, wo, n2w, rw3, False)
        xg_lat, affc_lat, post_lat = _route_call(lt_lat, hn_lat, cap_lat)
        sets = [(xg_lat, affc_lat)]
        if ctx_out:
            h1_ctx, hn_ctx, lt_ctx = _out_call([y[1] for y in ys], h_ctx, mod_l, wo, n2w, rw3, True)
            xg_ctx, affc_ctx, post_ctx = _route_call(lt_ctx, hn_ctx, cap_ctx)
            sets.append((xg_ctx, affc_ctx))
        y_exp = _ffn_call(sets, expert_w_gate, expert_w_up, expert_w_down, l)
        h_lat = _scatter_call(post_lat, y_exp[0], h1_lat, mod_l, False,
                              None if ctx_out else final_norm_w[None])
        if ctx_out:
            h_ctx = _scatter_call(post_ctx, y_exp[1], h1_ctx, mod_l, True, None)
    return h_lat
```

```python
import functools
import math

import jax
import jax.numpy as jnp
from jax import lax
from jax.experimental import pallas as pl
from jax.experimental.pallas import tpu as pltpu

F32 = jnp.float32
BF16 = jnp.bfloat16
I32 = jnp.int32

D_MODEL = 1024
GRID_W = 64
EPS = 1e-6
ROPE_BASE = 10000.0
GROUP_W = 256
HEAD_DIM = 64
N_HEADS = 4
CHUNK = 64
GLA_DK = 32
GLA_QK = 128
GLA_RANK = 16
GLA_TAU = 16.0
GLA_SUB = 16
SSD_CONV = 5
SSD_BC = 128
N_EXPERTS = 16
EC_CAPACITY = 2
EXPERT_FF = 1536
SMALL_W = 128
LR_OFF, DT_OFF, GATE_OFF = 0, 16, 24
ROW_PAD = 16
VMEM_LIMIT = 56 * 1024 * 1024


def _cparams(*sem):
    return pltpu.CompilerParams(dimension_semantics=sem, vmem_limit_bytes=VMEM_LIMIT)


def _bdot(a, b):
    return jnp.dot(a.astype(BF16), b.astype(BF16), preferred_element_type=F32)


def _bdot_nt(a, b):
    return lax.dot_general(a.astype(BF16), b.astype(BF16), (((1,), (1,)), ((), ())),
                           preferred_element_type=F32)


def _bdot_tn(a, b):
    return lax.dot_general(a.astype(BF16), b.astype(BF16), (((0,), (0,)), ((), ())),
                           preferred_element_type=F32)


def _silu(x):
    return x * jax.nn.sigmoid(x)


def _softplus(x):
    return jnp.maximum(x, 0.0) + jnp.log1p(jnp.exp(-jnp.abs(x)))


def _log_sigmoid(x):
    return -_softplus(-x)


def _rms(x, w):
    return x * lax.rsqrt(jnp.mean(x * x, axis=-1, keepdims=True) + EPS) * w


def _mod_body(cs_ref, w_ref, b_ref, o_ref):
    o_ref[...] = _bdot(_silu(cs_ref[...]), w_ref[...]) + b_ref[...]


def _mod_call(cs, w_mod, b_mod):
    depth, d, n = w_mod.shape
    tn = 1536
    return pl.pallas_call(
        _mod_body,
        grid=(depth, n // tn),
        in_specs=[pl.BlockSpec((ROW_PAD, d), lambda l, j: (0, 0)),
                  pl.BlockSpec((None, d, tn), lambda l, j: (l, 0, j)),
                  pl.BlockSpec((None, 1, tn), lambda l, j: (l, 0, j))],
        out_specs=pl.BlockSpec((None, ROW_PAD, tn), lambda l, j: (l, 0, j)),
        out_shape=jax.ShapeDtypeStruct((depth, ROW_PAD, n), F32),
        compiler_params=_cparams("parallel", "parallel"),
        name="adaln",
    )(cs, w_mod, b_mod.reshape(depth, 1, n))


def _mod_spec(is_ctx, batch):
    if is_ctx:
        return pl.BlockSpec((None, 6, D_MODEL), lambda b, i: (batch, 0, 0))
    return pl.BlockSpec((None, 6, D_MODEL), lambda b, i: (b, 0, 0))


IN_WIDTHS = (1024, 768, 768, 1024, SMALL_W)


def _in_body(h_ref, mod_ref, nw_ref, w_ref, ret_ref, gla_ref, ssd_ref, mls_ref, sm_ref):
    mod = mod_ref[...]
    xn = _rms(h_ref[...], nw_ref[...]) * (1.0 + mod[1:2, :]) + mod[0:1, :]
    u = jnp.dot(xn.astype(BF16), w_ref[...], preferred_element_type=F32)
    off = 0
    for ref, w in zip((ret_ref, gla_ref, ssd_ref, mls_ref, sm_ref), IN_WIDTHS):
        ref[...] = u[:, off:off + w]
        off += w


def _in_call(h, mod_l, nw, w_big, is_ctx):
    batch, t, d = h.shape
    tm = min(512, t)
    n = w_big.shape[1]
    return pl.pallas_call(
        _in_body,
        grid=(batch, t // tm),
        in_specs=[pl.BlockSpec((None, tm, d), lambda b, i: (b, i, 0)),
                  _mod_spec(is_ctx, batch),
                  pl.BlockSpec((1, d), lambda b, i: (0, 0)),
                  pl.BlockSpec((d, n), lambda b, i: (0, 0))],
        out_specs=[pl.BlockSpec((None, tm, w), lambda b, i: (b, i, 0)) for w in IN_WIDTHS],
        out_shape=[jax.ShapeDtypeStruct((batch, t, w), F32) for w in IN_WIDTHS],
        compiler_params=_cparams("parallel", "parallel"),
        name="in_proj",
    )(h, mod_l, nw, w_big)


def _arrange_w_in(w):
    ret_end = 4 * GROUP_W
    gla_main = ret_end + 2 * GLA_QK + 2 * GROUP_W
    gla_end = gla_main + GLA_RANK
    ssd_main = gla_end + GROUP_W + GROUP_W + 2 * SSD_BC
    ssd_end = ssd_main + 2 * N_HEADS
    mls_main = ssd_end + 4 * GROUP_W
    mls_end = mls_main + 4 * N_HEADS
    narrow = (gla_end - gla_main) + (ssd_end - ssd_main) + (mls_end - mls_main)
    pad = jnp.zeros((w.shape[0], SMALL_W - narrow), w.dtype)
    return jnp.concatenate([w[:, :ret_end], w[:, ret_end:gla_main], w[:, gla_end:ssd_main],
                            w[:, ssd_end:mls_main], w[:, gla_main:gla_end], w[:, ssd_main:ssd_end],
                            w[:, mls_main:mls_end], pad], axis=1).astype(BF16)


def _split3(x):
    hi = x.astype(BF16)
    r1 = x - hi.astype(F32)
    mid = r1.astype(BF16)
    return hi, mid, (r1 - mid.astype(F32)).astype(BF16)


def _iota2(shape, axis):
    return lax.broadcasted_iota(I32, shape, axis)


def _block_consts(rev):
    i = _iota2((CHUNK, GROUP_W), 0)
    j = _iota2((CHUNK, GROUP_W), 1) % HEAD_DIM
    return ((j >= i) if rev else (j <= i)), (j == i)


def _same_head():
    return (_iota2((GROUP_W, GROUP_W), 0) // HEAD_DIM) == (_iota2((GROUP_W, GROUP_W), 1) // HEAD_DIM)


def _cum_mat(rev):
    i = _iota2((CHUNK, 3 * CHUNK), 0)
    t = _iota2((CHUNK, 3 * CHUNK), 1) % CHUNK
    return ((t >= i) if rev else (t <= i)).astype(F32).astype(BF16)


def _expand_mat(first_lane, width=GROUP_W):
    return (_iota2((SMALL_W, width), 0) == first_lane + _iota2((SMALL_W, width), 1) // HEAD_DIM
            ).astype(F32).astype(BF16)


def _spread(x, expand):
    n = x.shape[0]
    y = jnp.dot(jnp.concatenate(_split3(x), axis=0), expand, preferred_element_type=F32)
    return (y[0:n] + y[n:2 * n]) + y[2 * n:3 * n]


def _cumsum_b(x_b, cum3):
    return jnp.dot(cum3, jnp.concatenate(_split3(x_b), axis=0), preferred_element_type=F32)


def _tile_heads(x):
    return jnp.concatenate([x] * N_HEADS, axis=0)


def _blockdiag(x, same_head):
    xb = _tile_heads(x.astype(BF16))
    return jnp.where(same_head, xb, jnp.zeros_like(xb))


def _lane_scores(lanes, same_head):
    for ln in lanes:
        if ln["want_out"]:
            ln["scores"] = _bdot_nt(ln["q"], _blockdiag(ln["k"], same_head))


def _decay_lanes(lanes, consts, s_ref):
    same_head = consts[0][2]
    for ln in lanes:
        mask_b, eye_b, _ = consts[ln["d"]]
        g_b = ln["g_b"]
        ln["g_tot"] = g_b[0:1] if ln["d"] == 1 else g_b[CHUNK - 1:CHUNK]
        if ln["want_out"]:
            g_row = jnp.sum(jnp.where(eye_b, g_b, 0.0), axis=0, keepdims=True)
            ln["p"] = ln["scores"] * jnp.exp(jnp.where(mask_b, g_b - g_row, -jnp.inf))
            ln["qg"] = ln["q"] * jnp.exp(g_b)
        ln["khat"] = ln["k"] * jnp.exp(ln["g_tot"] - g_b)
    for ln in lanes:
        ln["kv"] = jnp.where(same_head, _bdot_tn(ln["khat"], ln["v"]), 0.0)
    for ln in lanes:
        if ln["want_out"]:
            ln["pv"] = _bdot(ln["p"], _blockdiag(ln["v"], same_head))
    outs = []
    for ln in lanes:
        s_in = s_ref[ln["d"]]
        outs.append(ln["pv"] + _bdot(ln["qg"], s_in) if ln["want_out"] else None)
        s_ref[ln["d"]] = jnp.exp(ln["g_tot"]) * s_in + ln["kv"]
    return outs


def _seg_sum(x, same_head):
    seg = same_head.astype(F32).astype(BF16)
    hi = x.astype(BF16)
    lo = (x - hi.astype(F32)).astype(BF16)
    return jnp.dot(hi, seg, preferred_element_type=F32) + jnp.dot(lo, seg, preferred_element_type=F32)


def _head_rms(x, same_head, center):
    if center:
        x = x - _seg_sum(x, same_head) * (1.0 / HEAD_DIM)
    return x * lax.rsqrt(_seg_sum(x * x, same_head) * (1.0 / HEAD_DIM) + EPS)


SCAN_STEPS = 8


def _scan_lanes(n_ctx, n_lat, multi_step):
    def run(is_lat, n):
        steps = min(SCAN_STEPS, n)

        def it(s, carry):
            lanes = []
            for j in range(steps):
                lanes += [(steps * s + j, 0), (n - 1 - steps * s - j, 1)]
            multi_step(is_lat, lanes)
            return carry

        lax.fori_loop(0, n // steps, it, 0)

    run(False, n_ctx)
    run(True, n_lat)


def _post_tiles(t):
    tile = min(256, t)
    return [(r, tile) for r in range(0, t, tile)]


def _mixer_call(body, name, batch, t_lat, t_ctx, ctx_out, per_sample, shared, scratch):
    in_specs = [pl.BlockSpec((None,) + a.shape[1:], lambda b, nd=a.ndim: (b,) + (0,) * (nd - 1))
                for a in per_sample]
    in_specs += [pl.BlockSpec(a.shape, lambda b, nd=a.ndim: (0,) * nd) for a in shared]
    args = list(per_sample) + list(shared)
    out_shape = [jax.ShapeDtypeStruct((batch, t_lat, GROUP_W), F32)]
    out_specs = [pl.BlockSpec((None, t_lat, GROUP_W), lambda b: (b, 0, 0))]
    if ctx_out:
        out_shape.append(jax.ShapeDtypeStruct((batch, t_ctx, GROUP_W), F32))
        out_specs.append(pl.BlockSpec((None, t_ctx, GROUP_W), lambda b: (b, 0, 0)))
    return pl.pallas_call(
        body, grid=(batch,), in_specs=in_specs, out_specs=out_specs, out_shape=out_shape,
        scratch_shapes=scratch, compiler_params=_cparams("parallel"), name=name,
    )(*args)


def _rope(x, cos, sin_signed, first_half):
    return x * cos + jnp.where(first_half, pltpu.roll(x, x.shape[1] - 16, 1), pltpu.roll(x, 16, 1)) * sin_signed


def _scan_consts():
    same_head = _same_head()
    return tuple(_block_consts(rev) + (same_head,) for rev in (False, True))


def _ret_body(ul_ref, uc_ref, lg_ref, cos_ref, sin_ref, nw_ref, *rest, ctx_out):
    if ctx_out:
        yl_ref, yc_ref, s_ref, accl_ref, accc_ref = rest
    else:
        yl_ref, s_ref, accl_ref, accc_ref = rest
        yc_ref = None
    t_lat, t_ctx = ul_ref.shape[0], uc_ref.shape[0]
    s_ref[...] = jnp.zeros_like(s_ref)
    first_half = (_iota2((1, GROUP_W), 1) % 32) < 16
    consts = _scan_consts()
    row = _iota2((CHUNK, GROUP_W), 0).astype(F32)
    n_decays = (row + 1.0, CHUNK - row)

    def multi_step(is_lat, chunks):
        u_ref, acc_ref = (ul_ref, accl_ref) if is_lat else (uc_ref, accc_ref)
        lanes = []
        for c, d in chunks:
            r = pl.multiple_of(c * CHUNK, CHUNK)
            q = u_ref[pl.ds(r, CHUNK), 0:GROUP_W] * (HEAD_DIM ** -0.5)
            k = u_ref[pl.ds(r, CHUNK), GROUP_W:2 * GROUP_W]
            if is_lat:
                cos, sin = cos_ref[pl.ds(r, CHUNK), :], sin_ref[pl.ds(r, CHUNK), :]
                q, k = _rope(q, cos, sin, first_half), _rope(k, cos, sin, first_half)
            lanes.append(dict(d=d, r=r, want_out=is_lat or ctx_out, q=q, k=k,
                              v=u_ref[pl.ds(r, CHUNK), 2 * GROUP_W:3 * GROUP_W], g_b=n_decays[d] * lg_ref[d]))
        _lane_scores(lanes, consts[0][2])
        for ln, o in zip(lanes, _decay_lanes(lanes, consts, s_ref)):
            if ln["want_out"]:
                acc_ref[ln["d"], pl.ds(ln["r"], CHUNK), :] = o

    _scan_lanes(t_ctx // CHUNK, t_lat // CHUNK, multi_step)

    def post(u_ref, acc_ref, y_ref, t):
        for r, n in _post_tiles(t):
            o = _head_rms(acc_ref[0, r:r + n, :] + acc_ref[1, r:r + n, :], consts[0][2], True)
            y_ref[r:r + n, :] = o * nw_ref[...] * _silu(u_ref[r:r + n, 3 * GROUP_W:4 * GROUP_W])

    post(ul_ref, accl_ref, yl_ref, t_lat)
    if ctx_out:
        post(uc_ref, accc_ref, yc_ref, t_ctx)


def _state_scratch(n, t_lat, t_ctx):
    return ([pltpu.VMEM((2, GROUP_W, GROUP_W), F32)] * n
            + [pltpu.VMEM((2, t_lat, GROUP_W), F32), pltpu.VMEM((2, t_ctx, GROUP_W), F32)])


def _ret_call(u_lat, u_ctx, log_gamma, cos, sin, nw, ctx_out):
    batch, t_lat, _ = u_lat.shape
    t_ctx = u_ctx.shape[1]
    lg = jnp.repeat(log_gamma, HEAD_DIM, axis=1)[:, None, :]
    return _mixer_call(functools.partial(_ret_body, ctx_out=ctx_out), "retention", batch, t_lat, t_ctx, ctx_out,
                       [u_lat, u_ctx], [lg, cos, sin, nw], _state_scratch(1, t_lat, t_ctx))


def _rope_tables(t_lat):
    nf = 16
    inv = ROPE_BASE ** (-jnp.arange(nf, dtype=F32) / nf)
    rows = jnp.repeat(jnp.arange(t_lat // GRID_W), GRID_W).astype(F32)
    cols = jnp.tile(jnp.arange(GRID_W), t_lat // GRID_W).astype(F32)
    ang_r = rows[:, None] * inv[None, :]
    ang_c = cols[:, None] * inv[None, :]
    cos_h = jnp.concatenate([jnp.cos(ang_r)] * 2 + [jnp.cos(ang_c)] * 2, axis=1)
    sin_h = jnp.concatenate([-jnp.sin(ang_r), jnp.sin(ang_r), -jnp.sin(ang_c), jnp.sin(ang_c)], axis=1)
    return jnp.tile(cos_h, (1, N_HEADS)), jnp.tile(sin_h, (1, N_HEADS))


def _ssd_body(ul_ref, uc_ref, sml_ref, smc_ref, cw_ref, cb_ref, dtb_ref, an_ref, dsk_ref, nw_ref, *rest,
              ctx_out):
    if ctx_out:
        yl_ref, yc_ref, s_ref, accl_ref, accc_ref, xl_ref, xc_ref, pad_ref = rest
    else:
        yl_ref, s_ref, accl_ref, accc_ref, xl_ref, xc_ref, pad_ref = rest
        yc_ref = None
    t_lat, t_ctx = ul_ref.shape[0], uc_ref.shape[0]
    conv_ch = GROUP_W + 2 * SSD_BC
    half = (SSD_CONV - 1) // 2
    halo = 8

    def conv(u_ref, x_ref, t):
        pad_ref[0:halo, :] = jnp.zeros((halo, conv_ch), F32)
        pad_ref[halo + t:2 * halo + t, :] = jnp.zeros((halo, conv_ch), F32)
        for r, n in _post_tiles(t):
            pad_ref[halo + r:halo + r + n, :] = u_ref[r:r + n, GROUP_W:GROUP_W + conv_ch]
        for r, n in _post_tiles(t):
            acc = jnp.zeros((n, conv_ch), F32) + cb_ref[...]
            for tap in range(SSD_CONV):
                lo = halo + r + tap - half
                acc = acc + pad_ref[lo:lo + n, :] * cw_ref[tap:tap + 1, :]
            act = _silu(acc)
            grp = [act[:, GROUP_W + m * HEAD_DIM:GROUP_W + (m + 1) * HEAD_DIM] for m in range(4)]
            x_ref[r:r + n, 0:GROUP_W] = act[:, 0:GROUP_W]
            x_ref[r:r + n, GROUP_W:2 * GROUP_W] = jnp.concatenate([grp[0], grp[0], grp[1], grp[1]], axis=1)
            x_ref[r:r + n, 2 * GROUP_W:3 * GROUP_W] = jnp.concatenate([grp[2], grp[2], grp[3], grp[3]], axis=1)

    conv(uc_ref, xc_ref, t_ctx)
    conv(ul_ref, xl_ref, t_lat)
    s_ref[...] = jnp.zeros_like(s_ref)
    consts = _scan_consts()
    cum3 = (_cum_mat(False), _cum_mat(True))
    expand = (_expand_mat(DT_OFF), _expand_mat(DT_OFF + N_HEADS))

    def multi_step(is_lat, chunks):
        x_ref, sm_ref, acc_ref = (xl_ref, sml_ref, accl_ref) if is_lat else (xc_ref, smc_ref, accc_ref)
        lanes = []
        for c, d in chunks:
            r = pl.multiple_of(c * CHUNK, CHUNK)
            dt = _spread(_softplus(sm_ref[pl.ds(r, CHUNK), :] + dtb_ref[...]), expand[d])
            lanes.append(dict(d=d, r=r, want_out=is_lat or ctx_out, dt=dt,
                              k=x_ref[pl.ds(r, CHUNK), GROUP_W:2 * GROUP_W],
                              q=x_ref[pl.ds(r, CHUNK), 2 * GROUP_W:3 * GROUP_W]))
        _lane_scores(lanes, consts[0][2])
        for ln in lanes:
            ln["g_b"] = _cumsum_b(ln["dt"] * an_ref[ln["d"]], cum3[ln["d"]])
            ln["v"] = x_ref[pl.ds(ln["r"], CHUNK), 0:GROUP_W] * ln["dt"]
        for ln, o in zip(lanes, _decay_lanes(lanes, consts, s_ref)):
            if ln["want_out"]:
                acc_ref[ln["d"], pl.ds(ln["r"], CHUNK), :] = o

    _scan_lanes(t_ctx // CHUNK, t_lat // CHUNK, multi_step)

    def post(u_ref, x_ref, acc_ref, y_ref, t):
        for r, n in _post_tiles(t):
            y = acc_ref[0, r:r + n, :] + acc_ref[1, r:r + n, :] + dsk_ref[...] * x_ref[r:r + n, 0:GROUP_W]
            y_ref[r:r + n, :] = _rms(y * _silu(u_ref[r:r + n, 0:GROUP_W]), nw_ref[...])

    post(ul_ref, xl_ref, accl_ref, yl_ref, t_lat)
    if ctx_out:
        post(uc_ref, xc_ref, accc_ref, yc_ref, t_ctx)


def _ssd_call(u_lat, u_ctx, sm_lat, sm_ctx, conv_w, conv_b, dt_bias, a_log, d_skip, nw, ctx_out):
    batch, t_lat, _ = u_lat.shape
    t_ctx = u_ctx.shape[1]
    conv_ch = GROUP_W + 2 * SSD_BC
    dtb = jnp.zeros((SMALL_W,), F32).at[DT_OFF:DT_OFF + 2 * N_HEADS].set(dt_bias.reshape(-1))[None]
    a_neg = jnp.repeat(-jnp.exp(a_log.astype(F32)), HEAD_DIM, axis=1)[:, None, :]
    cw = jnp.concatenate([conv_w, jnp.zeros((8 - SSD_CONV, conv_ch), F32)], axis=0)
    dsk = jnp.repeat(d_skip, HEAD_DIM)[None]
    scratch = _state_scratch(1, t_lat, t_ctx) + [
        pltpu.VMEM((t_lat, 3 * GROUP_W), F32), pltpu.VMEM((t_ctx, 3 * GROUP_W), F32),
        pltpu.VMEM((t_lat + 16, conv_ch), F32)]
    return _mixer_call(functools.partial(_ssd_body, ctx_out=ctx_out), "ssd", batch, t_lat, t_ctx, ctx_out,
                       [u_lat, u_ctx, sm_lat, sm_ctx], [cw, conv_b[None], dtb, a_neg, dsk, nw], scratch)


def _mlstm_body(ul_ref, uc_ref, sml_ref, smc_ref, gb_ref, nw_ref, *rest, ctx_out):
    if ctx_out:
        yl_ref, yc_ref, c_ref, n_ref, m_ref, accl_ref, accc_ref = rest
    else:
        yl_ref, c_ref, n_ref, m_ref, accl_ref, accc_ref = rest
        yc_ref = None
    t_lat, t_ctx = ul_ref.shape[0], uc_ref.shape[0]
    c_ref[...] = jnp.zeros_like(c_ref)
    n_ref[...] = jnp.zeros_like(n_ref)
    m_ref[...] = jnp.zeros_like(m_ref)
    consts = _scan_consts()
    same_head = consts[0][2]
    seg = same_head.astype(F32).astype(BF16)
    cum3 = (_cum_mat(False), _cum_mat(True))
    f_lane = tuple(GATE_OFF + 2 * N_HEADS * d + N_HEADS for d in (0, 1))
    expand_fi = tuple(jnp.concatenate([_expand_mat(f_lane[d]), _expand_mat(f_lane[d] - N_HEADS)], axis=1)
                      for d in (0, 1))
    lane = _iota2((1, SMALL_W), 1)
    is_f = (lane >= GATE_OFF) & ((lane - GATE_OFF) % (2 * N_HEADS) >= N_HEADS)

    def running_max(x, rev):
        row = _iota2(x.shape, 0)
        s = 1
        while s < CHUNK:
            shifted = pltpu.roll(x, CHUNK - s if rev else s, 0)
            valid = (row < CHUNK - s) if rev else (row >= s)
            x = jnp.maximum(x, jnp.where(valid, shifted, -jnp.inf))
            s *= 2
        return x

    def multi_step(is_lat, chunks):
        u_ref, sm_ref, acc_ref = (ul_ref, sml_ref, accl_ref) if is_lat else (uc_ref, smc_ref, accc_ref)
        want_out = is_lat or ctx_out
        lanes = []
        for c, d in chunks:
            r = pl.multiple_of(c * CHUNK, CHUNK)
            gates = sm_ref[pl.ds(r, CHUNK), :] + gb_ref[...]
            both = _spread(jnp.where(is_f, _log_sigmoid(gates), gates), expand_fi[d])
            lanes.append(dict(d=d, r=r, want_out=want_out, lf_b=both[:, 0:GROUP_W], ig_b=both[:, GROUP_W:2 * GROUP_W],
                              q=u_ref[pl.ds(r, CHUNK), 0:GROUP_W],
                              k=u_ref[pl.ds(r, CHUNK), GROUP_W:2 * GROUP_W] * (HEAD_DIM ** -0.5),
                              v=u_ref[pl.ds(r, CHUNK), 2 * GROUP_W:3 * GROUP_W]))
        _lane_scores(lanes, same_head)
        for ln in lanes:
            ln["g_b"] = _cumsum_b(ln["lf_b"], cum3[ln["d"]])
        m_now, n_now = [m_ref[0], m_ref[1]], [n_ref[0], n_ref[1]]
        for ln in lanes:
            d, g_b, ig_b = ln["d"], ln["g_b"], ln["ig_b"]
            mask_b, eye_b, _ = consts[d]
            g_tot = g_b[0:1] if d == 1 else g_b[CHUNK - 1:CHUNK]
            w_end = g_tot - g_b + ig_b
            b_max = jnp.max(w_end, axis=0, keepdims=True)
            ln["ke"] = ln["k"] * jnp.exp(w_end - b_max)
            m_in, n_in = m_now[d], n_now[d]
            m_new = jnp.maximum(g_tot + m_in, b_max)
            ln["old"], ln["new"] = jnp.exp(g_tot + m_in - m_new), jnp.exp(b_max - m_new)
            m_now[d] = m_new
            n_now[d] = ln["old"] * n_in + ln["new"] * jnp.sum(ln["ke"], axis=0, keepdims=True)
            if want_out:
                m_inter = g_b + m_in
                m_tot = jnp.maximum(m_inter, g_b + running_max(ig_b - g_b, d == 1))
                g_row = jnp.sum(jnp.where(eye_b, g_b, 0.0), axis=0, keepdims=True)
                ig_row = jnp.sum(jnp.where(eye_b, ig_b, 0.0), axis=0, keepdims=True)
                d_log = jnp.where(mask_b, g_b - g_row + ig_row, -jnp.inf)
                ln["qk"] = ln["scores"] * jnp.exp(d_log - m_tot)
                ln["s_inter"] = jnp.exp(m_inter - m_tot)
                ln["floor"] = jnp.exp(-m_tot)
                ln["den_terms"] = ln["qk"] + ln["s_inter"] * (ln["q"] * n_in)
        m_ref[0], m_ref[1] = m_now
        n_ref[0], n_ref[1] = n_now
        for ln in lanes:
            ln["kv"] = jnp.where(same_head, _bdot_tn(ln["ke"], ln["v"]), 0.0)
        if want_out:
            for ln in lanes:
                ln["num"] = _bdot(ln["qk"], _blockdiag(ln["v"], same_head))
                ln["den"] = _seg_sum(ln["den_terms"], same_head)
        for ln in lanes:
            d = ln["d"]
            c_in = c_ref[d]
            if want_out:
                num = ln["num"] + ln["s_inter"] * _bdot(ln["q"], c_in)
                acc_ref[d, pl.ds(ln["r"], CHUNK), :] = num / jnp.maximum(jnp.abs(ln["den"]), ln["floor"])
            c_ref[d] = ln["old"] * c_in + ln["new"] * ln["kv"]

    _scan_lanes(t_ctx // CHUNK, t_lat // CHUNK, multi_step)

    def post(u_ref, acc_ref, y_ref, t):
        for r, n in _post_tiles(t):
            o = _head_rms(acc_ref[0, r:r + n, :] + acc_ref[1, r:r + n, :], same_head, False)
            y_ref[r:r + n, :] = jax.nn.sigmoid(u_ref[r:r + n, 3 * GROUP_W:4 * GROUP_W]) * (o * nw_ref[...])

    post(ul_ref, accl_ref, yl_ref, t_lat)
    if ctx_out:
        post(uc_ref, accc_ref, yc_ref, t_ctx)


def _mlstm_call(u_lat, u_ctx, sm_lat, sm_ctx, gate_b, nw, ctx_out):
    batch, t_lat, _ = u_lat.shape
    t_ctx = u_ctx.shape[1]
    gb = jnp.zeros((SMALL_W,), F32).at[GATE_OFF:GATE_OFF + 4 * N_HEADS].set(gate_b.reshape(-1))[None]
    scratch = ([pltpu.VMEM((2, GROUP_W, GROUP_W), F32), pltpu.VMEM((2, 1, GROUP_W), F32),
                pltpu.VMEM((2, 1, GROUP_W), F32)]
               + [pltpu.VMEM((2, t_lat, GROUP_W), F32), pltpu.VMEM((2, t_ctx, GROUP_W), F32)])
    return _mixer_call(functools.partial(_mlstm_body, ctx_out=ctx_out), "mlstm", batch, t_lat, t_ctx, ctx_out,
                       [u_lat, u_ctx, sm_lat, sm_ctx], [gb, nw], scratch)


def _gla_body(ul_ref, uc_ref, sml_ref, smc_ref, gw_ref, gb_ref, nw_ref, *rest, ctx_out):
    if ctx_out:
        yl_ref, yc_ref, st_ref, accl_ref, accc_ref, gl_ref, gc_ref = rest
    else:
        yl_ref, st_ref, accl_ref, accc_ref, gl_ref, gc_ref = rest
        yc_ref = None
    t_lat, t_ctx = ul_ref.shape[0], uc_ref.shape[0]
    st_ref[...] = jnp.zeros_like(st_ref)
    n_sub = CHUNK // GLA_SUB
    i = lax.broadcasted_iota(I32, (CHUNK, CHUNK), 0)
    j = lax.broadcasted_iota(I32, (CHUNK, CHUNK), 1)
    same = (i // GLA_SUB) == (j // GLA_SUB)
    lcum = ((same & (j <= i)).astype(F32).astype(BF16), (same & (j >= i)).astype(F32).astype(BF16))
    expand = ((lax.broadcasted_iota(I32, (GLA_QK, GROUP_W), 0) // GLA_DK)
              == (lax.broadcasted_iota(I32, (GLA_QK, GROUP_W), 1) // HEAD_DIM)).astype(F32).astype(BF16)
    blockdiag = ((lax.broadcasted_iota(I32, (GROUP_W, GLA_QK), 0) // HEAD_DIM)
                 == (lax.broadcasted_iota(I32, (GROUP_W, GLA_QK), 1) // GLA_DK)).astype(F32)
    sub_row = lax.broadcasted_iota(I32, (CHUNK, 1), 0) % GLA_SUB

    def roll_sub(x, delta):
        return jnp.concatenate([pltpu.roll(x[a * GLA_SUB:(a + 1) * GLA_SUB], delta, 0) for a in range(n_sub)],
                               axis=0)

    def qkv(u_ref, r):
        return (u_ref[pl.ds(r, CHUNK), 0:GLA_QK] * (GLA_DK ** -0.5), u_ref[pl.ds(r, CHUNK), GLA_QK:2 * GLA_QK],
                u_ref[pl.ds(r, CHUNK), 2 * GLA_QK:2 * GLA_QK + GROUP_W])

    def intra(u_ref, sm_ref, g_ref, acc_ref, want_out, c):
        r = pl.multiple_of(c * CHUNK, CHUNK)
        lr = sm_ref[pl.ds(r, CHUNK), LR_OFF:LR_OFF + GLA_RANK]
        g = []
        for d in (0, 1):
            la = _log_sigmoid(_bdot(lr, gw_ref[d]) + gb_ref[d]) / GLA_TAU
            g.append(sum(jnp.dot(lcum[d], piece, preferred_element_type=F32) for piece in _split3(la)))
            g_ref[d, pl.ds(r, CHUNK), :] = g[d]
        if not want_out:
            return
        q, k, v = qkv(u_ref, r)
        prods, vals = [2.0 * q * k], [v]
        for delta in range(1, GLA_SUB):
            fwd_pair = sub_row >= delta
            e = jnp.exp(jnp.where(fwd_pair, g[0] - roll_sub(g[0], delta), g[1] - roll_sub(g[1], delta)))
            prods.append(q * roll_sub(k, delta) * e)
            vals.append(roll_sub(v, delta))
        w = jnp.dot(jnp.concatenate(prods, axis=0).astype(BF16), expand, preferred_element_type=F32)
        acc_ref[pl.ds(r, CHUNK), :] = sum(w[n * CHUNK:(n + 1) * CHUNK] * vals[n] for n in range(GLA_SUB))

    def intra_ctx(c, carry):
        intra(uc_ref, smc_ref, gc_ref, accc_ref, ctx_out, c)
        return carry

    def intra_lat(c, carry):
        intra(ul_ref, sml_ref, gl_ref, accl_ref, True, c)
        return carry

    lax.fori_loop(0, t_ctx // CHUNK, intra_ctx, 0)
    lax.fori_loop(0, t_lat // CHUNK, intra_lat, 0, unroll=4)

    def multi_step(is_lat, chunks):
        u_ref, g_ref, acc_ref = (ul_ref, gl_ref, accl_ref) if is_lat else (uc_ref, gc_ref, accc_ref)
        want_out = is_lat or ctx_out
        lanes = []
        for c, d in chunks:
            r = pl.multiple_of(c * CHUNK, CHUNK)
            q, k, v = qkv(u_ref, r)
            g = g_ref[d, pl.ds(r, CHUNK), :]
            subs = []
            for a in (range(n_sub) if d == 0 else range(n_sub - 1, -1, -1)):
                rows = slice(a * GLA_SUB, (a + 1) * GLA_SUB)
                g_a = g[rows]
                g_end = g_a[GLA_SUB - 1:GLA_SUB] if d == 0 else g_a[0:1]
                subs.append(dict(a=a, decay=jnp.exp(g_end),
                                 kv=_bdot_tn(v[rows], k[rows] * jnp.exp(g_end - g_a)) * blockdiag,
                                 qg=q[rows] * jnp.exp(g_a) if want_out else None))
            lanes.append(dict(d=d, r=r, subs=subs, out=[None] * n_sub))
        for step_lanes in (lanes[n:n + 2] for n in range(0, len(lanes), 2)):
            for n in range(n_sub):
                for ln in step_lanes:
                    sub, st = ln["subs"][n], st_ref[ln["d"]]
                    if want_out:
                        ln["out"][sub["a"]] = _bdot_nt(sub["qg"], st)
                    st_ref[ln["d"]] = st * sub["decay"] + sub["kv"]
        if want_out:
            for ln in lanes:
                acc_ref[pl.ds(ln["r"], CHUNK), :] += jnp.concatenate(ln["out"], axis=0)

    _scan_lanes(t_ctx // CHUNK, t_lat // CHUNK, multi_step)

    def post(u_ref, acc_ref, y_ref, t):
        for r, n in _post_tiles(t):
            o = _head_rms(acc_ref[r:r + n, :], _same_head(), False)
            y_ref[r:r + n, :] = o * nw_ref[...] * _silu(u_ref[r:r + n, 2 * GLA_QK + GROUP_W:2 * GLA_QK + 2 * GROUP_W])

    post(ul_ref, accl_ref, yl_ref, t_lat)
    if ctx_out:
        post(uc_ref, accc_ref, yc_ref, t_ctx)


def _gla_call(u_lat, u_ctx, sm_lat, sm_ctx, gate_w, gate_b, nw, ctx_out):
    batch, t_lat, _ = u_lat.shape
    t_ctx = u_ctx.shape[1]
    scratch = [pltpu.VMEM((2, GROUP_W, GLA_QK), F32),
               pltpu.VMEM((t_lat, GROUP_W), F32), pltpu.VMEM((t_ctx, GROUP_W), F32),
               pltpu.VMEM((2, t_lat, GLA_QK), F32), pltpu.VMEM((2, t_ctx, GLA_QK), F32)]
    return _mixer_call(functools.partial(_gla_body, ctx_out=ctx_out), "gla", batch, t_lat, t_ctx, ctx_out,
                       [u_lat, u_ctx, sm_lat, sm_ctx], [gate_w, gate_b[:, None, :], nw], scratch)


def _out_body(yr_ref, yg_ref, ys_ref, ym_ref, h_ref, mod_ref, wo_ref, n2_ref, rw_ref, h1_ref, hn_ref, lt_ref):
    mod = mod_ref[...]
    proj = sum(_bdot(y_ref[...], wo_ref[n * GROUP_W:(n + 1) * GROUP_W, :])
               for n, y_ref in enumerate((yr_ref, yg_ref, ys_ref, ym_ref)))
    h1 = h_ref[...] + mod[2:3, :] * proj
    h1_ref[...] = h1
    xn = _rms(h1, n2_ref[...]) * (1.0 + mod[4:5, :]) + mod[3:4, :]
    hn_ref[...] = xn.astype(BF16)
    x_hi, x_mid, _ = _split3(xn)
    parts = (jnp.dot(x_hi, rw_ref[...], preferred_element_type=F32)
             + jnp.dot(x_mid, rw_ref[...], preferred_element_type=F32)).T
    lt_ref[...] = parts[0:N_EXPERTS] + parts[N_EXPERTS:2 * N_EXPERTS] + parts[2 * N_EXPERTS:3 * N_EXPERTS]


def _out_call(ys, h, mod_l, wo, n2w, rw3, is_ctx):
    batch, t, d = h.shape
    tm = min(512, t)
    y_spec = pl.BlockSpec((None, tm, GROUP_W), lambda b, i: (b, i, 0))
    return pl.pallas_call(
        _out_body,
        grid=(batch, t // tm),
        in_specs=[y_spec, y_spec, y_spec, y_spec,
                  pl.BlockSpec((None, tm, d), lambda b, i: (b, i, 0)),
                  _mod_spec(is_ctx, batch),
                  pl.BlockSpec((d, d), lambda b, i: (0, 0)),
                  pl.BlockSpec((1, d), lambda b, i: (0, 0)),
                  pl.BlockSpec(rw3.shape, lambda b, i: (0, 0))],
        out_specs=[pl.BlockSpec((None, tm, d), lambda b, i: (b, i, 0)),
                   pl.BlockSpec((None, tm, d), lambda b, i: (b, i, 0)),
                   pl.BlockSpec((None, N_EXPERTS, tm), lambda b, i: (b, 0, i))],
        out_shape=[jax.ShapeDtypeStruct((batch, t, d), F32),
                   jax.ShapeDtypeStruct((batch, t, d), BF16),
                   jax.ShapeDtypeStruct((batch, N_EXPERTS, t), F32)],
        compiler_params=_cparams("parallel", "parallel"),
        name="out_proj",
    )(*ys, h, mod_l, wo, n2w, rw3)


def _excl_prefix(flags, t):
    blk = min(256, t)
    below = (lax.broadcasted_iota(I32, (blk, blk), 0) < lax.broadcasted_iota(I32, (blk, blk), 1))
    below = below.astype(F32).astype(BF16)
    outs = []
    carry = jnp.zeros((flags.shape[0], 1), F32)
    for n in range(t // blk):
        f = flags[:, n * blk:(n + 1) * blk]
        outs.append(jnp.dot(f.astype(BF16), below, preferred_element_type=F32) + carry)
        carry = carry + jnp.sum(f, axis=1, keepdims=True)
    return jnp.concatenate(outs, axis=1)


def _route_body(lt_ref, hn_ref, xg_ref, affc_ref, post_ref, pos_s, aff_s, *, cap):
    t = hn_ref.shape[0]
    logits = lt_ref[...]
    ex = jnp.exp(logits - jnp.max(logits, axis=0, keepdims=True))
    aff = ex / jnp.sum(ex, axis=0, keepdims=True)
    def count_at_least(cand):
        return jnp.sum((aff >= pltpu.bitcast(cand, F32)).astype(F32), axis=1, keepdims=True)

    def fix_two_bits(n, thr):
        lo = 28 - 2 * n
        best = thr
        for digit in (1, 2, 3):
            cand = thr | (jnp.int32(digit) << lo)
            best = jnp.where(count_at_least(cand) >= float(cap), cand, best)
        return best

    thr = lax.fori_loop(0, 15, fix_two_bits, jnp.zeros((N_EXPERTS, 1), I32))
    above = (aff >= pltpu.bitcast(thr + 1, F32)).astype(F32)
    tied = (aff >= pltpu.bitcast(thr, F32)).astype(F32) - above
    need = float(cap) - jnp.sum(above, axis=1, keepdims=True)
    sel = above + tied * (_excl_prefix(tied, t) < need).astype(F32)
    pos = jnp.where(sel > 0.0, _excl_prefix(sel, t), -1.0)
    pos_s[...] = pos
    aff_s[...] = aff
    post_ref[...] = jnp.concatenate([pos, jnp.full((SMALL_W - N_EXPERTS, t), -1.0, F32)], axis=0).T
    slots = lax.broadcasted_iota(I32, (cap, t), 0).astype(F32)

    def gather(e, carry):
        hit = slots == pos_s[pl.ds(e, 1), :]
        xg_ref[e] = jnp.dot(hit.astype(BF16), hn_ref[...], preferred_element_type=F32).astype(BF16)
        a = jnp.sum(jnp.where(hit, aff_s[pl.ds(e, 1), :], 0.0), axis=1, keepdims=True)
        affc_ref[e] = jnp.broadcast_to(a, (cap, SMALL_W))
        return carry

    lax.fori_loop(0, N_EXPERTS, gather, 0, unroll=2)


def _route_call(lt, hn, cap):
    batch, t, d = hn.shape
    return pl.pallas_call(
        functools.partial(_route_body, cap=cap),
        grid=(batch,),
        in_specs=[pl.BlockSpec((None, N_EXPERTS, t), lambda b: (b, 0, 0)),
                  pl.BlockSpec((None, t, d), lambda b: (b, 0, 0))],
        out_specs=[pl.BlockSpec((N_EXPERTS, cap, d), lambda b: (0, b, 0)),
                   pl.BlockSpec((N_EXPERTS, cap, SMALL_W), lambda b: (0, b, 0)),
                   pl.BlockSpec((None, t, SMALL_W), lambda b: (b, 0, 0))],
        out_shape=[jax.ShapeDtypeStruct((N_EXPERTS, batch * cap, d), BF16),
                   jax.ShapeDtypeStruct((N_EXPERTS, batch * cap, SMALL_W), F32),
                   jax.ShapeDtypeStruct((batch, t, SMALL_W), F32)],
        scratch_shapes=[pltpu.VMEM((N_EXPERTS, t), F32), pltpu.VMEM((N_EXPERTS, t), F32)],
        compiler_params=_cparams("parallel"),
        name="route_gather",
    )(lt, hn)


FFN_TF = 512
FFN_TM = 512


def _ffn_body(*refs, n_sets):
    ins = refs[:2 * n_sets]
    wg_ref, wu_ref, wd_ref = refs[2 * n_sets:2 * n_sets + 3]
    outs = refs[2 * n_sets + 3:3 * n_sets + 3]
    accs = refs[3 * n_sets + 3:4 * n_sets + 3]
    f = pl.program_id(1)
    dot = functools.partial(jnp.dot, preferred_element_type=F32)
    for s in range(n_sets):
        x_ref, a_ref, y_ref, acc_ref = ins[2 * s], ins[2 * s + 1], outs[s], accs[s]
        rows = x_ref.shape[0]

        @pl.when(f == 0)
        def _():
            acc_ref[...] = jnp.zeros_like(acc_ref)

        for r in range(0, rows, FFN_TM):
            n = min(FFN_TM, rows - r)
            x = x_ref[r:r + n, :]
            hid = _silu(dot(x, wg_ref[...].astype(BF16))) * dot(x, wu_ref[...].astype(BF16))
            acc_ref[r:r + n, :] += dot(hid.astype(BF16), wd_ref[...].astype(BF16))

        @pl.when(f == pl.num_programs(1) - 1)
        def _():
            for r in range(0, rows, FFN_TM):
                n = min(FFN_TM, rows - r)
                y_ref[r:r + n, :] = (acc_ref[r:r + n, :] * a_ref[r:r + n, 0:1]).astype(BF16)


def _ffn_call(sets, wg, wu, wd, layer):
    _, n_exp, d, ff = wg.shape
    in_specs, args, out_specs, out_shape, scratch = [], [], [], [], []
    for xg, affc in sets:
        m = xg.shape[1]
        in_specs += [pl.BlockSpec((None, m, d), lambda e, f: (e, 0, 0)),
                     pl.BlockSpec((None, m, SMALL_W), lambda e, f: (e, 0, 0))]
        args += [xg, affc]
        out_specs.append(pl.BlockSpec((None, m, d), lambda e, f: (e, 0, 0)))
        out_shape.append(jax.ShapeDtypeStruct((n_exp, m, d), BF16))
        scratch.append(pltpu.VMEM((m, d), F32))
    in_specs += [pl.BlockSpec((None, None, d, FFN_TF), lambda e, f: (layer, e, 0, f)),
                 pl.BlockSpec((None, None, d, FFN_TF), lambda e, f: (layer, e, 0, f)),
                 pl.BlockSpec((None, None, FFN_TF, d), lambda e, f: (layer, e, f, 0))]
    return pl.pallas_call(
        functools.partial(_ffn_body, n_sets=len(sets)),
        grid=(n_exp, ff // FFN_TF),
        in_specs=in_specs, out_specs=out_specs, out_shape=out_shape, scratch_shapes=scratch,
        compiler_params=_cparams("parallel", "arbitrary"),
        name="expert_ffn",
    )(*args, wg, wu, wd)


def _scatter_body(post_ref, y_ref, h1_ref, mod_ref, fw_ref, o_ref, *, final):
    n_exp, cap, d = y_ref.shape
    pos = post_ref[...]
    slots = lax.broadcasted_iota(I32, (pos.shape[0], cap), 1).astype(F32)
    hit = jnp.concatenate([(pos[:, e:e+1] == slots).astype(BF16) for e in range(n_exp)], axis=1)
    upd = jnp.dot(hit, y_ref[...].reshape(n_exp * cap, d), preferred_element_type=F32)
    h2 = h1_ref[...] + mod_ref[5:6, :] * upd
    o_ref[...] = _rms(h2, fw_ref[...]) if final else h2


def _scatter_call(post, y, h1, mod_l, is_ctx, final_w):
    batch, t, d = h1.shape
    n_exp = y.shape[0]
    cap = y.shape[1] // batch
    tm = min(512, t)
    final = final_w is not None
    fw = final_w if final else jnp.ones((1, d), F32)
    return pl.pallas_call(
        functools.partial(_scatter_body, final=final),
        grid=(batch, t // tm),
        in_specs=[pl.BlockSpec((None, tm, SMALL_W), lambda b, i: (b, i, 0)),
                  pl.BlockSpec((n_exp, cap, d), lambda b, i: (0, b, 0)),
                  pl.BlockSpec((None, tm, d), lambda b, i: (b, i, 0)),
                  _mod_spec(is_ctx, batch),
                  pl.BlockSpec((1, d), lambda b, i: (0, 0))],
        out_specs=pl.BlockSpec((None, tm, d), lambda b, i: (b, i, 0)),
        out_shape=jax.ShapeDtypeStruct((batch, t, d), F32),
        compiler_params=_cparams("parallel", "parallel"),
        name="scatter_residual",
    )(post, y, h1, mod_l, fw)


def kernel(x, c, ctx, c_ctx, w_mod, b_mod, norm1_w, norm2_w, w_in, w_out, ret_decay_logit, ret_norm_w,
           gla_gate_w, gla_gate_b, gla_norm_w, ssd_conv_w, ssd_conv_b, ssd_dt_bias, ssd_a_log, ssd_d,
           ssd_norm_w, mlstm_gate_b, mlstm_norm_w, router_w, expert_w_gate, expert_w_up, expert_w_down,
           final_norm_w):
    batch, t_lat, d = x.shape
    t_ctx = ctx.shape[1]
    depth = w_mod.shape[0]
    assert batch + 1 <= ROW_PAD and d == D_MODEL
    for t in (t_lat, t_ctx):
        assert t % 128 == 0 and (t // CHUNK) % min(SCAN_STEPS, t // CHUNK) == 0
    assert t_lat % 256 == 0
    cs = jnp.concatenate([c, c_ctx[None], jnp.zeros((ROW_PAD - batch - 1, d), F32)], axis=0)
    mod = _mod_call(cs, w_mod, b_mod).reshape(depth, ROW_PAD, 6, d)
    cos, sin = _rope_tables(t_lat)
    cap_lat = EC_CAPACITY * t_lat // N_EXPERTS
    cap_ctx = EC_CAPACITY * t_ctx // N_EXPERTS
    h_lat, h_ctx = x, ctx
    for l in range(depth):
        ctx_out = l < depth - 1
        mod_l = mod[l]
        w_big = _arrange_w_in(w_in[l])
        ul = _in_call(h_lat, mod_l, norm1_w[l][None], w_big, False)
        uc = _in_call(h_ctx, mod_l, norm1_w[l][None], w_big, True)
        ys = [_ret_call(ul[0], uc[0], jax.nn.log_sigmoid(ret_decay_logit[l].astype(F32)), cos, sin,
                        ret_norm_w[l][None], ctx_out),
              _gla_call(ul[1], uc[1], ul[4], uc[4], gla_gate_w[l], gla_gate_b[l], gla_norm_w[l][None], ctx_out),
              _ssd_call(ul[2], uc[2], ul[4], uc[4], ssd_conv_w[l], ssd_conv_b[l], ssd_dt_bias[l], ssd_a_log[l],
                        ssd_d[l], ssd_norm_w[l][None], ctx_out),
              _mlstm_call(ul[3], uc[3], ul[4], uc[4], mlstm_gate_b[l], mlstm_norm_w[l][None], ctx_out)]
        wo = w_out[l].astype(BF16)
        rw3 = jnp.pad(jnp.concatenate(_split3(router_w[l]), axis=1), ((0, 0), (0, SMALL_W - 3 * N_EXPERTS)))
        n2w = norm2_w[l][None]
        h1_lat, hn_lat, lt_lat = _out_call([y[0] for y in ys], h_lat, mod_l, wo, n2w, rw3, False)
        xg_lat, affc_lat, post_lat = _route_call(lt_lat, hn_lat, cap_lat)
        sets = [(xg_lat, affc_lat)]
        if ctx_out:
            h1_ctx, hn_ctx, lt_ctx = _out_call([y[1] for y in ys], h_ctx, mod_l, wo, n2w, rw3, True)
            xg_ctx, affc_ctx, post_ctx = _route_call(lt_ctx, hn_ctx, cap_ctx)
            sets.append((xg_ctx, affc_ctx))
        y_exp = _ffn_call(sets, expert_w_gate, expert_w_up, expert_w_down, l)
        h_lat = _scatter_call(post_lat, y_exp[0], h1_lat, mod_l, False,
                              None if ctx_out else final_norm_w[None])
        if ctx_out:
            h_ctx = _scatter_call(post_ctx, y_exp[1], h1_ctx, mod_l, True, None)
    return h_lat
```

```python
import functools
import math

import jax
import jax.numpy as jnp
from jax import lax
from jax.experimental import pallas as pl
from jax.experimental.pallas import tpu as pltpu

F32 = jnp.float32
BF16 = jnp.bfloat16
I32 = jnp.int32

D_MODEL = 1024
GRID_W = 64
EPS = 1e-6
ROPE_BASE = 10000.0
GROUP_W = 256
HEAD_DIM = 64
N_HEADS = 4
CHUNK = 64
GLA_DK = 32
GLA_QK = 128
GLA_RANK = 16
GLA_TAU = 16.0
GLA_SUB = 16
SSD_CONV = 5
SSD_BC = 128
N_EXPERTS = 16
EC_CAPACITY = 2
EXPERT_FF = 1536
SMALL_W = 128
LR_OFF, DT_OFF, GATE_OFF = 0, 16, 24
ROW_PAD = 16
VMEM_LIMIT = 56 * 1024 * 1024


def _cparams(*sem):
    return pltpu.CompilerParams(dimension_semantics=sem, vmem_limit_bytes=VMEM_LIMIT)


def _bdot(a, b):
    return jnp.dot(a.astype(BF16), b.astype(BF16), preferred_element_type=F32)


def _bdot_nt(a, b):
    return lax.dot_general(a.astype(BF16), b.astype(BF16), (((1,), (1,)), ((), ())),
                           preferred_element_type=F32)


def _bdot_tn(a, b):
    return lax.dot_general(a.astype(BF16), b.astype(BF16), (((0,), (0,)), ((), ())),
                           preferred_element_type=F32)


def _silu(x):
    return x * jax.nn.sigmoid(x)


def _softplus(x):
    return jnp.maximum(x, 0.0) + jnp.log1p(jnp.exp(-jnp.abs(x)))


def _log_sigmoid(x):
    return -_softplus(-x)


def _rms(x, w):
    return x * lax.rsqrt(jnp.mean(x * x, axis=-1, keepdims=True) + EPS) * w


def _mod_body(cs_ref, w_ref, b_ref, o_ref):
    o_ref[...] = _bdot(_silu(cs_ref[...]), w_ref[...]) + b_ref[...]


def _mod_call(cs, w_mod, b_mod):
    depth, d, n = w_mod.shape
    tn = 1536
    return pl.pallas_call(
        _mod_body,
        grid=(depth, n // tn),
        in_specs=[pl.BlockSpec((ROW_PAD, d), lambda l, j: (0, 0)),
                  pl.BlockSpec((None, d, tn), lambda l, j: (l, 0, j)),
                  pl.BlockSpec((None, 1, tn), lambda l, j: (l, 0, j))],
        out_specs=pl.BlockSpec((None, ROW_PAD, tn), lambda l, j: (l, 0, j)),
        out_shape=jax.ShapeDtypeStruct((depth, ROW_PAD, n), F32),
        compiler_params=_cparams("parallel", "parallel"),
        name="adaln",
    )(cs, w_mod, b_mod.reshape(depth, 1, n))


def _mod_spec(is_ctx, batch):
    if is_ctx:
        return pl.BlockSpec((None, 6, D_MODEL), lambda b, i: (batch, 0, 0))
    return pl.BlockSpec((None, 6, D_MODEL), lambda b, i: (b, 0, 0))


IN_WIDTHS = (1024, 768, 768, 1024, SMALL_W)


def _in_body(h_ref, mod_ref, nw_ref, w_ref, ret_ref, gla_ref, ssd_ref, mls_ref, sm_ref):
    mod = mod_ref[...]
    xn = _rms(h_ref[...], nw_ref[...]) * (1.0 + mod[1:2, :]) + mod[0:1, :]
    u = jnp.dot(xn.astype(BF16), w_ref[...], preferred_element_type=F32)
    off = 0
    for ref, w in zip((ret_ref, gla_ref, ssd_ref, mls_ref, sm_ref), IN_WIDTHS):
        ref[...] = u[:, off:off + w]
        off += w


def _in_call(h, mod_l, nw, w_big, is_ctx):
    batch, t, d = h.shape
    tm = min(512, t)
    n = w_big.shape[1]
    return pl.pallas_call(
        _in_body,
        grid=(batch, t // tm),
        in_specs=[pl.BlockSpec((None, tm, d), lambda b, i: (b, i, 0)),
                  _mod_spec(is_ctx, batch),
                  pl.BlockSpec((1, d), lambda b, i: (0, 0)),
                  pl.BlockSpec((d, n), lambda b, i: (0, 0))],
        out_specs=[pl.BlockSpec((None, tm, w), lambda b, i: (b, i, 0)) for w in IN_WIDTHS],
        out_shape=[jax.ShapeDtypeStruct((batch, t, w), F32) for w in IN_WIDTHS],
        compiler_params=_cparams("parallel", "parallel"),
        name="in_proj",
    )(h, mod_l, nw, w_big)


def _arrange_w_in(w):
    ret_end = 4 * GROUP_W
    gla_main = ret_end + 2 * GLA_QK + 2 * GROUP_W
    gla_end = gla_main + GLA_RANK
    ssd_main = gla_end + GROUP_W + GROUP_W + 2 * SSD_BC
    ssd_end = ssd_main + 2 * N_HEADS
    mls_main = ssd_end + 4 * GROUP_W
    mls_end = mls_main + 4 * N_HEADS
    narrow = (gla_end - gla_main) + (ssd_end - ssd_main) + (mls_end - mls_main)
    pad = jnp.zeros((w.shape[0], SMALL_W - narrow), w.dtype)
    return jnp.concatenate([w[:, :ret_end], w[:, ret_end:gla_main], w[:, gla_end:ssd_main],
                            w[:, ssd_end:mls_main], w[:, gla_main:gla_end], w[:, ssd_main:ssd_end],
                            w[:, mls_main:mls_end], pad], axis=1).astype(BF16)


def _split3(x):
    hi = x.astype(BF16)
    r1 = x - hi.astype(F32)
    mid = r1.astype(BF16)
    return hi, mid, (r1 - mid.astype(F32)).astype(BF16)


def _iota2(shape, axis):
    return lax.broadcasted_iota(I32, shape, axis)


def _block_consts(rev):
    i = _iota2((CHUNK, GROUP_W), 0)
    j = _iota2((CHUNK, GROUP_W), 1) % HEAD_DIM
    return ((j >= i) if rev else (j <= i)), (j == i)


def _same_head():
    return (_iota2((GROUP_W, GROUP_W), 0) // HEAD_DIM) == (_iota2((GROUP_W, GROUP_W), 1) // HEAD_DIM)


def _cum_mat(rev):
    i = _iota2((CHUNK, 3 * CHUNK), 0)
    t = _iota2((CHUNK, 3 * CHUNK), 1) % CHUNK
    return ((t >= i) if rev else (t <= i)).astype(F32).astype(BF16)


def _expand_mat(first_lane, width=GROUP_W):
    return (_iota2((SMALL_W, width), 0) == first_lane + _iota2((SMALL_W, width), 1) // HEAD_DIM
            ).astype(F32).astype(BF16)


def _spread(x, expand):
    n = x.shape[0]
    y = jnp.dot(jnp.concatenate(_split3(x), axis=0), expand, preferred_element_type=F32)
    return (y[0:n] + y[n:2 * n]) + y[2 * n:3 * n]


def _cumsum_b(x_b, cum3):
    return jnp.dot(cum3, jnp.concatenate(_split3(x_b), axis=0), preferred_element_type=F32)


def _tile_heads(x):
    return jnp.concatenate([x] * N_HEADS, axis=0)


def _blockdiag(x, same_head):
    xb = _tile_heads(x.astype(BF16))
    return jnp.where(same_head, xb, jnp.zeros_like(xb))


def _lane_scores(lanes, same_head):
    for ln in lanes:
        if ln["want_out"]:
            ln["scores"] = _bdot_nt(ln["q"], _blockdiag(ln["k"], same_head))


def _decay_lanes(lanes, consts, s_ref):
    same_head = consts[0][2]
    for ln in lanes:
        mask_b, eye_b, _ = consts[ln["d"]]
        g_b = ln["g_b"]
        ln["g_tot"] = g_b[0:1] if ln["d"] == 1 else g_b[CHUNK - 1:CHUNK]
        if ln["want_out"]:
            g_row = jnp.sum(jnp.where(eye_b, g_b, 0.0), axis=0, keepdims=True)
            ln["p"] = ln["scores"] * jnp.exp(jnp.where(mask_b, g_b - g_row, -jnp.inf))
            ln["qg"] = ln["q"] * jnp.exp(g_b)
        ln["khat"] = ln["k"] * jnp.exp(ln["g_tot"] - g_b)
    for ln in lanes:
        ln["kv"] = jnp.where(same_head, _bdot_tn(ln["khat"], ln["v"]), 0.0)
    for ln in lanes:
        if ln["want_out"]:
            ln["pv"] = _bdot(ln["p"], _blockdiag(ln["v"], same_head))
    outs = []
    for ln in lanes:
        s_in = s_ref[ln["d"]]
        outs.append(ln["pv"] + _bdot(ln["qg"], s_in) if ln["want_out"] else None)
        s_ref[ln["d"]] = jnp.exp(ln["g_tot"]) * s_in + ln["kv"]
    return outs


def _seg_sum(x, same_head):
    seg = same_head.astype(F32).astype(BF16)
    hi = x.astype(BF16)
    lo = (x - hi.astype(F32)).astype(BF16)
    return jnp.dot(hi, seg, preferred_element_type=F32) + jnp.dot(lo, seg, preferred_element_type=F32)


def _head_rms(x, same_head, center):
    if center:
        x = x - _seg_sum(x, same_head) * (1.0 / HEAD_DIM)
    return x * lax.rsqrt(_seg_sum(x * x, same_head) * (1.0 / HEAD_DIM) + EPS)


SCAN_STEPS = 8


def _scan_lanes(n_ctx, n_lat, multi_step):
    def run(is_lat, n):
        steps = min(SCAN_STEPS, n)

        def it(s, carry):
            lanes = []
            for j in range(steps):
                lanes += [(steps * s + j, 0), (n - 1 - steps * s - j, 1)]
            multi_step(is_lat, lanes)
            return carry

        lax.fori_loop(0, n // steps, it, 0)

    run(False, n_ctx)
    run(True, n_lat)


def _post_tiles(t):
    tile = min(256, t)
    return [(r, tile) for r in range(0, t, tile)]


def _mixer_call(body, name, batch, t_lat, t_ctx, ctx_out, per_sample, shared, scratch):
    in_specs = [pl.BlockSpec((None,) + a.shape[1:], lambda b, nd=a.ndim: (b,) + (0,) * (nd - 1))
                for a in per_sample]
    in_specs += [pl.BlockSpec(a.shape, lambda b, nd=a.ndim: (0,) * nd) for a in shared]
    args = list(per_sample) + list(shared)
    out_shape = [jax.ShapeDtypeStruct((batch, t_lat, GROUP_W), F32)]
    out_specs = [pl.BlockSpec((None, t_lat, GROUP_W), lambda b: (b, 0, 0))]
    if ctx_out:
        out_shape.append(jax.ShapeDtypeStruct((batch, t_ctx, GROUP_W), F32))
        out_specs.append(pl.BlockSpec((None, t_ctx, GROUP_W), lambda b: (b, 0, 0)))
    return pl.pallas_call(
        body, grid=(batch,), in_specs=in_specs, out_specs=out_specs, out_shape=out_shape,
        scratch_shapes=scratch, compiler_params=_cparams("parallel"), name=name,
    )(*args)


def _rope(x, cos, sin_signed, first_half):
    return x * cos + jnp.where(first_half, pltpu.roll(x, x.shape[1] - 16, 1), pltpu.roll(x, 16, 1)) * sin_signed


def _scan_consts():
    same_head = _same_head()
    return tuple(_block_consts(rev) + (same_head,) for rev in (False, True))


def _ret_body(ul_ref, uc_ref, lg_ref, cos_ref, sin_ref, nw_ref, *rest, ctx_out):
    if ctx_out:
        yl_ref, yc_ref, s_ref, accl_ref, accc_ref = rest
    else:
        yl_ref, s_ref, accl_ref, accc_ref = rest
        yc_ref = None
    t_lat, t_ctx = ul_ref.shape[0], uc_ref.shape[0]
    s_ref[...] = jnp.zeros_like(s_ref)
    first_half = (_iota2((1, GROUP_W), 1) % 32) < 16
    consts = _scan_consts()
    row = _iota2((CHUNK, GROUP_W), 0).astype(F32)
    n_decays = (row + 1.0, CHUNK - row)

    def multi_step(is_lat, chunks):
        u_ref, acc_ref = (ul_ref, accl_ref) if is_lat else (uc_ref, accc_ref)
        lanes = []
        for c, d in chunks:
            r = pl.multiple_of(c * CHUNK, CHUNK)
            q = u_ref[pl.ds(r, CHUNK), 0:GROUP_W] * (HEAD_DIM ** -0.5)
            k = u_ref[pl.ds(r, CHUNK), GROUP_W:2 * GROUP_W]
            if is_lat:
                cos, sin = cos_ref[pl.ds(r, CHUNK), :], sin_ref[pl.ds(r, CHUNK), :]
                q, k = _rope(q, cos, sin, first_half), _rope(k, cos, sin, first_half)
            lanes.append(dict(d=d, r=r, want_out=is_lat or ctx_out, q=q, k=k,
                              v=u_ref[pl.ds(r, CHUNK), 2 * GROUP_W:3 * GROUP_W], g_b=n_decays[d] * lg_ref[d]))
        _lane_scores(lanes, consts[0][2])
        for ln, o in zip(lanes, _decay_lanes(lanes, consts, s_ref)):
            if ln["want_out"]:
                acc_ref[ln["d"], pl.ds(ln["r"], CHUNK), :] = o

    _scan_lanes(t_ctx // CHUNK, t_lat // CHUNK, multi_step)

    def post(u_ref, acc_ref, y_ref, t):
        for r, n in _post_tiles(t):
            o = _head_rms(acc_ref[0, r:r + n, :] + acc_ref[1, r:r + n, :], consts[0][2], True)
            y_ref[r:r + n, :] = o * nw_ref[...] * _silu(u_ref[r:r + n, 3 * GROUP_W:4 * GROUP_W])

    post(ul_ref, accl_ref, yl_ref, t_lat)
    if ctx_out:
        post(uc_ref, accc_ref, yc_ref, t_ctx)


def _state_scratch(n, t_lat, t_ctx):
    return ([pltpu.VMEM((2, GROUP_W, GROUP_W), F32)] * n
            + [pltpu.VMEM((2, t_lat, GROUP_W), F32), pltpu.VMEM((2, t_ctx, GROUP_W), F32)])


def _ret_call(u_lat, u_ctx, log_gamma, cos, sin, nw, ctx_out):
    batch, t_lat, _ = u_lat.shape
    t_ctx = u_ctx.shape[1]
    lg = jnp.repeat(log_gamma, HEAD_DIM, axis=1)[:, None, :]
    return _mixer_call(functools.partial(_ret_body, ctx_out=ctx_out), "retention", batch, t_lat, t_ctx, ctx_out,
                       [u_lat, u_ctx], [lg, cos, sin, nw], _state_scratch(1, t_lat, t_ctx))


def _rope_tables(t_lat):
    nf = 16
    inv = ROPE_BASE ** (-jnp.arange(nf, dtype=F32) / nf)
    rows = jnp.repeat(jnp.arange(t_lat // GRID_W), GRID_W).astype(F32)
    cols = jnp.tile(jnp.arange(GRID_W), t_lat // GRID_W).astype(F32)
    ang_r = rows[:, None] * inv[None, :]
    ang_c = cols[:, None] * inv[None, :]
    cos_h = jnp.concatenate([jnp.cos(ang_r)] * 2 + [jnp.cos(ang_c)] * 2, axis=1)
    sin_h = jnp.concatenate([-jnp.sin(ang_r), jnp.sin(ang_r), -jnp.sin(ang_c), jnp.sin(ang_c)], axis=1)
    return jnp.tile(cos_h, (1, N_HEADS)), jnp.tile(sin_h, (1, N_HEADS))


def _ssd_body(ul_ref, uc_ref, sml_ref, smc_ref, cw_ref, cb_ref, dtb_ref, an_ref, dsk_ref, nw_ref, *rest,
              ctx_out):
    if ctx_out:
        yl_ref, yc_ref, s_ref, accl_ref, accc_ref, xl_ref, xc_ref, pad_ref = rest
    else:
        yl_ref, s_ref, accl_ref, accc_ref, xl_ref, xc_ref, pad_ref = rest
        yc_ref = None
    t_lat, t_ctx = ul_ref.shape[0], uc_ref.shape[0]
    conv_ch = GROUP_W + 2 * SSD_BC
    half = (SSD_CONV - 1) // 2
    halo = 8

    def conv(u_ref, x_ref, t):
        pad_ref[0:halo, :] = jnp.zeros((halo, conv_ch), F32)
        pad_ref[halo + t:2 * halo + t, :] = jnp.zeros((halo, conv_ch), F32)
        for r, n in _post_tiles(t):
            pad_ref[halo + r:halo + r + n, :] = u_ref[r:r + n, GROUP_W:GROUP_W + conv_ch]
        for r, n in _post_tiles(t):
            acc = jnp.zeros((n, conv_ch), F32) + cb_ref[...]
            for tap in range(SSD_CONV):
                lo = halo + r + tap - half
                acc = acc + pad_ref[lo:lo + n, :] * cw_ref[tap:tap + 1, :]
            act = _silu(acc)
            grp = [act[:, GROUP_W + m * HEAD_DIM:GROUP_W + (m + 1) * HEAD_DIM] for m in range(4)]
            x_ref[r:r + n, 0:GROUP_W] = act[:, 0:GROUP_W]
            x_ref[r:r + n, GROUP_W:2 * GROUP_W] = jnp.concatenate([grp[0], grp[0], grp[1], grp[1]], axis=1)
            x_ref[r:r + n, 2 * GROUP_W:3 * GROUP_W] = jnp.concatenate([grp[2], grp[2], grp[3], grp[3]], axis=1)

    conv(uc_ref, xc_ref, t_ctx)
    conv(ul_ref, xl_ref, t_lat)
    s_ref[...] = jnp.zeros_like(s_ref)
    consts = _scan_consts()
    cum3 = (_cum_mat(False), _cum_mat(True))
    expand = (_expand_mat(DT_OFF), _expand_mat(DT_OFF + N_HEADS))

    def multi_step(is_lat, chunks):
        x_ref, sm_ref, acc_ref = (xl_ref, sml_ref, accl_ref) if is_lat else (xc_ref, smc_ref, accc_ref)
        lanes = []
        for c, d in chunks:
            r = pl.multiple_of(c * CHUNK, CHUNK)
            dt = _spread(_softplus(sm_ref[pl.ds(r, CHUNK), :] + dtb_ref[...]), expand[d])
            lanes.append(dict(d=d, r=r, want_out=is_lat or ctx_out, dt=dt,
                              k=x_ref[pl.ds(r, CHUNK), GROUP_W:2 * GROUP_W],
                              q=x_ref[pl.ds(r, CHUNK), 2 * GROUP_W:3 * GROUP_W]))
        _lane_scores(lanes, consts[0][2])
        for ln in lanes:
            ln["g_b"] = _cumsum_b(ln["dt"] * an_ref[ln["d"]], cum3[ln["d"]])
            ln["v"] = x_ref[pl.ds(ln["r"], CHUNK), 0:GROUP_W] * ln["dt"]
        for ln, o in zip(lanes, _decay_lanes(lanes, consts, s_ref)):
            if ln["want_out"]:
                acc_ref[ln["d"], pl.ds(ln["r"], CHUNK), :] = o

    _scan_lanes(t_ctx // CHUNK, t_lat // CHUNK, multi_step)

    def post(u_ref, x_ref, acc_ref, y_ref, t):
        for r, n in _post_tiles(t):
            y = acc_ref[0, r:r + n, :] + acc_ref[1, r:r + n, :] + dsk_ref[...] * x_ref[r:r + n, 0:GROUP_W]
            y_ref[r:r + n, :] = _rms(y * _silu(u_ref[r:r + n, 0:GROUP_W]), nw_ref[...])

    post(ul_ref, xl_ref, accl_ref, yl_ref, t_lat)
    if ctx_out:
        post(uc_ref, xc_ref, accc_ref, yc_ref, t_ctx)


def _ssd_call(u_lat, u_ctx, sm_lat, sm_ctx, conv_w, conv_b, dt_bias, a_log, d_skip, nw, ctx_out):
    batch, t_lat, _ = u_lat.shape
    t_ctx = u_ctx.shape[1]
    conv_ch = GROUP_W + 2 * SSD_BC
    dtb = jnp.zeros((SMALL_W,), F32).at[DT_OFF:DT_OFF + 2 * N_HEADS].set(dt_bias.reshape(-1))[None]
    a_neg = jnp.repeat(-jnp.exp(a_log.astype(F32)), HEAD_DIM, axis=1)[:, None, :]
    cw = jnp.concatenate([conv_w, jnp.zeros((8 - SSD_CONV, conv_ch), F32)], axis=0)
    dsk = jnp.repeat(d_skip, HEAD_DIM)[None]
    scratch = _state_scratch(1, t_lat, t_ctx) + [
        pltpu.VMEM((t_lat, 3 * GROUP_W), F32), pltpu.VMEM((t_ctx, 3 * GROUP_W), F32),
        pltpu.VMEM((t_lat + 16, conv_ch), F32)]
    return _mixer_call(functools.partial(_ssd_body, ctx_out=ctx_out), "ssd", batch, t_lat, t_ctx, ctx_out,
                       [u_lat, u_ctx, sm_lat, sm_ctx], [cw, conv_b[None], dtb, a_neg, dsk, nw], scratch)


def _mlstm_body(ul_ref, uc_ref, sml_ref, smc_ref, gb_ref, nw_ref, *rest, ctx_out):
    if ctx_out:
        yl_ref, yc_ref, c_ref, n_ref, m_ref, accl_ref, accc_ref = rest
    else:
        yl_ref, c_ref, n_ref, m_ref, accl_ref, accc_ref = rest
        yc_ref = None
    t_lat, t_ctx = ul_ref.shape[0], uc_ref.shape[0]
    c_ref[...] = jnp.zeros_like(c_ref)
    n_ref[...] = jnp.zeros_like(n_ref)
    m_ref[...] = jnp.zeros_like(m_ref)
    consts = _scan_consts()
    same_head = consts[0][2]
    seg = same_head.astype(F32).astype(BF16)
    cum3 = (_cum_mat(False), _cum_mat(True))
    f_lane = tuple(GATE_OFF + 2 * N_HEADS * d + N_HEADS for d in (0, 1))
    expand_fi = tuple(jnp.concatenate([_expand_mat(f_lane[d]), _expand_mat(f_lane[d] - N_HEADS)], axis=1)
                      for d in (0, 1))
    lane = _iota2((1, SMALL_W), 1)
    is_f = (lane >= GATE_OFF) & ((lane - GATE_OFF) % (2 * N_HEADS) >= N_HEADS)

    def running_max(x, rev):
        row = _iota2(x.shape, 0)
        s = 1
        while s < CHUNK:
            shifted = pltpu.roll(x, CHUNK - s if rev else s, 0)
            valid = (row < CHUNK - s) if rev else (row >= s)
            x = jnp.maximum(x, jnp.where(valid, shifted, -jnp.inf))
            s *= 2
        return x

    def multi_step(is_lat, chunks):
        u_ref, sm_ref, acc_ref = (ul_ref, sml_ref, accl_ref) if is_lat else (uc_ref, smc_ref, accc_ref)
        want_out = is_lat or ctx_out
        lanes = []
        for c, d in chunks:
            r = pl.multiple_of(c * CHUNK, CHUNK)
            gates = sm_ref[pl.ds(r, CHUNK), :] + gb_ref[...]
            both = _spread(jnp.where(is_f, _log_sigmoid(gates), gates), expand_fi[d])
            lanes.append(dict(d=d, r=r, want_out=want_out, lf_b=both[:, 0:GROUP_W], ig_b=both[:, GROUP_W:2 * GROUP_W],
                              q=u_ref[pl.ds(r, CHUNK), 0:GROUP_W],
                              k=u_ref[pl.ds(r, CHUNK), GROUP_W:2 * GROUP_W] * (HEAD_DIM ** -0.5),
                              v=u_ref[pl.ds(r, CHUNK), 2 * GROUP_W:3 * GROUP_W]))
        _lane_scores(lanes, same_head)
        for ln in lanes:
            ln["g_b"] = _cumsum_b(ln["lf_b"], cum3[ln["d"]])
        m_now, n_now = [m_ref[0], m_ref[1]], [n_ref[0], n_ref[1]]
        for ln in lanes:
            d, g_b, ig_b = ln["d"], ln["g_b"], ln["ig_b"]
            mask_b, eye_b, _ = consts[d]
            g_tot = g_b[0:1] if d == 1 else g_b[CHUNK - 1:CHUNK]
            w_end = g_tot - g_b + ig_b
            b_max = jnp.max(w_end, axis=0, keepdims=True)
            ln["ke"] = ln["k"] * jnp.exp(w_end - b_max)
            m_in, n_in = m_now[d], n_now[d]
            m_new = jnp.maximum(g_tot + m_in, b_max)
            ln["old"], ln["new"] = jnp.exp(g_tot + m_in - m_new), jnp.exp(b_max - m_new)
            m_now[d] = m_new
            n_now[d] = ln["old"] * n_in + ln["new"] * jnp.sum(ln["ke"], axis=0, keepdims=True)
            if want_out:
                m_inter = g_b + m_in
                m_tot = jnp.maximum(m_inter, g_b + running_max(ig_b - g_b, d == 1))
                g_row = jnp.sum(jnp.where(eye_b, g_b, 0.0), axis=0, keepdims=True)
                ig_row = jnp.sum(jnp.where(eye_b, ig_b, 0.0), axis=0, keepdims=True)
                d_log = jnp.where(mask_b, g_b - g_row + ig_row, -jnp.inf)
                ln["qk"] = ln["scores"] * jnp.exp(d_log - m_tot)
                ln["s_inter"] = jnp.exp(m_inter - m_tot)
                ln["floor"] = jnp.exp(-m_tot)
                ln["den_terms"] = ln["qk"] + ln["s_inter"] * (ln["q"] * n_in)
        m_ref[0], m_ref[1] = m_now
        n_ref[0], n_ref[1] = n_now
        for ln in lanes:
            ln["kv"] = jnp.where(same_head, _bdot_tn(ln["ke"], ln["v"]), 0.0)
        if want_out:
            for ln in lanes:
                ln["num"] = _bdot(ln["qk"], _blockdiag(ln["v"], same_head))
                ln["den"] = _seg_sum(ln["den_terms"], same_head)
        for ln in lanes:
            d = ln["d"]
            c_in = c_ref[d]
            if want_out:
                num = ln["num"] + ln["s_inter"] * _bdot(ln["q"], c_in)
                acc_ref[d, pl.ds(ln["r"], CHUNK), :] = num / jnp.maximum(jnp.abs(ln["den"]), ln["floor"])
            c_ref[d] = ln["old"] * c_in + ln["new"] * ln["kv"]

    _scan_lanes(t_ctx // CHUNK, t_lat // CHUNK, multi_step)

    def post(u_ref, acc_ref, y_ref, t):
        for r, n in _post_tiles(t):
            o = _head_rms(acc_ref[0, r:r + n, :] + acc_ref[1, r:r + n, :], same_head, False)
            y_ref[r:r + n, :] = jax.nn.sigmoid(u_ref[r:r + n, 3 * GROUP_W:4 * GROUP_W]) * (o * nw_ref[...])

    post(ul_ref, accl_ref, yl_ref, t_lat)
    if ctx_out:
        post(uc_ref, accc_ref, yc_ref, t_ctx)


def _mlstm_call(u_lat, u_ctx, sm_lat, sm_ctx, gate_b, nw, ctx_out):
    batch, t_lat, _ = u_lat.shape
    t_ctx = u_ctx.shape[1]
    gb = jnp.zeros((SMALL_W,), F32).at[GATE_OFF:GATE_OFF + 4 * N_HEADS].set(gate_b.reshape(-1))[None]
    scratch = ([pltpu.VMEM((2, GROUP_W, GROUP_W), F32), pltpu.VMEM((2, 1, GROUP_W), F32),
                pltpu.VMEM((2, 1, GROUP_W), F32)]
               + [pltpu.VMEM((2, t_lat, GROUP_W), F32), pltpu.VMEM((2, t_ctx, GROUP_W), F32)])
    return _mixer_call(functools.partial(_mlstm_body, ctx_out=ctx_out), "mlstm", batch, t_lat, t_ctx, ctx_out,
                       [u_lat, u_ctx, sm_lat, sm_ctx], [gb, nw], scratch)


def _gla_body(ul_ref, uc_ref, sml_ref, smc_ref, gw_ref, gb_ref, nw_ref, *rest, ctx_out):
    if ctx_out:
        yl_ref, yc_ref, st_ref, accl_ref, accc_ref, gl_ref, gc_ref = rest
    else:
        yl_ref, st_ref, accl_ref, accc_ref, gl_ref, gc_ref = rest
        yc_ref = None
    t_lat, t_ctx = ul_ref.shape[0], uc_ref.shape[0]
    st_ref[...] = jnp.zeros_like(st_ref)
    n_sub = CHUNK // GLA_SUB
    i = lax.broadcasted_iota(I32, (CHUNK, CHUNK), 0)
    j = lax.broadcasted_iota(I32, (CHUNK, CHUNK), 1)
    same = (i // GLA_SUB) == (j // GLA_SUB)
    lcum = ((same & (j <= i)).astype(F32).astype(BF16), (same & (j >= i)).astype(F32).astype(BF16))
    expand = ((lax.broadcasted_iota(I32, (GLA_QK, GROUP_W), 0) // GLA_DK)
              == (lax.broadcasted_iota(I32, (GLA_QK, GROUP_W), 1) // HEAD_DIM)).astype(F32).astype(BF16)
    blockdiag = ((lax.broadcasted_iota(I32, (GROUP_W, GLA_QK), 0) // HEAD_DIM)
                 == (lax.broadcasted_iota(I32, (GROUP_W, GLA_QK), 1) // GLA_DK)).astype(F32)
    sub_row = lax.broadcasted_iota(I32, (CHUNK, 1), 0) % GLA_SUB

    def roll_sub(x, delta):
        return jnp.concatenate([pltpu.roll(x[a * GLA_SUB:(a + 1) * GLA_SUB], delta, 0) for a in range(n_sub)],
                               axis=0)

    def qkv(u_ref, r):
        return (u_ref[pl.ds(r, CHUNK), 0:GLA_QK] * (GLA_DK ** -0.5), u_ref[pl.ds(r, CHUNK), GLA_QK:2 * GLA_QK],
                u_ref[pl.ds(r, CHUNK), 2 * GLA_QK:2 * GLA_QK + GROUP_W])

    def intra(u_ref, sm_ref, g_ref, acc_ref, want_out, c):
        r = pl.multiple_of(c * CHUNK, CHUNK)
        lr = sm_ref[pl.ds(r, CHUNK), LR_OFF:LR_OFF + GLA_RANK]
        g = []
        for d in (0, 1):
            la = _log_sigmoid(_bdot(lr, gw_ref[d]) + gb_ref[d]) / GLA_TAU
            g.append(sum(jnp.dot(lcum[d], piece, preferred_element_type=F32) for piece in _split3(la)))
            g_ref[d, pl.ds(r, CHUNK), :] = g[d]
        if not want_out:
            return
        q, k, v = qkv(u_ref, r)
        prods, vals = [2.0 * q * k], [v]
        for delta in range(1, GLA_SUB):
            fwd_pair = sub_row >= delta
            e = jnp.exp(jnp.where(fwd_pair, g[0] - roll_sub(g[0], delta), g[1] - roll_sub(g[1], delta)))
            prods.append(q * roll_sub(k, delta) * e)
            vals.append(roll_sub(v, delta))
        w = jnp.dot(jnp.concatenate(prods, axis=0).astype(BF16), expand, preferred_element_type=F32)
        acc_ref[pl.ds(r, CHUNK), :] = sum(w[n * CHUNK:(n + 1) * CHUNK] * vals[n] for n in range(GLA_SUB))

    def intra_ctx(c, carry):
        intra(uc_ref, smc_ref, gc_ref, accc_ref, ctx_out, c)
        return carry

    def intra_lat(c, carry):
        intra(ul_ref, sml_ref, gl_ref, accl_ref, True, c)
        return carry

    lax.fori_loop(0, t_ctx // CHUNK, intra_ctx, 0)
    lax.fori_loop(0, t_lat // CHUNK, intra_lat, 0, unroll=4)

    def multi_step(is_lat, chunks):
        u_ref, g_ref, acc_ref = (ul_ref, gl_ref, accl_ref) if is_lat else (uc_ref, gc_ref, accc_ref)
        want_out = is_lat or ctx_out
        lanes = []
        for c, d in chunks:
            r = pl.multiple_of(c * CHUNK, CHUNK)
            q, k, v = qkv(u_ref, r)
            g = g_ref[d, pl.ds(r, CHUNK), :]
            subs = []
            for a in (range(n_sub) if d == 0 else range(n_sub - 1, -1, -1)):
                rows = slice(a * GLA_SUB, (a + 1) * GLA_SUB)
                g_a = g[rows]
                g_end = g_a[GLA_SUB - 1:GLA_SUB] if d == 0 else g_a[0:1]
                subs.append(dict(a=a, decay=jnp.exp(g_end),
                                 kv=_bdot_tn(v[rows], k[rows] * jnp.exp(g_end - g_a)) * blockdiag,
                                 qg=q[rows] * jnp.exp(g_a) if want_out else None))
            lanes.append(dict(d=d, r=r, subs=subs, out=[None] * n_sub))
        for step_lanes in (lanes[n:n + 2] for n in range(0, len(lanes), 2)):
            for n in range(n_sub):
                for ln in step_lanes:
                    sub, st = ln["subs"][n], st_ref[ln["d"]]
                    if want_out:
                        ln["out"][sub["a"]] = _bdot_nt(sub["qg"], st)
                    st_ref[ln["d"]] = st * sub["decay"] + sub["kv"]
        if want_out:
            for ln in lanes:
                acc_ref[pl.ds(ln["r"], CHUNK), :] += jnp.concatenate(ln["out"], axis=0)

    _scan_lanes(t_ctx // CHUNK, t_lat // CHUNK, multi_step)

    def post(u_ref, acc_ref, y_ref, t):
        for r, n in _post_tiles(t):
            o = _head_rms(acc_ref[r:r + n, :], _same_head(), False)
            y_ref[r:r + n, :] = o * nw_ref[...] * _silu(u_ref[r:r + n, 2 * GLA_QK + GROUP_W:2 * GLA_QK + 2 * GROUP_W])

    post(ul_ref, accl_ref, yl_ref, t_lat)
    if ctx_out:
        post(uc_ref, accc_ref, yc_ref, t_ctx)


def _gla_call(u_lat, u_ctx, sm_lat, sm_ctx, gate_w, gate_b, nw, ctx_out):
    batch, t_lat, _ = u_lat.shape
    t_ctx = u_ctx.shape[1]
    scratch = [pltpu.VMEM((2, GROUP_W, GLA_QK), F32),
               pltpu.VMEM((t_lat, GROUP_W), F32), pltpu.VMEM((t_ctx, GROUP_W), F32),
               pltpu.VMEM((2, t_lat, GLA_QK), F32), pltpu.VMEM((2, t_ctx, GLA_QK), F32)]
    return _mixer_call(functools.partial(_gla_body, ctx_out=ctx_out), "gla", batch, t_lat, t_ctx, ctx_out,
                       [u_lat, u_ctx, sm_lat, sm_ctx], [gate_w, gate_b[:, None, :], nw], scratch)


def _out_body(yr_ref, yg_ref, ys_ref, ym_ref, h_ref, mod_ref, wo_ref, n2_ref, rw_ref, h1_ref, hn_ref, lt_ref):
    mod = mod_ref[...]
    proj = sum(_bdot(y_ref[...], wo_ref[n * GROUP_W:(n + 1) * GROUP_W, :])
               for n, y_ref in enumerate((yr_ref, yg_ref, ys_ref, ym_ref)))
    h1 = h_ref[...] + mod[2:3, :] * proj
    h1_ref[...] = h1
    xn = _rms(h1, n2_ref[...]) * (1.0 + mod[4:5, :]) + mod[3:4, :]
    hn = xn.astype(BF16)
    hn_ref[...] = hn
    parts = jnp.dot(hn, rw_ref[...], preferred_element_type=F32).T
    lt_ref[...] = parts[0:N_EXPERTS] + parts[N_EXPERTS:2 * N_EXPERTS] + parts[2 * N_EXPERTS:3 * N_EXPERTS]


def _out_call(ys, h, mod_l, wo, n2w, rw3, is_ctx):
    batch, t, d = h.shape
    tm = min(1024, t)
    y_spec = pl.BlockSpec((None, tm, GROUP_W), lambda b, i: (b, i, 0))
    return pl.pallas_call(
        _out_body,
        grid=(batch, t // tm),
        in_specs=[y_spec, y_spec, y_spec, y_spec,
                  pl.BlockSpec((None, tm, d), lambda b, i: (b, i, 0)),
                  _mod_spec(is_ctx, batch),
                  pl.BlockSpec((d, d), lambda b, i: (0, 0)),
                  pl.BlockSpec((1, d), lambda b, i: (0, 0)),
                  pl.BlockSpec(rw3.shape, lambda b, i: (0, 0))],
        out_specs=[pl.BlockSpec((None, tm, d), lambda b, i: (b, i, 0)),
                   pl.BlockSpec((None, tm, d), lambda b, i: (b, i, 0)),
                   pl.BlockSpec((None, N_EXPERTS, tm), lambda b, i: (b, 0, i))],
        out_shape=[jax.ShapeDtypeStruct((batch, t, d), F32),
                   jax.ShapeDtypeStruct((batch, t, d), BF16),
                   jax.ShapeDtypeStruct((batch, N_EXPERTS, t), F32)],
        compiler_params=_cparams("parallel", "parallel"),
        name="out_proj",
    )(*ys, h, mod_l, wo, n2w, rw3)


def _excl_prefix(flags, t):
    blk = min(256, t)
    below = (lax.broadcasted_iota(I32, (blk, blk), 0) < lax.broadcasted_iota(I32, (blk, blk), 1))
    below = below.astype(F32).astype(BF16)
    outs = []
    carry = jnp.zeros((flags.shape[0], 1), F32)
    for n in range(t // blk):
        f = flags[:, n * blk:(n + 1) * blk]
        outs.append(jnp.dot(f.astype(BF16), below, preferred_element_type=F32) + carry)
        carry = carry + jnp.sum(f, axis=1, keepdims=True)
    return jnp.concatenate(outs, axis=1)


def _route_body(lt_ref, hn_ref, xg_ref, affc_ref, post_ref, pos_s, aff_s, *, cap):
    t = hn_ref.shape[0]
    logits = lt_ref[...]
    ex = jnp.exp(logits - jnp.max(logits, axis=0, keepdims=True))
    aff = ex / jnp.sum(ex, axis=0, keepdims=True)
    def count_at_least(cand):
        return jnp.sum((aff >= pltpu.bitcast(cand, F32)).astype(F32), axis=1, keepdims=True)

    def fix_two_bits(n, thr):
        lo = 28 - 2 * n
        best = thr
        for digit in (1, 2, 3):
            cand = thr | (jnp.int32(digit) << lo)
            best = jnp.where(count_at_least(cand) >= float(cap), cand, best)
        return best

    thr = lax.fori_loop(0, 15, fix_two_bits, jnp.zeros((N_EXPERTS, 1), I32))
    above = (aff >= pltpu.bitcast(thr + 1, F32)).astype(F32)
    tied = (aff >= pltpu.bitcast(thr, F32)).astype(F32) - above
    need = float(cap) - jnp.sum(above, axis=1, keepdims=True)
    sel = above + tied * (_excl_prefix(tied, t) < need).astype(F32)
    pos = jnp.where(sel > 0.0, _excl_prefix(sel, t), -1.0)
    pos_s[...] = pos
    aff_s[...] = aff
    post_ref[...] = jnp.concatenate([pos, jnp.full((SMALL_W - N_EXPERTS, t), -1.0, F32)], axis=0).T
    slots = lax.broadcasted_iota(I32, (cap, t), 0).astype(F32)

    def gather(e, carry):
        hit = slots == pos_s[pl.ds(e, 1), :]
        xg_ref[e] = jnp.dot(hit.astype(BF16), hn_ref[...], preferred_element_type=F32).astype(BF16)
        a = jnp.sum(jnp.where(hit, aff_s[pl.ds(e, 1), :], 0.0), axis=1, keepdims=True)
        affc_ref[e] = jnp.broadcast_to(a, (cap, SMALL_W))
        return carry

    lax.fori_loop(0, N_EXPERTS, gather, 0, unroll=2)


def _route_call(lt, hn, cap):
    batch, t, d = hn.shape
    return pl.pallas_call(
        functools.partial(_route_body, cap=cap),
        grid=(batch,),
        in_specs=[pl.BlockSpec((None, N_EXPERTS, t), lambda b: (b, 0, 0)),
                  pl.BlockSpec((None, t, d), lambda b: (b, 0, 0))],
        out_specs=[pl.BlockSpec((N_EXPERTS, cap, d), lambda b: (0, b, 0)),
                   pl.BlockSpec((N_EXPERTS, cap, SMALL_W), lambda b: (0, b, 0)),
                   pl.BlockSpec((None, t, SMALL_W), lambda b: (b, 0, 0))],
        out_shape=[jax.ShapeDtypeStruct((N_EXPERTS, batch * cap, d), BF16),
                   jax.ShapeDtypeStruct((N_EXPERTS, batch * cap, SMALL_W), F32),
                   jax.ShapeDtypeStruct((batch, t, SMALL_W), F32)],
        scratch_shapes=[pltpu.VMEM((N_EXPERTS, t), F32), pltpu.VMEM((N_EXPERTS, t), F32)],
        compiler_params=_cparams("parallel"),
        name="route_gather",
    )(lt, hn)


FFN_TF = 512
FFN_TM = 512


def _ffn_body(*refs, n_sets):
    ins = refs[:2 * n_sets]
    wg_ref, wu_ref, wd_ref = refs[2 * n_sets:2 * n_sets + 3]
    outs = refs[2 * n_sets + 3:3 * n_sets + 3]
    accs = refs[3 * n_sets + 3:4 * n_sets + 3]
    f = pl.program_id(1)
    dot = functools.partial(jnp.dot, preferred_element_type=F32)
    for s in range(n_sets):
        x_ref, a_ref, y_ref, acc_ref = ins[2 * s], ins[2 * s + 1], outs[s], accs[s]
        rows = x_ref.shape[0]

        @pl.when(f == 0)
        def _():
            acc_ref[...] = jnp.zeros_like(acc_ref)

        for r in range(0, rows, FFN_TM):
            n = min(FFN_TM, rows - r)
            x = x_ref[r:r + n, :]
            hid = _silu(dot(x, wg_ref[...].astype(BF16))) * dot(x, wu_ref[...].astype(BF16))
            acc_ref[r:r + n, :] += dot(hid.astype(BF16), wd_ref[...].astype(BF16))

        @pl.when(f == pl.num_programs(1) - 1)
        def _():
            for r in range(0, rows, FFN_TM):
                n = min(FFN_TM, rows - r)
                y_ref[r:r + n, :] = (acc_ref[r:r + n, :] * a_ref[r:r + n, 0:1]).astype(BF16)


def _ffn_call(sets, wg, wu, wd, layer):
    _, n_exp, d, ff = wg.shape
    in_specs, args, out_specs, out_shape, scratch = [], [], [], [], []
    for xg, affc in sets:
        m = xg.shape[1]
        in_specs += [pl.BlockSpec((None, m, d), lambda e, f: (e, 0, 0)),
                     pl.BlockSpec((None, m, SMALL_W), lambda e, f: (e, 0, 0))]
        args += [xg, affc]
        out_specs.append(pl.BlockSpec((None, m, d), lambda e, f: (e, 0, 0)))
        out_shape.append(jax.ShapeDtypeStruct((n_exp, m, d), BF16))
        scratch.append(pltpu.VMEM((m, d), F32))
    in_specs += [pl.BlockSpec((None, None, d, FFN_TF), lambda e, f: (layer, e, 0, f)),
                 pl.BlockSpec((None, None, d, FFN_TF), lambda e, f: (layer, e, 0, f)),
                 pl.BlockSpec((None, None, FFN_TF, d), lambda e, f: (layer, e, f, 0))]
    return pl.pallas_call(
        functools.partial(_ffn_body, n_sets=len(sets)),
        grid=(n_exp, ff // FFN_TF),
        in_specs=in_specs, out_specs=out_specs, out_shape=out_shape, scratch_shapes=scratch,
        compiler_params=_cparams("parallel", "arbitrary"),
        name="expert_ffn",
    )(*args, wg, wu, wd)


def _scatter_body(post_ref, y_ref, h1_ref, mod_ref, fw_ref, o_ref, *, final):
    n_exp, cap, d = y_ref.shape
    pos = post_ref[...]
    slots = lax.broadcasted_iota(I32, (pos.shape[0], cap), 1).astype(F32)
    hit = jnp.concatenate([(pos[:, e:e+1] == slots).astype(BF16) for e in range(n_exp)], axis=1)
    upd = jnp.dot(hit, y_ref[...].reshape(n_exp * cap, d), preferred_element_type=F32)
    h2 = h1_ref[...] + mod_ref[5:6, :] * upd
    o_ref[...] = _rms(h2, fw_ref[...]) if final else h2


def _scatter_call(post, y, h1, mod_l, is_ctx, final_w):
    batch, t, d = h1.shape
    n_exp = y.shape[0]
    cap = y.shape[1] // batch
    tm = min(512, t)
    final = final_w is not None
    fw = final_w if final else jnp.ones((1, d), F32)
    return pl.pallas_call(
        functools.partial(_scatter_body, final=final),
        grid=(batch, t // tm),
        in_specs=[pl.BlockSpec((None, tm, SMALL_W), lambda b, i: (b, i, 0)),
                  pl.BlockSpec((n_exp, cap, d), lambda b, i: (0, b, 0)),
                  pl.BlockSpec((None, tm, d), lambda b, i: (b, i, 0)),
                  _mod_spec(is_ctx, batch),
                  pl.BlockSpec((1, d), lambda b, i: (0, 0))],
        out_specs=pl.BlockSpec((None, tm, d), lambda b, i: (b, i, 0)),
        out_shape=jax.ShapeDtypeStruct((batch, t, d), F32),
        compiler_params=_cparams("parallel", "parallel"),
        name="scatter_residual",
    )(post, y, h1, mod_l, fw)


def kernel(x, c, ctx, c_ctx, w_mod, b_mod, norm1_w, norm2_w, w_in, w_out, ret_decay_logit, ret_norm_w,
           gla_gate_w, gla_gate_b, gla_norm_w, ssd_conv_w, ssd_conv_b, ssd_dt_bias, ssd_a_log, ssd_d,
           ssd_norm_w, mlstm_gate_b, mlstm_norm_w, router_w, expert_w_gate, expert_w_up, expert_w_down,
           final_norm_w):
    batch, t_lat, d = x.shape
    t_ctx = ctx.shape[1]
    depth = w_mod.shape[0]
    assert batch + 1 <= ROW_PAD and d == D_MODEL
    for t in (t_lat, t_ctx):
        assert t % 128 == 0 and (t // CHUNK) % min(SCAN_STEPS, t // CHUNK) == 0
    assert t_lat % 256 == 0
    cs = jnp.concatenate([c, c_ctx[None], jnp.zeros((ROW_PAD - batch - 1, d), F32)], axis=0)
    mod = _mod_call(cs, w_mod, b_mod).reshape(depth, ROW_PAD, 6, d)
    cos, sin = _rope_tables(t_lat)
    cap_lat = EC_CAPACITY * t_lat // N_EXPERTS
    cap_ctx = EC_CAPACITY * t_ctx // N_EXPERTS
    h_lat, h_ctx = x, ctx
    for l in range(depth):
        ctx_out = l < depth - 1
        mod_l = mod[l]
        w_big = _arrange_w_in(w_in[l])
        ul = _in_call(h_lat, mod_l, norm1_w[l][None], w_big, False)
        uc = _in_call(h_ctx, mod_l, norm1_w[l][None], w_big, True)
        ys = [_ret_call(ul[0], uc[0], jax.nn.log_sigmoid(ret_decay_logit[l].astype(F32)), cos, sin,
                        ret_norm_w[l][None], ctx_out),
              _gla_call(ul[1], uc[1], ul[4], uc[4], gla_gate_w[l], gla_gate_b[l], gla_norm_w[l][None], ctx_out),
              _ssd_call(ul[2], uc[2], ul[4], uc[4], ssd_conv_w[l], ssd_conv_b[l], ssd_dt_bias[l], ssd_a_log[l],
                        ssd_d[l], ssd_norm_w[l][None], ctx_out),
              _mlstm_call(ul[3], uc[3], ul[4], uc[4], mlstm_gate_b[l], mlstm_norm_w[l][None], ctx_out)]
        wo = w_out[l].astype(BF16)
        rw3 = jnp.pad(jnp.concatenate(_split3(router_w[l]), axis=1), ((0, 0), (0, SMALL_W - 3 * N_EXPERTS)))
        n2w = norm2_w[l][None]
        h1_lat, hn_lat, lt_lat = _out_call([y[0] for y in ys], h_lat, mod_l, wo, n2w, rw3, False)
        xg_lat, affc_lat, post_lat = _route_call(lt_lat, hn_lat, cap_lat)
        sets = [(xg_lat, affc_lat)]
        if ctx_out:
            h1_ctx, hn_ctx, lt_ctx = _out_call([y[1] for y in ys], h_ctx, mod_l, wo, n2w, rw3, True)
            xg_ctx, affc_ctx, post_ctx = _route_call(lt_ctx, hn_ctx, cap_ctx)
            sets.append((xg_ctx, affc_ctx))
        y_exp = _ffn_call(sets, expert_w_gate, expert_w_up, expert_w_down, l)
        h_lat = _scatter_call(post_lat, y_exp[0], h1_lat, mod_l, False,
                              None if ctx_out else final_norm_w[None])
        if ctx_out:
            h_ctx = _scatter_call(post_ctx, y_exp[1], h1_ctx, mod_l, True, None)
    return h_lat
```

```python
import functools
import math

import jax
import jax.numpy as jnp
from jax import lax
from jax.experimental import pallas as pl
from jax.experimental.pallas import tpu as pltpu

F32 = jnp.float32
BF16 = jnp.bfloat16
I32 = jnp.int32

D_MODEL = 1024
GRID_W = 64
EPS = 1e-6
ROPE_BASE = 10000.0
GROUP_W = 256
HEAD_DIM = 64
N_HEADS = 4
CHUNK = 64
GLA_DK = 32
GLA_QK = 128
GLA_RANK = 16
GLA_TAU = 16.0
GLA_SUB = 16
SSD_CONV = 5
SSD_BC = 128
N_EXPERTS = 16
EC_CAPACITY = 2
EXPERT_FF = 1536
SMALL_W = 128
LR_OFF, DT_OFF, GATE_OFF = 0, 16, 24
ROW_PAD = 16
VMEM_LIMIT = 56 * 1024 * 1024


def _cparams(*sem):
    return pltpu.CompilerParams(dimension_semantics=sem, vmem_limit_bytes=VMEM_LIMIT)


def _bdot(a, b):
    return jnp.dot(a.astype(BF16), b.astype(BF16), preferred_element_type=F32)


def _bdot_nt(a, b):
    return lax.dot_general(a.astype(BF16), b.astype(BF16), (((1,), (1,)), ((), ())),
                           preferred_element_type=F32)


def _bdot_tn(a, b):
    return lax.dot_general(a.astype(BF16), b.astype(BF16), (((0,), (0,)), ((), ())),
                           preferred_element_type=F32)


def _silu(x):
    return x * jax.nn.sigmoid(x)


def _softplus(x):
    return jnp.maximum(x, 0.0) + jnp.log1p(jnp.exp(-jnp.abs(x)))


def _log_sigmoid(x):
    return -_softplus(-x)


def _rms(x, w):
    return x * lax.rsqrt(jnp.mean(x * x, axis=-1, keepdims=True) + EPS) * w


def _mod_body(cs_ref, w_ref, b_ref, o_ref):
    o_ref[...] = _bdot(_silu(cs_ref[...]), w_ref[...]) + b_ref[...]


def _mod_call(cs, w_mod, b_mod):
    depth, d, n = w_mod.shape
    tn = 1536
    return pl.pallas_call(
        _mod_body,
        grid=(depth, n // tn),
        in_specs=[pl.BlockSpec((ROW_PAD, d), lambda l, j: (0, 0)),
                  pl.BlockSpec((None, d, tn), lambda l, j: (l, 0, j)),
                  pl.BlockSpec((None, 1, tn), lambda l, j: (l, 0, j))],
        out_specs=pl.BlockSpec((None, ROW_PAD, tn), lambda l, j: (l, 0, j)),
        out_shape=jax.ShapeDtypeStruct((depth, ROW_PAD, n), F32),
        compiler_params=_cparams("parallel", "parallel"),
        name="adaln",
    )(cs, w_mod, b_mod.reshape(depth, 1, n))


def _mod_spec(is_ctx, batch):
    if is_ctx:
        return pl.BlockSpec((None, 6, D_MODEL), lambda b, i: (batch, 0, 0))
    return pl.BlockSpec((None, 6, D_MODEL), lambda b, i: (b, 0, 0))


IN_WIDTHS = (1024, 768, 768, 1024, SMALL_W)


def _in_body(h_ref, mod_ref, nw_ref, w_ref, ret_ref, gla_ref, ssd_ref, mls_ref, sm_ref):
    mod = mod_ref[...]
    xn = _rms(h_ref[...], nw_ref[...]) * (1.0 + mod[1:2, :]) + mod[0:1, :]
    u = jnp.dot(xn.astype(BF16), w_ref[...], preferred_element_type=F32)
    off = 0
    for ref, w in zip((ret_ref, gla_ref, ssd_ref, mls_ref, sm_ref), IN_WIDTHS):
        ref[...] = u[:, off:off + w]
        off += w


def _in_call(h, mod_l, nw, w_big, is_ctx):
    batch, t, d = h.shape
    tm = min(512, t)
    n = w_big.shape[1]
    return pl.pallas_call(
        _in_body,
        grid=(batch, t // tm),
        in_specs=[pl.BlockSpec((None, tm, d), lambda b, i: (b, i, 0)),
                  _mod_spec(is_ctx, batch),
                  pl.BlockSpec((1, d), lambda b, i: (0, 0)),
                  pl.BlockSpec((d, n), lambda b, i: (0, 0))],
        out_specs=[pl.BlockSpec((None, tm, w), lambda b, i: (b, i, 0)) for w in IN_WIDTHS],
        out_shape=[jax.ShapeDtypeStruct((batch, t, w), F32) for w in IN_WIDTHS],
        compiler_params=_cparams("parallel", "parallel"),
        name="in_proj",
    )(h, mod_l, nw, w_big)


def _arrange_w_in(w):
    ret_end = 4 * GROUP_W
    gla_main = ret_end + 2 * GLA_QK + 2 * GROUP_W
    gla_end = gla_main + GLA_RANK
    ssd_main = gla_end + GROUP_W + GROUP_W + 2 * SSD_BC
    ssd_end = ssd_main + 2 * N_HEADS
    mls_main = ssd_end + 4 * GROUP_W
    mls_end = mls_main + 4 * N_HEADS
    narrow = (gla_end - gla_main) + (ssd_end - ssd_main) + (mls_end - mls_main)
    pad = jnp.zeros((w.shape[0], SMALL_W - narrow), w.dtype)
    return jnp.concatenate([w[:, :ret_end], w[:, ret_end:gla_main], w[:, gla_end:ssd_main],
                            w[:, ssd_end:mls_main], w[:, gla_main:gla_end], w[:, ssd_main:ssd_end],
                            w[:, mls_main:mls_end], pad], axis=1).astype(BF16)


def _split3(x):
    hi = x.astype(BF16)
    r1 = x - hi.astype(F32)
    mid = r1.astype(BF16)
    return hi, mid, (r1 - mid.astype(F32)).astype(BF16)


def _iota2(shape, axis):
    return lax.broadcasted_iota(I32, shape, axis)


def _block_consts(rev):
    i = _iota2((CHUNK, GROUP_W), 0)
    j = _iota2((CHUNK, GROUP_W), 1) % HEAD_DIM
    return ((j >= i) if rev else (j <= i)), (j == i)


def _same_head():
    return (_iota2((GROUP_W, GROUP_W), 0) // HEAD_DIM) == (_iota2((GROUP_W, GROUP_W), 1) // HEAD_DIM)


def _cum_mat(rev):
    i = _iota2((CHUNK, 3 * CHUNK), 0)
    t = _iota2((CHUNK, 3 * CHUNK), 1) % CHUNK
    return ((t >= i) if rev else (t <= i)).astype(F32).astype(BF16)


def _expand_mat(first_lane, width=GROUP_W):
    return (_iota2((SMALL_W, width), 0) == first_lane + _iota2((SMALL_W, width), 1) // HEAD_DIM
            ).astype(F32).astype(BF16)


def _spread(x, expand):
    n = x.shape[0]
    y = jnp.dot(jnp.concatenate(_split3(x), axis=0), expand, preferred_element_type=F32)
    return (y[0:n] + y[n:2 * n]) + y[2 * n:3 * n]


def _cumsum_b(x_b, cum3):
    return jnp.dot(cum3, jnp.concatenate(_split3(x_b), axis=0), preferred_element_type=F32)


def _tile_heads(x):
    return jnp.concatenate([x] * N_HEADS, axis=0)


def _blockdiag(x, same_head):
    xb = _tile_heads(x.astype(BF16))
    return jnp.where(same_head, xb, jnp.zeros_like(xb))


def _lane_scores(lanes, same_head):
    for ln in lanes:
        if ln["want_out"]:
            ln["scores"] = _bdot_nt(ln["q"], _blockdiag(ln["k"], same_head))


def _decay_lanes(lanes, consts, s_ref):
    same_head = consts[0][2]
    for ln in lanes:
        mask_b, eye_b, _ = consts[ln["d"]]
        g_b = ln["g_b"]
        ln["g_tot"] = g_b[0:1] if ln["d"] == 1 else g_b[CHUNK - 1:CHUNK]
        if ln["want_out"]:
            g_row = jnp.sum(jnp.where(eye_b, g_b, 0.0), axis=0, keepdims=True)
            ln["p"] = ln["scores"] * jnp.exp(jnp.where(mask_b, g_b - g_row, -jnp.inf))
            ln["qg"] = ln["q"] * jnp.exp(g_b)
        ln["khat"] = ln["k"] * jnp.exp(ln["g_tot"] - g_b)
    for ln in lanes:
        ln["kv"] = jnp.where(same_head, _bdot_tn(ln["khat"], ln["v"]), 0.0)
    for ln in lanes:
        if ln["want_out"]:
            ln["pv"] = _bdot(ln["p"], _blockdiag(ln["v"], same_head))
    outs = []
    for ln in lanes:
        s_in = s_ref[ln["d"]]
        outs.append(ln["pv"] + _bdot(ln["qg"], s_in) if ln["want_out"] else None)
        s_ref[ln["d"]] = jnp.exp(ln["g_tot"]) * s_in + ln["kv"]
    return outs


def _seg_sum(x, same_head):
    seg = same_head.astype(F32).astype(BF16)
    hi = x.astype(BF16)
    lo = (x - hi.astype(F32)).astype(BF16)
    return jnp.dot(hi, seg, preferred_element_type=F32) + jnp.dot(lo, seg, preferred_element_type=F32)


def _head_rms(x, same_head, center):
    if center:
        x = x - _seg_sum(x, same_head) * (1.0 / HEAD_DIM)
    return x * lax.rsqrt(_seg_sum(x * x, same_head) * (1.0 / HEAD_DIM) + EPS)


SCAN_STEPS = 8


def _scan_lanes(n_ctx, n_lat, multi_step):
    def run(is_lat, n):
        steps = min(SCAN_STEPS, n)

        def it(s, carry):
            lanes = []
            for j in range(steps):
                lanes += [(steps * s + j, 0), (n - 1 - steps * s - j, 1)]
            multi_step(is_lat, lanes)
            return carry

        lax.fori_loop(0, n // steps, it, 0)

    run(False, n_ctx)
    run(True, n_lat)


def _post_tiles(t):
    tile = min(256, t)
    return [(r, tile) for r in range(0, t, tile)]


def _mixer_call(body, name, batch, t_lat, t_ctx, ctx_out, per_sample, shared, scratch):
    in_specs = [pl.BlockSpec((None,) + a.shape[1:], lambda b, nd=a.ndim: (b,) + (0,) * (nd - 1))
                for a in per_sample]
    in_specs += [pl.BlockSpec(a.shape, lambda b, nd=a.ndim: (0,) * nd) for a in shared]
    args = list(per_sample) + list(shared)
    out_shape = [jax.ShapeDtypeStruct((batch, t_lat, GROUP_W), F32)]
    out_specs = [pl.BlockSpec((None, t_lat, GROUP_W), lambda b: (b, 0, 0))]
    if ctx_out:
        out_shape.append(jax.ShapeDtypeStruct((batch, t_ctx, GROUP_W), F32))
        out_specs.append(pl.BlockSpec((None, t_ctx, GROUP_W), lambda b: (b, 0, 0)))
    return pl.pallas_call(
        body, grid=(batch,), in_specs=in_specs, out_specs=out_specs, out_shape=out_shape,
        scratch_shapes=scratch, compiler_params=_cparams("parallel"), name=name,
    )(*args)


def _rope(x, cos, sin_signed, first_half):
    return x * cos + jnp.where(first_half, pltpu.roll(x, x.shape[1] - 16, 1), pltpu.roll(x, 16, 1)) * sin_signed


def _scan_consts():
    same_head = _same_head()
    return tuple(_block_consts(rev) + (same_head,) for rev in (False, True))


def _ret_body(ul_ref, uc_ref, lg_ref, cos_ref, sin_ref, nw_ref, *rest, ctx_out):
    if ctx_out:
        yl_ref, yc_ref, s_ref, accl_ref, accc_ref = rest
    else:
        yl_ref, s_ref, accl_ref, accc_ref = rest
        yc_ref = None
    t_lat, t_ctx = ul_ref.shape[0], uc_ref.shape[0]
    s_ref[...] = jnp.zeros_like(s_ref)
    first_half = (_iota2((1, GROUP_W), 1) % 32) < 16
    consts = _scan_consts()
    row = _iota2((CHUNK, GROUP_W), 0).astype(F32)
    n_decays = (row + 1.0, CHUNK - row)

    def multi_step(is_lat, chunks):
        u_ref, acc_ref = (ul_ref, accl_ref) if is_lat else (uc_ref, accc_ref)
        lanes = []
        for c, d in chunks:
            r = pl.multiple_of(c * CHUNK, CHUNK)
            q = u_ref[pl.ds(r, CHUNK), 0:GROUP_W] * (HEAD_DIM ** -0.5)
            k = u_ref[pl.ds(r, CHUNK), GROUP_W:2 * GROUP_W]
            if is_lat:
                cos, sin = cos_ref[pl.ds(r, CHUNK), :], sin_ref[pl.ds(r, CHUNK), :]
                q, k = _rope(q, cos, sin, first_half), _rope(k, cos, sin, first_half)
            lanes.append(dict(d=d, r=r, want_out=is_lat or ctx_out, q=q, k=k,
                              v=u_ref[pl.ds(r, CHUNK), 2 * GROUP_W:3 * GROUP_W], g_b=n_decays[d] * lg_ref[d]))
        _lane_scores(lanes, consts[0][2])
        for ln, o in zip(lanes, _decay_lanes(lanes, consts, s_ref)):
            if ln["want_out"]:
                acc_ref[ln["d"], pl.ds(ln["r"], CHUNK), :] = o

    _scan_lanes(t_ctx // CHUNK, t_lat // CHUNK, multi_step)

    def post(u_ref, acc_ref, y_ref, t):
        for r, n in _post_tiles(t):
            o = _head_rms(acc_ref[0, r:r + n, :] + acc_ref[1, r:r + n, :], consts[0][2], True)
            y_ref[r:r + n, :] = o * nw_ref[...] * _silu(u_ref[r:r + n, 3 * GROUP_W:4 * GROUP_W])

    post(ul_ref, accl_ref, yl_ref, t_lat)
    if ctx_out:
        post(uc_ref, accc_ref, yc_ref, t_ctx)


def _state_scratch(n, t_lat, t_ctx):
    return ([pltpu.VMEM((2, GROUP_W, GROUP_W), F32)] * n
            + [pltpu.VMEM((2, t_lat, GROUP_W), F32), pltpu.VMEM((2, t_ctx, GROUP_W), F32)])


def _ret_call(u_lat, u_ctx, log_gamma, cos, sin, nw, ctx_out):
    batch, t_lat, _ = u_lat.shape
    t_ctx = u_ctx.shape[1]
    lg = jnp.repeat(log_gamma, HEAD_DIM, axis=1)[:, None, :]
    return _mixer_call(functools.partial(_ret_body, ctx_out=ctx_out), "retention", batch, t_lat, t_ctx, ctx_out,
                       [u_lat, u_ctx], [lg, cos, sin, nw], _state_scratch(1, t_lat, t_ctx))


def _rope_tables(t_lat):
    nf = 16
    inv = ROPE_BASE ** (-jnp.arange(nf, dtype=F32) / nf)
    rows = jnp.repeat(jnp.arange(t_lat // GRID_W), GRID_W).astype(F32)
    cols = jnp.tile(jnp.arange(GRID_W), t_lat // GRID_W).astype(F32)
    ang_r = rows[:, None] * inv[None, :]
    ang_c = cols[:, None] * inv[None, :]
    cos_h = jnp.concatenate([jnp.cos(ang_r)] * 2 + [jnp.cos(ang_c)] * 2, axis=1)
    sin_h = jnp.concatenate([-jnp.sin(ang_r), jnp.sin(ang_r), -jnp.sin(ang_c), jnp.sin(ang_c)], axis=1)
    return jnp.tile(cos_h, (1, N_HEADS)), jnp.tile(sin_h, (1, N_HEADS))


def _ssd_body(ul_ref, uc_ref, sml_ref, smc_ref, cw_ref, cb_ref, dtb_ref, an_ref, dsk_ref, nw_ref, *rest,
              ctx_out):
    if ctx_out:
        yl_ref, yc_ref, s_ref, accl_ref, accc_ref, xl_ref, xc_ref, pad_ref = rest
    else:
        yl_ref, s_ref, accl_ref, accc_ref, xl_ref, xc_ref, pad_ref = rest
        yc_ref = None
    t_lat, t_ctx = ul_ref.shape[0], uc_ref.shape[0]
    conv_ch = GROUP_W + 2 * SSD_BC
    half = (SSD_CONV - 1) // 2
    halo = 8

    def conv(u_ref, x_ref, t):
        pad_ref[0:halo, :] = jnp.zeros((halo, conv_ch), F32)
        pad_ref[halo + t:2 * halo + t, :] = jnp.zeros((halo, conv_ch), F32)
        for r, n in _post_tiles(t):
            pad_ref[halo + r:halo + r + n, :] = u_ref[r:r + n, GROUP_W:GROUP_W + conv_ch]
        for r, n in _post_tiles(t):
            acc = jnp.zeros((n, conv_ch), F32) + cb_ref[...]
            for tap in range(SSD_CONV):
                lo = halo + r + tap - half
                acc = acc + pad_ref[lo:lo + n, :] * cw_ref[tap:tap + 1, :]
            act = _silu(acc)
            grp = [act[:, GROUP_W + m * HEAD_DIM:GROUP_W + (m + 1) * HEAD_DIM] for m in range(4)]
            x_ref[r:r + n, 0:GROUP_W] = act[:, 0:GROUP_W]
            x_ref[r:r + n, GROUP_W:2 * GROUP_W] = jnp.concatenate([grp[0], grp[0], grp[1], grp[1]], axis=1)
            x_ref[r:r + n, 2 * GROUP_W:3 * GROUP_W] = jnp.concatenate([grp[2], grp[2], grp[3], grp[3]], axis=1)

    conv(uc_ref, xc_ref, t_ctx)
    conv(ul_ref, xl_ref, t_lat)
    s_ref[...] = jnp.zeros_like(s_ref)
    consts = _scan_consts()
    cum3 = (_cum_mat(False), _cum_mat(True))
    expand = (_expand_mat(DT_OFF), _expand_mat(DT_OFF + N_HEADS))

    def multi_step(is_lat, chunks):
        x_ref, sm_ref, acc_ref = (xl_ref, sml_ref, accl_ref) if is_lat else (xc_ref, smc_ref, accc_ref)
        lanes = []
        for c, d in chunks:
            r = pl.multiple_of(c * CHUNK, CHUNK)
            dt = _spread(_softplus(sm_ref[pl.ds(r, CHUNK), :] + dtb_ref[...]), expand[d])
            lanes.append(dict(d=d, r=r, want_out=is_lat or ctx_out, dt=dt,
                              k=x_ref[pl.ds(r, CHUNK), GROUP_W:2 * GROUP_W],
                              q=x_ref[pl.ds(r, CHUNK), 2 * GROUP_W:3 * GROUP_W]))
        _lane_scores(lanes, consts[0][2])
        for ln in lanes:
            ln["g_b"] = _cumsum_b(ln["dt"] * an_ref[ln["d"]], cum3[ln["d"]])
            ln["v"] = x_ref[pl.ds(ln["r"], CHUNK), 0:GROUP_W] * ln["dt"]
        for ln, o in zip(lanes, _decay_lanes(lanes, consts, s_ref)):
            if ln["want_out"]:
                acc_ref[ln["d"], pl.ds(ln["r"], CHUNK), :] = o

    _scan_lanes(t_ctx // CHUNK, t_lat // CHUNK, multi_step)

    def post(u_ref, x_ref, acc_ref, y_ref, t):
        for r, n in _post_tiles(t):
            y = acc_ref[0, r:r + n, :] + acc_ref[1, r:r + n, :] + dsk_ref[...] * x_ref[r:r + n, 0:GROUP_W]
            y_ref[r:r + n, :] = _rms(y * _silu(u_ref[r:r + n, 0:GROUP_W]), nw_ref[...])

    post(ul_ref, xl_ref, accl_ref, yl_ref, t_lat)
    if ctx_out:
        post(uc_ref, xc_ref, accc_ref, yc_ref, t_ctx)


def _ssd_call(u_lat, u_ctx, sm_lat, sm_ctx, conv_w, conv_b, dt_bias, a_log, d_skip, nw, ctx_out):
    batch, t_lat, _ = u_lat.shape
    t_ctx = u_ctx.shape[1]
    conv_ch = GROUP_W + 2 * SSD_BC
    dtb = jnp.zeros((SMALL_W,), F32).at[DT_OFF:DT_OFF + 2 * N_HEADS].set(dt_bias.reshape(-1))[None]
    a_neg = jnp.repeat(-jnp.exp(a_log.astype(F32)), HEAD_DIM, axis=1)[:, None, :]
    cw = jnp.concatenate([conv_w, jnp.zeros((8 - SSD_CONV, conv_ch), F32)], axis=0)
    dsk = jnp.repeat(d_skip, HEAD_DIM)[None]
    scratch = _state_scratch(1, t_lat, t_ctx) + [
        pltpu.VMEM((t_lat, 3 * GROUP_W), F32), pltpu.VMEM((t_ctx, 3 * GROUP_W), F32),
        pltpu.VMEM((t_lat + 16, conv_ch), F32)]
    return _mixer_call(functools.partial(_ssd_body, ctx_out=ctx_out), "ssd", batch, t_lat, t_ctx, ctx_out,
                       [u_lat, u_ctx, sm_lat, sm_ctx], [cw, conv_b[None], dtb, a_neg, dsk, nw], scratch)


def _mlstm_body(ul_ref, uc_ref, sml_ref, smc_ref, gb_ref, nw_ref, *rest, ctx_out):
    if ctx_out:
        yl_ref, yc_ref, c_ref, n_ref, m_ref, accl_ref, accc_ref = rest
    else:
        yl_ref, c_ref, n_ref, m_ref, accl_ref, accc_ref = rest
        yc_ref = None
    t_lat, t_ctx = ul_ref.shape[0], uc_ref.shape[0]
    c_ref[...] = jnp.zeros_like(c_ref)
    n_ref[...] = jnp.zeros_like(n_ref)
    m_ref[...] = jnp.zeros_like(m_ref)
    consts = _scan_consts()
    same_head = consts[0][2]
    seg = same_head.astype(F32).astype(BF16)
    cum3 = (_cum_mat(False), _cum_mat(True))
    f_lane = tuple(GATE_OFF + 2 * N_HEADS * d + N_HEADS for d in (0, 1))
    expand_fi = tuple(jnp.concatenate([_expand_mat(f_lane[d]), _expand_mat(f_lane[d] - N_HEADS)], axis=1)
                      for d in (0, 1))
    lane = _iota2((1, SMALL_W), 1)
    is_f = (lane >= GATE_OFF) & ((lane - GATE_OFF) % (2 * N_HEADS) >= N_HEADS)

    def running_max(x, rev):
        row = _iota2(x.shape, 0)
        s = 1
        while s < CHUNK:
            shifted = pltpu.roll(x, CHUNK - s if rev else s, 0)
            valid = (row < CHUNK - s) if rev else (row >= s)
            x = jnp.maximum(x, jnp.where(valid, shifted, -jnp.inf))
            s *= 2
        return x

    def multi_step(is_lat, chunks):
        u_ref, sm_ref, acc_ref = (ul_ref, sml_ref, accl_ref) if is_lat else (uc_ref, smc_ref, accc_ref)
        want_out = is_lat or ctx_out
        lanes = []
        for c, d in chunks:
            r = pl.multiple_of(c * CHUNK, CHUNK)
            gates = sm_ref[pl.ds(r, CHUNK), :] + gb_ref[...]
            both = _spread(jnp.where(is_f, _log_sigmoid(gates), gates), expand_fi[d])
            lanes.append(dict(d=d, r=r, want_out=want_out, lf_b=both[:, 0:GROUP_W], ig_b=both[:, GROUP_W:2 * GROUP_W],
                              q=u_ref[pl.ds(r, CHUNK), 0:GROUP_W],
                              k=u_ref[pl.ds(r, CHUNK), GROUP_W:2 * GROUP_W] * (HEAD_DIM ** -0.5),
                              v=u_ref[pl.ds(r, CHUNK), 2 * GROUP_W:3 * GROUP_W]))
        _lane_scores(lanes, same_head)
        for ln in lanes:
            ln["g_b"] = _cumsum_b(ln["lf_b"], cum3[ln["d"]])
        m_now, n_now = [m_ref[0], m_ref[1]], [n_ref[0], n_ref[1]]
        for ln in lanes:
            d, g_b, ig_b = ln["d"], ln["g_b"], ln["ig_b"]
            mask_b, eye_b, _ = consts[d]
            g_tot = g_b[0:1] if d == 1 else g_b[CHUNK - 1:CHUNK]
            w_end = g_tot - g_b + ig_b
            b_max = jnp.max(w_end, axis=0, keepdims=True)
            ln["ke"] = ln["k"] * jnp.exp(w_end - b_max)
            m_in, n_in = m_now[d], n_now[d]
            m_new = jnp.maximum(g_tot + m_in, b_max)
            ln["old"], ln["new"] = jnp.exp(g_tot + m_in - m_new), jnp.exp(b_max - m_new)
            m_now[d] = m_new
            n_now[d] = ln["old"] * n_in + ln["new"] * jnp.sum(ln["ke"], axis=0, keepdims=True)
            if want_out:
                m_inter = g_b + m_in
                m_tot = jnp.maximum(m_inter, g_b + running_max(ig_b - g_b, d == 1))
                g_row = jnp.sum(jnp.where(eye_b, g_b, 0.0), axis=0, keepdims=True)
                ig_row = jnp.sum(jnp.where(eye_b, ig_b, 0.0), axis=0, keepdims=True)
                d_log = jnp.where(mask_b, g_b - g_row + ig_row, -jnp.inf)
                ln["qk"] = ln["scores"] * jnp.exp(d_log - m_tot)
                ln["s_inter"] = jnp.exp(m_inter - m_tot)
                ln["floor"] = jnp.exp(-m_tot)
                ln["den_terms"] = ln["qk"] + ln["s_inter"] * (ln["q"] * n_in)
        m_ref[0], m_ref[1] = m_now
        n_ref[0], n_ref[1] = n_now
        for ln in lanes:
            ln["kv"] = jnp.where(same_head, _bdot_tn(ln["ke"], ln["v"]), 0.0)
        if want_out:
            for ln in lanes:
                ln["num"] = _bdot(ln["qk"], _blockdiag(ln["v"], same_head))
                ln["den"] = _seg_sum(ln["den_terms"], same_head)
        for ln in lanes:
            d = ln["d"]
            c_in = c_ref[d]
            if want_out:
                num = ln["num"] + ln["s_inter"] * _bdot(ln["q"], c_in)
                acc_ref[d, pl.ds(ln["r"], CHUNK), :] = num / jnp.maximum(jnp.abs(ln["den"]), ln["floor"])
            c_ref[d] = ln["old"] * c_in + ln["new"] * ln["kv"]

    _scan_lanes(t_ctx // CHUNK, t_lat // CHUNK, multi_step)

    def post(u_ref, acc_ref, y_ref, t):
        for r, n in _post_tiles(t):
            o = _head_rms(acc_ref[0, r:r + n, :] + acc_ref[1, r:r + n, :], same_head, False)
            y_ref[r:r + n, :] = jax.nn.sigmoid(u_ref[r:r + n, 3 * GROUP_W:4 * GROUP_W]) * (o * nw_ref[...])

    post(ul_ref, accl_ref, yl_ref, t_lat)
    if ctx_out:
        post(uc_ref, accc_ref, yc_ref, t_ctx)


def _mlstm_call(u_lat, u_ctx, sm_lat, sm_ctx, gate_b, nw, ctx_out):
    batch, t_lat, _ = u_lat.shape
    t_ctx = u_ctx.shape[1]
    gb = jnp.zeros((SMALL_W,), F32).at[GATE_OFF:GATE_OFF + 4 * N_HEADS].set(gate_b.reshape(-1))[None]
    scratch = ([pltpu.VMEM((2, GROUP_W, GROUP_W), F32), pltpu.VMEM((2, 1, GROUP_W), F32),
                pltpu.VMEM((2, 1, GROUP_W), F32)]
               + [pltpu.VMEM((2, t_lat, GROUP_W), F32), pltpu.VMEM((2, t_ctx, GROUP_W), F32)])
    return _mixer_call(functools.partial(_mlstm_body, ctx_out=ctx_out), "mlstm", batch, t_lat, t_ctx, ctx_out,
                       [u_lat, u_ctx, sm_lat, sm_ctx], [gb, nw], scratch)


def _gla_body(ul_ref, uc_ref, sml_ref, smc_ref, gw_ref, gb_ref, nw_ref, *rest, ctx_out):
    if ctx_out:
        yl_ref, yc_ref, st_ref, accl_ref, accc_ref, gl_ref, gc_ref = rest
    else:
        yl_ref, st_ref, accl_ref, accc_ref, gl_ref, gc_ref = rest
        yc_ref = None
    t_lat, t_ctx = ul_ref.shape[0], uc_ref.shape[0]
    st_ref[...] = jnp.zeros_like(st_ref)
    n_sub = CHUNK // GLA_SUB
    i = lax.broadcasted_iota(I32, (CHUNK, CHUNK), 0)
    j = lax.broadcasted_iota(I32, (CHUNK, CHUNK), 1)
    same = (i // GLA_SUB) == (j // GLA_SUB)
    lcum = ((same & (j <= i)).astype(F32).astype(BF16), (same & (j >= i)).astype(F32).astype(BF16))
    expand = ((lax.broadcasted_iota(I32, (GLA_QK, GROUP_W), 0) // GLA_DK)
              == (lax.broadcasted_iota(I32, (GLA_QK, GROUP_W), 1) // HEAD_DIM)).astype(F32).astype(BF16)
    blockdiag = ((lax.broadcasted_iota(I32, (GROUP_W, GLA_QK), 0) // HEAD_DIM)
                 == (lax.broadcasted_iota(I32, (GROUP_W, GLA_QK), 1) // GLA_DK)).astype(F32)
    sub_row = lax.broadcasted_iota(I32, (CHUNK, 1), 0) % GLA_SUB

    def roll_sub(x, delta):
        return jnp.concatenate([pltpu.roll(x[a * GLA_SUB:(a + 1) * GLA_SUB], delta, 0) for a in range(n_sub)],
                               axis=0)

    def qkv(u_ref, r):
        return (u_ref[pl.ds(r, CHUNK), 0:GLA_QK] * (GLA_DK ** -0.5), u_ref[pl.ds(r, CHUNK), GLA_QK:2 * GLA_QK],
                u_ref[pl.ds(r, CHUNK), 2 * GLA_QK:2 * GLA_QK + GROUP_W])

    def intra(u_ref, sm_ref, g_ref, acc_ref, want_out, c):
        r = pl.multiple_of(c * CHUNK, CHUNK)
        lr = sm_ref[pl.ds(r, CHUNK), LR_OFF:LR_OFF + GLA_RANK]
        g = []
        for d in (0, 1):
            la = _log_sigmoid(_bdot(lr, gw_ref[d]) + gb_ref[d]) / GLA_TAU
            g.append(sum(jnp.dot(lcum[d], piece, preferred_element_type=F32) for piece in _split3(la)))
            g_ref[d, pl.ds(r, CHUNK), :] = g[d]
        if not want_out:
            return
        q, k, v = qkv(u_ref, r)
        prods, vals = [2.0 * q * k], [v]
        for delta in range(1, GLA_SUB):
            fwd_pair = sub_row >= delta
            e = jnp.exp(jnp.where(fwd_pair, g[0] - roll_sub(g[0], delta), g[1] - roll_sub(g[1], delta)))
            prods.append(q * roll_sub(k, delta) * e)
            vals.append(roll_sub(v, delta))
        w = jnp.dot(jnp.concatenate(prods, axis=0).astype(BF16), expand, preferred_element_type=F32)
        acc_ref[pl.ds(r, CHUNK), :] = sum(w[n * CHUNK:(n + 1) * CHUNK] * vals[n] for n in range(GLA_SUB))

    def intra_ctx(c, carry):
        intra(uc_ref, smc_ref, gc_ref, accc_ref, ctx_out, c)
        return carry

    def intra_lat(c, carry):
        intra(ul_ref, sml_ref, gl_ref, accl_ref, True, c)
        return carry

    lax.fori_loop(0, t_ctx // CHUNK, intra_ctx, 0)
    lax.fori_loop(0, t_lat // CHUNK, intra_lat, 0, unroll=4)

    def multi_step(is_lat, chunks):
        u_ref, g_ref, acc_ref = (ul_ref, gl_ref, accl_ref) if is_lat else (uc_ref, gc_ref, accc_ref)
        want_out = is_lat or ctx_out
        lanes = []
        for c, d in chunks:
            r = pl.multiple_of(c * CHUNK, CHUNK)
            q, k, v = qkv(u_ref, r)
            g = g_ref[d, pl.ds(r, CHUNK), :]
            subs = []
            for a in (range(n_sub) if d == 0 else range(n_sub - 1, -1, -1)):
                rows = slice(a * GLA_SUB, (a + 1) * GLA_SUB)
                g_a = g[rows]
                g_end = g_a[GLA_SUB - 1:GLA_SUB] if d == 0 else g_a[0:1]
                subs.append(dict(a=a, decay=jnp.exp(g_end),
                                 kv=_bdot_tn(v[rows], k[rows] * jnp.exp(g_end - g_a)) * blockdiag,
                                 qg=q[rows] * jnp.exp(g_a) if want_out else None))
            lanes.append(dict(d=d, r=r, subs=subs, out=[None] * n_sub))
        for step_lanes in (lanes[n:n + 2] for n in range(0, len(lanes), 2)):
            for n in range(n_sub):
                for ln in step_lanes:
                    sub, st = ln["subs"][n], st_ref[ln["d"]]
                    if want_out:
                        ln["out"][sub["a"]] = _bdot_nt(sub["qg"], st)
                    st_ref[ln["d"]] = st * sub["decay"] + sub["kv"]
        if want_out:
            for ln in lanes:
                acc_ref[pl.ds(ln["r"], CHUNK), :] += jnp.concatenate(ln["out"], axis=0)

    _scan_lanes(t_ctx // CHUNK, t_lat // CHUNK, multi_step)

    def post(u_ref, acc_ref, y_ref, t):
        for r, n in _post_tiles(t):
            o = _head_rms(acc_ref[r:r + n, :], _same_head(), False)
            y_ref[r:r + n, :] = o * nw_ref[...] * _silu(u_ref[r:r + n, 2 * GLA_QK + GROUP_W:2 * GLA_QK + 2 * GROUP_W])

    post(ul_ref, accl_ref, yl_ref, t_lat)
    if ctx_out:
        post(uc_ref, accc_ref, yc_ref, t_ctx)


def _gla_call(u_lat, u_ctx, sm_lat, sm_ctx, gate_w, gate_b, nw, ctx_out):
    batch, t_lat, _ = u_lat.shape
    t_ctx = u_ctx.shape[1]
    scratch = [pltpu.VMEM((2, GROUP_W, GLA_QK), F32),
               pltpu.VMEM((t_lat, GROUP_W), F32), pltpu.VMEM((t_ctx, GROUP_W), F32),
               pltpu.VMEM((2, t_lat, GLA_QK), F32), pltpu.VMEM((2, t_ctx, GLA_QK), F32)]
    return _mixer_call(functools.partial(_gla_body, ctx_out=ctx_out), "gla", batch, t_lat, t_ctx, ctx_out,
                       [u_lat, u_ctx, sm_lat, sm_ctx], [gate_w, gate_b[:, None, :], nw], scratch)


def _out_body(yr_ref, yg_ref, ys_ref, ym_ref, h_ref, mod_ref, wo_ref, n2_ref, rw_ref, h1_ref, hn_ref, lt_ref):
    mod = mod_ref[...]
    proj = sum(_bdot(y_ref[...], wo_ref[n * GROUP_W:(n + 1) * GROUP_W, :])
               for n, y_ref in enumerate((yr_ref, yg_ref, ys_ref, ym_ref)))
    h1 = h_ref[...] + mod[2:3, :] * proj
    h1_ref[...] = h1
    xn = _rms(h1, n2_ref[...]) * (1.0 + mod[4:5, :]) + mod[3:4, :]
    hn = xn.astype(BF16)
    hn_ref[...] = hn
    parts = jnp.dot(hn, rw_ref[...], preferred_element_type=F32).T
    lt_ref[...] = parts[0:N_EXPERTS] + parts[N_EXPERTS:2 * N_EXPERTS] + parts[2 * N_EXPERTS:3 * N_EXPERTS]


def _out_call(ys, h, mod_l, wo, n2w, rw3, is_ctx):
    batch, t, d = h.shape
    tm = min(1024, t)
    y_spec = pl.BlockSpec((None, tm, GROUP_W), lambda b, i: (b, i, 0))
    return pl.pallas_call(
        _out_body,
        grid=(batch, t // tm),
        in_specs=[y_spec, y_spec, y_spec, y_spec,
                  pl.BlockSpec((None, tm, d), lambda b, i: (b, i, 0)),
                  _mod_spec(is_ctx, batch),
                  pl.BlockSpec((d, d), lambda b, i: (0, 0)),
                  pl.BlockSpec((1, d), lambda b, i: (0, 0)),
                  pl.BlockSpec(rw3.shape, lambda b, i: (0, 0))],
        out_specs=[pl.BlockSpec((None, tm, d), lambda b, i: (b, i, 0)),
                   pl.BlockSpec((None, tm, d), lambda b, i: (b, i, 0)),
                   pl.BlockSpec((None, N_EXPERTS, tm), lambda b, i: (b, 0, i))],
        out_shape=[jax.ShapeDtypeStruct((batch, t, d), F32),
                   jax.ShapeDtypeStruct((batch, t, d), BF16),
                   jax.ShapeDtypeStruct((batch, N_EXPERTS, t), F32)],
        compiler_params=_cparams("parallel", "parallel"),
        name="out_proj",
    )(*ys, h, mod_l, wo, n2w, rw3)


def _excl_prefix(flags, t):
    blk = min(256, t)
    below = (lax.broadcasted_iota(I32, (blk, blk), 0) < lax.broadcasted_iota(I32, (blk, blk), 1))
    below = below.astype(F32).astype(BF16)
    outs = []
    carry = jnp.zeros((flags.shape[0], 1), F32)
    for n in range(t // blk):
        f = flags[:, n * blk:(n + 1) * blk]
        outs.append(jnp.dot(f.astype(BF16), below, preferred_element_type=F32) + carry)
        carry = carry + jnp.sum(f, axis=1, keepdims=True)
    return jnp.concatenate(outs, axis=1)


def _route_body(lt_ref, hn_ref, xg_ref, affc_ref, post_ref, pos_s, aff_s, *, cap):
    t = hn_ref.shape[0]
    logits = lt_ref[...]
    ex = jnp.exp(logits - jnp.max(logits, axis=0, keepdims=True))
    aff = ex / jnp.sum(ex, axis=0, keepdims=True)
    def count_at_least(cand):
        return jnp.sum((aff >= pltpu.bitcast(cand, F32)).astype(F32), axis=1, keepdims=True)

    def fix_two_bits(n, thr):
        lo = 28 - 2 * n
        best = thr
        for digit in (1, 2, 3):
            cand = thr | (jnp.int32(digit) << lo)
            best = jnp.where(count_at_least(cand) >= float(cap), cand, best)
        return best

    thr = lax.fori_loop(0, 15, fix_two_bits, jnp.zeros((N_EXPERTS, 1), I32))
    above = (aff >= pltpu.bitcast(thr + 1, F32)).astype(F32)
    tied = (aff >= pltpu.bitcast(thr, F32)).astype(F32) - above
    need = float(cap) - jnp.sum(above, axis=1, keepdims=True)
    sel = above + tied * (_excl_prefix(tied, t) < need).astype(F32)
    pos = jnp.where(sel > 0.0, _excl_prefix(sel, t), -1.0)
    pos_s[...] = pos
    aff_s[...] = aff
    post_ref[...] = jnp.concatenate([pos, jnp.full((SMALL_W - N_EXPERTS, t), -1.0, F32)], axis=0).T
    slots = lax.broadcasted_iota(I32, (cap, t), 0).astype(F32)

    def gather(e, carry):
        hit = slots == pos_s[pl.ds(e, 1), :]
        xg_ref[e] = jnp.dot(hit.astype(BF16), hn_ref[...], preferred_element_type=F32).astype(BF16)
        a = jnp.sum(jnp.where(hit, aff_s[pl.ds(e, 1), :], 0.0), axis=1, keepdims=True)
        affc_ref[e] = jnp.broadcast_to(a, (cap, SMALL_W))
        return carry

    lax.fori_loop(0, N_EXPERTS, gather, 0, unroll=2)


def _route_call(lt, hn, cap):
    batch, t, d = hn.shape
    return pl.pallas_call(
        functools.partial(_route_body, cap=cap),
        grid=(batch,),
        in_specs=[pl.BlockSpec((None, N_EXPERTS, t), lambda b: (b, 0, 0)),
                  pl.BlockSpec((None, t, d), lambda b: (b, 0, 0))],
        out_specs=[pl.BlockSpec((N_EXPERTS, cap, d), lambda b: (0, b, 0)),
                   pl.BlockSpec((N_EXPERTS, cap, SMALL_W), lambda b: (0, b, 0)),
                   pl.BlockSpec((None, t, SMALL_W), lambda b: (b, 0, 0))],
        out_shape=[jax.ShapeDtypeStruct((N_EXPERTS, batch * cap, d), BF16),
                   jax.ShapeDtypeStruct((N_EXPERTS, batch * cap, SMALL_W), F32),
                   jax.ShapeDtypeStruct((batch, t, SMALL_W), F32)],
        scratch_shapes=[pltpu.VMEM((N_EXPERTS, t), F32), pltpu.VMEM((N_EXPERTS, t), F32)],
        compiler_params=_cparams("parallel"),
        name="route_gather",
    )(lt, hn)


FFN_TF = 768
FFN_TM = 512


def _ffn_body(*refs, n_sets):
    ins = refs[:2 * n_sets]
    wg_ref, wu_ref, wd_ref = refs[2 * n_sets:2 * n_sets + 3]
    outs = refs[2 * n_sets + 3:3 * n_sets + 3]
    accs = refs[3 * n_sets + 3:4 * n_sets + 3]
    f = pl.program_id(1)
    dot = functools.partial(jnp.dot, preferred_element_type=F32)
    for s in range(n_sets):
        x_ref, a_ref, y_ref, acc_ref = ins[2 * s], ins[2 * s + 1], outs[s], accs[s]
        rows = x_ref.shape[0]

        @pl.when(f == 0)
        def _():
            acc_ref[...] = jnp.zeros_like(acc_ref)

        for r in range(0, rows, FFN_TM):
            n = min(FFN_TM, rows - r)
            x = x_ref[r:r + n, :]
            hid = _silu(dot(x, wg_ref[...].astype(BF16))) * dot(x, wu_ref[...].astype(BF16))
            acc_ref[r:r + n, :] += dot(hid.astype(BF16), wd_ref[...].astype(BF16))

        @pl.when(f == pl.num_programs(1) - 1)
        def _():
            for r in range(0, rows, FFN_TM):
                n = min(FFN_TM, rows - r)
                y_ref[r:r + n, :] = (acc_ref[r:r + n, :] * a_ref[r:r + n, 0:1]).astype(BF16)


def _ffn_call(sets, wg, wu, wd, layer):
    _, n_exp, d, ff = wg.shape
    in_specs, args, out_specs, out_shape, scratch = [], [], [], [], []
    for xg, affc in sets:
        m = xg.shape[1]
        in_specs += [pl.BlockSpec((None, m, d), lambda e, f: (e, 0, 0)),
                     pl.BlockSpec((None, m, SMALL_W), lambda e, f: (e, 0, 0))]
        args += [xg, affc]
        out_specs.append(pl.BlockSpec((None, m, d), lambda e, f: (e, 0, 0)))
        out_shape.append(jax.ShapeDtypeStruct((n_exp, m, d), BF16))
        scratch.append(pltpu.VMEM((m, d), F32))
    in_specs += [pl.BlockSpec((None, None, d, FFN_TF), lambda e, f: (layer, e, 0, f)),
                 pl.BlockSpec((None, None, d, FFN_TF), lambda e, f: (layer, e, 0, f)),
                 pl.BlockSpec((None, None, FFN_TF, d), lambda e, f: (layer, e, f, 0))]
    return pl.pallas_call(
        functools.partial(_ffn_body, n_sets=len(sets)),
        grid=(n_exp, ff // FFN_TF),
        in_specs=in_specs, out_specs=out_specs, out_shape=out_shape, scratch_shapes=scratch,
        compiler_params=_cparams("parallel", "arbitrary"),
        name="expert_ffn",
    )(*args, wg, wu, wd)


def _scatter_body(post_ref, y_ref, h1_ref, mod_ref, fw_ref, o_ref, *, final):
    n_exp, cap, d = y_ref.shape
    pos = post_ref[...]
    slots = lax.broadcasted_iota(I32, (pos.shape[0], cap), 1).astype(F32)
    hit = jnp.concatenate([(pos[:, e:e+1] == slots).astype(BF16) for e in range(n_exp)], axis=1)
    upd = jnp.dot(hit, y_ref[...].reshape(n_exp * cap, d), preferred_element_type=F32)
    h2 = h1_ref[...] + mod_ref[5:6, :] * upd
    o_ref[...] = _rms(h2, fw_ref[...]) if final else h2


def _scatter_call(post, y, h1, mod_l, is_ctx, final_w):
    batch, t, d = h1.shape
    n_exp = y.shape[0]
    cap = y.shape[1] // batch
    tm = min(512, t)
    final = final_w is not None
    fw = final_w if final else jnp.ones((1, d), F32)
    return pl.pallas_call(
        functools.partial(_scatter_body, final=final),
        grid=(batch, t // tm),
        in_specs=[pl.BlockSpec((None, tm, SMALL_W), lambda b, i: (b, i, 0)),
                  pl.BlockSpec((n_exp, cap, d), lambda b, i: (0, b, 0)),
                  pl.BlockSpec((None, tm, d), lambda b, i: (b, i, 0)),
                  _mod_spec(is_ctx, batch),
                  pl.BlockSpec((1, d), lambda b, i: (0, 0))],
        out_specs=pl.BlockSpec((None, tm, d), lambda b, i: (b, i, 0)),
        out_shape=jax.ShapeDtypeStruct((batch, t, d), F32),
        compiler_params=_cparams("parallel", "parallel"),
        name="scatter_residual",
    )(post, y, h1, mod_l, fw)


def kernel(x, c, ctx, c_ctx, w_mod, b_mod, norm1_w, norm2_w, w_in, w_out, ret_decay_logit, ret_norm_w,
           gla_gate_w, gla_gate_b, gla_norm_w, ssd_conv_w, ssd_conv_b, ssd_dt_bias, ssd_a_log, ssd_d,
           ssd_norm_w, mlstm_gate_b, mlstm_norm_w, router_w, expert_w_gate, expert_w_up, expert_w_down,
           final_norm_w):
    batch, t_lat, d = x.shape
    t_ctx = ctx.shape[1]
    depth = w_mod.shape[0]
    assert batch + 1 <= ROW_PAD and d == D_MODEL
    for t in (t_lat, t_ctx):
        assert t % 128 == 0 and (t // CHUNK) % min(SCAN_STEPS, t // CHUNK) == 0
    assert t_lat % 256 == 0
    cs = jnp.concatenate([c, c_ctx[None], jnp.zeros((ROW_PAD - batch - 1, d), F32)], axis=0)
    mod = _mod_call(cs, w_mod, b_mod).reshape(depth, ROW_PAD, 6, d)
    cos, sin = _rope_tables(t_lat)
    cap_lat = EC_CAPACITY * t_lat // N_EXPERTS
    cap_ctx = EC_CAPACITY * t_ctx // N_EXPERTS
    h_lat, h_ctx = x, ctx
    for l in range(depth):
        ctx_out = l < depth - 1
        mod_l = mod[l]
        w_big = _arrange_w_in(w_in[l])
        ul = _in_call(h_lat, mod_l, norm1_w[l][None], w_big, False)
        uc = _in_call(h_ctx, mod_l, norm1_w[l][None], w_big, True)
        ys = [_ret_call(ul[0], uc[0], jax.nn.log_sigmoid(ret_decay_logit[l].astype(F32)), cos, sin,
                        ret_norm_w[l][None], ctx_out),
              _gla_call(ul[1], uc[1], ul[4], uc[4], gla_gate_w[l], gla_gate_b[l], gla_norm_w[l][None], ctx_out),
              _ssd_call(ul[2], uc[2], ul[4], uc[4], ssd_conv_w[l], ssd_conv_b[l], ssd_dt_bias[l], ssd_a_log[l],
                        ssd_d[l], ssd_norm_w[l][None], ctx_out),
              _mlstm_call(ul[3], uc[3], ul[4], uc[4], mlstm_gate_b[l], mlstm_norm_w[l][None], ctx_out)]
        wo = w_out[l].astype(BF16)
        rw3 = jnp.pad(jnp.concatenate(_split3(router_w[l]), axis=1), ((0, 0), (0, SMALL_W - 3 * N_EXPERTS)))
        n2w = norm2_w[l][None]
        h1_lat, hn_lat, lt_lat = _out_call([y[0] for y in ys], h_lat, mod_l, wo, n2w, rw3, False)
        xg_lat, affc_lat, post_lat = _route_call(lt_lat, hn_lat, cap_lat)
        sets = [(xg_lat, affc_lat)]
        if ctx_out:
            h1_ctx, hn_ctx, lt_ctx = _out_call([y[1] for y in ys], h_ctx, mod_l, wo, n2w, rw3, True)
            xg_ctx, affc_ctx, post_ctx = _route_call(lt_ctx, hn_ctx, cap_ctx)
            sets.append((xg_ctx, affc_ctx))
        y_exp = _ffn_call(sets, expert_w_gate, expert_w_up, expert_w_down, l)
        h_lat = _scatter_call(post_lat, y_exp[0], h1_lat, mod_l, False,
                              None if ctx_out else final_norm_w[None])
        if ctx_out:
            h_ctx = _scatter_call(post_ctx, y_exp[1], h1_ctx, mod_l, True, None)
    return h_lat
```

```python
import functools
import math

import jax
import jax.numpy as jnp
from jax import lax
from jax.experimental import pallas as pl
from jax.experimental.pallas import tpu as pltpu

F32 = jnp.float32
BF16 = jnp.bfloat16
I32 = jnp.int32

D_MODEL = 1024
GRID_W = 64
EPS = 1e-6
ROPE_BASE = 10000.0
GROUP_W = 256
HEAD_DIM = 64
N_HEADS = 4
CHUNK = 64
GLA_DK = 32
GLA_QK = 128
GLA_RANK = 16
GLA_TAU = 16.0
GLA_SUB = 16
SSD_CONV = 5
SSD_BC = 128
N_EXPERTS = 16
EC_CAPACITY = 2
EXPERT_FF = 1536
SMALL_W = 128
LR_OFF, DT_OFF, GATE_OFF = 0, 16, 24
ROW_PAD = 16
VMEM_LIMIT = 56 * 1024 * 1024


def _cparams(*sem):
    return pltpu.CompilerParams(dimension_semantics=sem, vmem_limit_bytes=VMEM_LIMIT)


def _bdot(a, b):
    return jnp.dot(a.astype(BF16), b.astype(BF16), preferred_element_type=F32)


def _bdot_nt(a, b):
    return lax.dot_general(a.astype(BF16), b.astype(BF16), (((1,), (1,)), ((), ())),
                           preferred_element_type=F32)


def _bdot_tn(a, b):
    return lax.dot_general(a.astype(BF16), b.astype(BF16), (((0,), (0,)), ((), ())),
                           preferred_element_type=F32)


def _silu(x):
    return x * jax.nn.sigmoid(x)


def _softplus(x):
    return jnp.maximum(x, 0.0) + jnp.log1p(jnp.exp(-jnp.abs(x)))


def _log_sigmoid(x):
    return -_softplus(-x)


def _rms(x, w):
    return x * lax.rsqrt(jnp.mean(x * x, axis=-1, keepdims=True) + EPS) * w


def _mod_body(cs_ref, w_ref, b_ref, o_ref):
    o_ref[...] = _bdot(_silu(cs_ref[...]), w_ref[...]) + b_ref[...]


def _mod_call(cs, w_mod, b_mod):
    depth, d, n = w_mod.shape
    tn = 1536
    return pl.pallas_call(
        _mod_body,
        grid=(depth, n // tn),
        in_specs=[pl.BlockSpec((ROW_PAD, d), lambda l, j: (0, 0)),
                  pl.BlockSpec((None, d, tn), lambda l, j: (l, 0, j)),
                  pl.BlockSpec((None, 1, tn), lambda l, j: (l, 0, j))],
        out_specs=pl.BlockSpec((None, ROW_PAD, tn), lambda l, j: (l, 0, j)),
        out_shape=jax.ShapeDtypeStruct((depth, ROW_PAD, n), F32),
        compiler_params=_cparams("parallel", "parallel"),
        name="adaln",
    )(cs, w_mod, b_mod.reshape(depth, 1, n))


def _mod_spec(is_ctx, batch):
    if is_ctx:
        return pl.BlockSpec((None, 6, D_MODEL), lambda b, i: (batch, 0, 0))
    return pl.BlockSpec((None, 6, D_MODEL), lambda b, i: (b, 0, 0))


IN_WIDTHS = (1024, 768, 768, 1024, SMALL_W)


def _in_body(h_ref, mod_ref, nw_ref, w_ref, ret_ref, gla_ref, ssd_ref, mls_ref, sm_ref):
    mod = mod_ref[...]
    xn = _rms(h_ref[...], nw_ref[...]) * (1.0 + mod[1:2, :]) + mod[0:1, :]
    u = jnp.dot(xn.astype(BF16), w_ref[...], preferred_element_type=F32)
    off = 0
    for ref, w in zip((ret_ref, gla_ref, ssd_ref, mls_ref, sm_ref), IN_WIDTHS):
        ref[...] = u[:, off:off + w]
        off += w


def _in_call(h, mod_l, nw, w_big, is_ctx):
    batch, t, d = h.shape
    tm = min(1024, t)
    n = w_big.shape[1]
    return pl.pallas_call(
        _in_body,
        grid=(batch, t // tm),
        in_specs=[pl.BlockSpec((None, tm, d), lambda b, i: (b, i, 0)),
                  _mod_spec(is_ctx, batch),
                  pl.BlockSpec((1, d), lambda b, i: (0, 0)),
                  pl.BlockSpec((d, n), lambda b, i: (0, 0), pipeline_mode=pl.Buffered(1))],
        out_specs=[pl.BlockSpec((None, tm, w), lambda b, i: (b, i, 0)) for w in IN_WIDTHS],
        out_shape=[jax.ShapeDtypeStruct((batch, t, w), F32) for w in IN_WIDTHS],
        compiler_params=_cparams("parallel", "parallel"),
        name="in_proj",
    )(h, mod_l, nw, w_big)


def _arrange_w_in(w):
    ret_end = 4 * GROUP_W
    gla_main = ret_end + 2 * GLA_QK + 2 * GROUP_W
    gla_end = gla_main + GLA_RANK
    ssd_main = gla_end + GROUP_W + GROUP_W + 2 * SSD_BC
    ssd_end = ssd_main + 2 * N_HEADS
    mls_main = ssd_end + 4 * GROUP_W
    mls_end = mls_main + 4 * N_HEADS
    narrow = (gla_end - gla_main) + (ssd_end - ssd_main) + (mls_end - mls_main)
    pad = jnp.zeros((w.shape[0], SMALL_W - narrow), w.dtype)
    return jnp.concatenate([w[:, :ret_end], w[:, ret_end:gla_main], w[:, gla_end:ssd_main],
                            w[:, ssd_end:mls_main], w[:, gla_main:gla_end], w[:, ssd_main:ssd_end],
                            w[:, mls_main:mls_end], pad], axis=1).astype(BF16)


def _split3(x):
    hi = x.astype(BF16)
    r1 = x - hi.astype(F32)
    mid = r1.astype(BF16)
    return hi, mid, (r1 - mid.astype(F32)).astype(BF16)


def _iota2(shape, axis):
    return lax.broadcasted_iota(I32, shape, axis)


def _block_consts(rev):
    i = _iota2((CHUNK, GROUP_W), 0)
    j = _iota2((CHUNK, GROUP_W), 1) % HEAD_DIM
    return ((j >= i) if rev else (j <= i)), (j == i)


def _same_head():
    return (_iota2((GROUP_W, GROUP_W), 0) // HEAD_DIM) == (_iota2((GROUP_W, GROUP_W), 1) // HEAD_DIM)


def _cum_mat(rev):
    i = _iota2((CHUNK, 3 * CHUNK), 0)
    t = _iota2((CHUNK, 3 * CHUNK), 1) % CHUNK
    return ((t >= i) if rev else (t <= i)).astype(F32).astype(BF16)


def _expand_mat(first_lane, width=GROUP_W):
    return (_iota2((SMALL_W, width), 0) == first_lane + _iota2((SMALL_W, width), 1) // HEAD_DIM
            ).astype(F32).astype(BF16)


def _spread(x, expand):
    n = x.shape[0]
    y = jnp.dot(jnp.concatenate(_split3(x), axis=0), expand, preferred_element_type=F32)
    return (y[0:n] + y[n:2 * n]) + y[2 * n:3 * n]


def _cumsum_b(x_b, cum3):
    return jnp.dot(cum3, jnp.concatenate(_split3(x_b), axis=0), preferred_element_type=F32)


def _tile_heads(x):
    return jnp.concatenate([x] * N_HEADS, axis=0)


def _blockdiag(x, same_head):
    xb = _tile_heads(x.astype(BF16))
    return jnp.where(same_head, xb, jnp.zeros_like(xb))


def _lane_scores(lanes, same_head):
    for ln in lanes:
        if ln["want_out"]:
            ln["scores"] = _bdot_nt(ln["q"], _blockdiag(ln["k"], same_head))


def _decay_lanes(lanes, consts, s_ref):
    same_head = consts[0][2]
    for ln in lanes:
        mask_b, eye_b, _ = consts[ln["d"]]
        g_b = ln["g_b"]
        ln["g_tot"] = g_b[0:1] if ln["d"] == 1 else g_b[CHUNK - 1:CHUNK]
        if ln["want_out"]:
            g_row = jnp.sum(jnp.where(eye_b, g_b, 0.0), axis=0, keepdims=True)
            ln["p"] = ln["scores"] * jnp.exp(jnp.where(mask_b, g_b - g_row, -jnp.inf))
            ln["qg"] = ln["q"] * jnp.exp(g_b)
        ln["khat"] = ln["k"] * jnp.exp(ln["g_tot"] - g_b)
    for ln in lanes:
        ln["kv"] = jnp.where(same_head, _bdot_tn(ln["khat"], ln["v"]), 0.0)
    for ln in lanes:
        if ln["want_out"]:
            ln["pv"] = _bdot(ln["p"], _blockdiag(ln["v"], same_head))
    outs = []
    for ln in lanes:
        s_in = s_ref[ln["d"]]
        outs.append(ln["pv"] + _bdot(ln["qg"], s_in) if ln["want_out"] else None)
        s_ref[ln["d"]] = jnp.exp(ln["g_tot"]) * s_in + ln["kv"]
    return outs


def _seg_sum(x, same_head):
    seg = same_head.astype(F32).astype(BF16)
    hi = x.astype(BF16)
    lo = (x - hi.astype(F32)).astype(BF16)
    return jnp.dot(hi, seg, preferred_element_type=F32) + jnp.dot(lo, seg, preferred_element_type=F32)


def _head_rms(x, same_head, center):
    if center:
        x = x - _seg_sum(x, same_head) * (1.0 / HEAD_DIM)
    return x * lax.rsqrt(_seg_sum(x * x, same_head) * (1.0 / HEAD_DIM) + EPS)


SCAN_STEPS = 8


def _scan_lanes(n_ctx, n_lat, multi_step):
    def run(is_lat, n):
        steps = min(SCAN_STEPS, n)

        def it(s, carry):
            lanes = []
            for j in range(steps):
                lanes += [(steps * s + j, 0), (n - 1 - steps * s - j, 1)]
            multi_step(is_lat, lanes)
            return carry

        lax.fori_loop(0, n // steps, it, 0)

    run(False, n_ctx)
    run(True, n_lat)


def _post_tiles(t):
    tile = min(256, t)
    return [(r, tile) for r in range(0, t, tile)]


def _mixer_call(body, name, batch, t_lat, t_ctx, ctx_out, per_sample, shared, scratch):
    in_specs = [pl.BlockSpec((None,) + a.shape[1:], lambda b, nd=a.ndim: (b,) + (0,) * (nd - 1))
                for a in per_sample]
    in_specs += [pl.BlockSpec(a.shape, lambda b, nd=a.ndim: (0,) * nd) for a in shared]
    args = list(per_sample) + list(shared)
    out_shape = [jax.ShapeDtypeStruct((batch, t_lat, GROUP_W), F32)]
    out_specs = [pl.BlockSpec((None, t_lat, GROUP_W), lambda b: (b, 0, 0))]
    if ctx_out:
        out_shape.append(jax.ShapeDtypeStruct((batch, t_ctx, GROUP_W), F32))
        out_specs.append(pl.BlockSpec((None, t_ctx, GROUP_W), lambda b: (b, 0, 0)))
    return pl.pallas_call(
        body, grid=(batch,), in_specs=in_specs, out_specs=out_specs, out_shape=out_shape,
        scratch_shapes=scratch, compiler_params=_cparams("parallel"), name=name,
    )(*args)


def _rope(x, cos, sin_signed, first_half):
    return x * cos + jnp.where(first_half, pltpu.roll(x, x.shape[1] - 16, 1), pltpu.roll(x, 16, 1)) * sin_signed


def _scan_consts():
    same_head = _same_head()
    return tuple(_block_consts(rev) + (same_head,) for rev in (False, True))


def _ret_body(ul_ref, uc_ref, lg_ref, cos_ref, sin_ref, nw_ref, *rest, ctx_out):
    if ctx_out:
        yl_ref, yc_ref, s_ref, accl_ref, accc_ref = rest
    else:
        yl_ref, s_ref, accl_ref, accc_ref = rest
        yc_ref = None
    t_lat, t_ctx = ul_ref.shape[0], uc_ref.shape[0]
    s_ref[...] = jnp.zeros_like(s_ref)
    first_half = (_iota2((1, GROUP_W), 1) % 32) < 16
    consts = _scan_consts()
    row = _iota2((CHUNK, GROUP_W), 0).astype(F32)
    n_decays = (row + 1.0, CHUNK - row)

    def multi_step(is_lat, chunks):
        u_ref, acc_ref = (ul_ref, accl_ref) if is_lat else (uc_ref, accc_ref)
        lanes = []
        for c, d in chunks:
            r = pl.multiple_of(c * CHUNK, CHUNK)
            q = u_ref[pl.ds(r, CHUNK), 0:GROUP_W] * (HEAD_DIM ** -0.5)
            k = u_ref[pl.ds(r, CHUNK), GROUP_W:2 * GROUP_W]
            if is_lat:
                cos, sin = cos_ref[pl.ds(r, CHUNK), :], sin_ref[pl.ds(r, CHUNK), :]
                q, k = _rope(q, cos, sin, first_half), _rope(k, cos, sin, first_half)
            lanes.append(dict(d=d, r=r, want_out=is_lat or ctx_out, q=q, k=k,
                              v=u_ref[pl.ds(r, CHUNK), 2 * GROUP_W:3 * GROUP_W], g_b=n_decays[d] * lg_ref[d]))
        _lane_scores(lanes, consts[0][2])
        for ln, o in zip(lanes, _decay_lanes(lanes, consts, s_ref)):
            if ln["want_out"]:
                acc_ref[ln["d"], pl.ds(ln["r"], CHUNK), :] = o

    _scan_lanes(t_ctx // CHUNK, t_lat // CHUNK, multi_step)

    def post(u_ref, acc_ref, y_ref, t):
        for r, n in _post_tiles(t):
            o = _head_rms(acc_ref[0, r:r + n, :] + acc_ref[1, r:r + n, :], consts[0][2], True)
            y_ref[r:r + n, :] = o * nw_ref[...] * _silu(u_ref[r:r + n, 3 * GROUP_W:4 * GROUP_W])

    post(ul_ref, accl_ref, yl_ref, t_lat)
    if ctx_out:
        post(uc_ref, accc_ref, yc_ref, t_ctx)


def _state_scratch(n, t_lat, t_ctx):
    return ([pltpu.VMEM((2, GROUP_W, GROUP_W), F32)] * n
            + [pltpu.VMEM((2, t_lat, GROUP_W), F32), pltpu.VMEM((2, t_ctx, GROUP_W), F32)])


def _ret_call(u_lat, u_ctx, log_gamma, cos, sin, nw, ctx_out):
    batch, t_lat, _ = u_lat.shape
    t_ctx = u_ctx.shape[1]
    lg = jnp.repeat(log_gamma, HEAD_DIM, axis=1)[:, None, :]
    return _mixer_call(functools.partial(_ret_body, ctx_out=ctx_out), "retention", batch, t_lat, t_ctx, ctx_out,
                       [u_lat, u_ctx], [lg, cos, sin, nw], _state_scratch(1, t_lat, t_ctx))


def _rope_tables(t_lat):
    nf = 16
    inv = ROPE_BASE ** (-jnp.arange(nf, dtype=F32) / nf)
    rows = jnp.repeat(jnp.arange(t_lat // GRID_W), GRID_W).astype(F32)
    cols = jnp.tile(jnp.arange(GRID_W), t_lat // GRID_W).astype(F32)
    ang_r = rows[:, None] * inv[None, :]
    ang_c = cols[:, None] * inv[None, :]
    cos_h = jnp.concatenate([jnp.cos(ang_r)] * 2 + [jnp.cos(ang_c)] * 2, axis=1)
    sin_h = jnp.concatenate([-jnp.sin(ang_r), jnp.sin(ang_r), -jnp.sin(ang_c), jnp.sin(ang_c)], axis=1)
    return jnp.tile(cos_h, (1, N_HEADS)), jnp.tile(sin_h, (1, N_HEADS))


def _ssd_body(ul_ref, uc_ref, sml_ref, smc_ref, cw_ref, cb_ref, dtb_ref, an_ref, dsk_ref, nw_ref, *rest,
              ctx_out):
    if ctx_out:
        yl_ref, yc_ref, s_ref, accl_ref, accc_ref, xl_ref, xc_ref, pad_ref = rest
    else:
        yl_ref, s_ref, accl_ref, accc_ref, xl_ref, xc_ref, pad_ref = rest
        yc_ref = None
    t_lat, t_ctx = ul_ref.shape[0], uc_ref.shape[0]
    conv_ch = GROUP_W + 2 * SSD_BC
    half = (SSD_CONV - 1) // 2
    halo = 8

    def conv(u_ref, x_ref, t):
        pad_ref[0:halo, :] = jnp.zeros((halo, conv_ch), F32)
        pad_ref[halo + t:2 * halo + t, :] = jnp.zeros((halo, conv_ch), F32)
        for r, n in _post_tiles(t):
            pad_ref[halo + r:halo + r + n, :] = u_ref[r:r + n, GROUP_W:GROUP_W + conv_ch]
        for r, n in _post_tiles(t):
            acc = jnp.zeros((n, conv_ch), F32) + cb_ref[...]
            for tap in range(SSD_CONV):
                lo = halo + r + tap - half
                acc = acc + pad_ref[lo:lo + n, :] * cw_ref[tap:tap + 1, :]
            act = _silu(acc)
            grp = [act[:, GROUP_W + m * HEAD_DIM:GROUP_W + (m + 1) * HEAD_DIM] for m in range(4)]
            x_ref[r:r + n, 0:GROUP_W] = act[:, 0:GROUP_W]
            x_ref[r:r + n, GROUP_W:2 * GROUP_W] = jnp.concatenate([grp[0], grp[0], grp[1], grp[1]], axis=1)
            x_ref[r:r + n, 2 * GROUP_W:3 * GROUP_W] = jnp.concatenate([grp[2], grp[2], grp[3], grp[3]], axis=1)

    conv(uc_ref, xc_ref, t_ctx)
    conv(ul_ref, xl_ref, t_lat)
    s_ref[...] = jnp.zeros_like(s_ref)
    consts = _scan_consts()
    cum3 = (_cum_mat(False), _cum_mat(True))
    expand = (_expand_mat(DT_OFF), _expand_mat(DT_OFF + N_HEADS))

    def multi_step(is_lat, chunks):
        x_ref, sm_ref, acc_ref = (xl_ref, sml_ref, accl_ref) if is_lat else (xc_ref, smc_ref, accc_ref)
        lanes = []
        for c, d in chunks:
            r = pl.multiple_of(c * CHUNK, CHUNK)
            dt = _spread(_softplus(sm_ref[pl.ds(r, CHUNK), :] + dtb_ref[...]), expand[d])
            lanes.append(dict(d=d, r=r, want_out=is_lat or ctx_out, dt=dt,
                              k=x_ref[pl.ds(r, CHUNK), GROUP_W:2 * GROUP_W],
                              q=x_ref[pl.ds(r, CHUNK), 2 * GROUP_W:3 * GROUP_W]))
        _lane_scores(lanes, consts[0][2])
        for ln in lanes:
            ln["g_b"] = _cumsum_b(ln["dt"] * an_ref[ln["d"]], cum3[ln["d"]])
            ln["v"] = x_ref[pl.ds(ln["r"], CHUNK), 0:GROUP_W] * ln["dt"]
        for ln, o in zip(lanes, _decay_lanes(lanes, consts, s_ref)):
            if ln["want_out"]:
                acc_ref[ln["d"], pl.ds(ln["r"], CHUNK), :] = o

    _scan_lanes(t_ctx // CHUNK, t_lat // CHUNK, multi_step)

    def post(u_ref, x_ref, acc_ref, y_ref, t):
        for r, n in _post_tiles(t):
            y = acc_ref[0, r:r + n, :] + acc_ref[1, r:r + n, :] + dsk_ref[...] * x_ref[r:r + n, 0:GROUP_W]
            y_ref[r:r + n, :] = _rms(y * _silu(u_ref[r:r + n, 0:GROUP_W]), nw_ref[...])

    post(ul_ref, xl_ref, accl_ref, yl_ref, t_lat)
    if ctx_out:
        post(uc_ref, xc_ref, accc_ref, yc_ref, t_ctx)


def _ssd_call(u_lat, u_ctx, sm_lat, sm_ctx, conv_w, conv_b, dt_bias, a_log, d_skip, nw, ctx_out):
    batch, t_lat, _ = u_lat.shape
    t_ctx = u_ctx.shape[1]
    conv_ch = GROUP_W + 2 * SSD_BC
    dtb = jnp.zeros((SMALL_W,), F32).at[DT_OFF:DT_OFF + 2 * N_HEADS].set(dt_bias.reshape(-1))[None]
    a_neg = jnp.repeat(-jnp.exp(a_log.astype(F32)), HEAD_DIM, axis=1)[:, None, :]
    cw = jnp.concatenate([conv_w, jnp.zeros((8 - SSD_CONV, conv_ch), F32)], axis=0)
    dsk = jnp.repeat(d_skip, HEAD_DIM)[None]
    scratch = _state_scratch(1, t_lat, t_ctx) + [
        pltpu.VMEM((t_lat, 3 * GROUP_W), F32), pltpu.VMEM((t_ctx, 3 * GROUP_W), F32),
        pltpu.VMEM((t_lat + 16, conv_ch), F32)]
    return _mixer_call(functools.partial(_ssd_body, ctx_out=ctx_out), "ssd", batch, t_lat, t_ctx, ctx_out,
                       [u_lat, u_ctx, sm_lat, sm_ctx], [cw, conv_b[None], dtb, a_neg, dsk, nw], scratch)


def _mlstm_body(ul_ref, uc_ref, sml_ref, smc_ref, gb_ref, nw_ref, *rest, ctx_out):
    if ctx_out:
        yl_ref, yc_ref, c_ref, n_ref, m_ref, accl_ref, accc_ref = rest
    else:
        yl_ref, c_ref, n_ref, m_ref, accl_ref, accc_ref = rest
        yc_ref = None
    t_lat, t_ctx = ul_ref.shape[0], uc_ref.shape[0]
    c_ref[...] = jnp.zeros_like(c_ref)
    n_ref[...] = jnp.zeros_like(n_ref)
    m_ref[...] = jnp.zeros_like(m_ref)
    consts = _scan_consts()
    same_head = consts[0][2]
    seg = same_head.astype(F32).astype(BF16)
    cum3 = (_cum_mat(False), _cum_mat(True))
    f_lane = tuple(GATE_OFF + 2 * N_HEADS * d + N_HEADS for d in (0, 1))
    expand_fi = tuple(jnp.concatenate([_expand_mat(f_lane[d]), _expand_mat(f_lane[d] - N_HEADS)], axis=1)
                      for d in (0, 1))
    lane = _iota2((1, SMALL_W), 1)
    is_f = (lane >= GATE_OFF) & ((lane - GATE_OFF) % (2 * N_HEADS) >= N_HEADS)

    def running_max(x, rev):
        row = _iota2(x.shape, 0)
        s = 1
        while s < CHUNK:
            shifted = pltpu.roll(x, CHUNK - s if rev else s, 0)
            valid = (row < CHUNK - s) if rev else (row >= s)
            x = jnp.maximum(x, jnp.where(valid, shifted, -jnp.inf))
            s *= 2
        return x

    def multi_step(is_lat, chunks):
        u_ref, sm_ref, acc_ref = (ul_ref, sml_ref, accl_ref) if is_lat else (uc_ref, smc_ref, accc_ref)
        want_out = is_lat or ctx_out
        lanes = []
        for c, d in chunks:
            r = pl.multiple_of(c * CHUNK, CHUNK)
            gates = sm_ref[pl.ds(r, CHUNK), :] + gb_ref[...]
            both = _spread(jnp.where(is_f, _log_sigmoid(gates), gates), expand_fi[d])
            lanes.append(dict(d=d, r=r, want_out=want_out, lf_b=both[:, 0:GROUP_W], ig_b=both[:, GROUP_W:2 * GROUP_W],
                              q=u_ref[pl.ds(r, CHUNK), 0:GROUP_W],
                              k=u_ref[pl.ds(r, CHUNK), GROUP_W:2 * GROUP_W] * (HEAD_DIM ** -0.5),
                              v=u_ref[pl.ds(r, CHUNK), 2 * GROUP_W:3 * GROUP_W]))
        _lane_scores(lanes, same_head)
        for ln in lanes:
            ln["g_b"] = _cumsum_b(ln["lf_b"], cum3[ln["d"]])
        m_now, n_now = [m_ref[0], m_ref[1]], [n_ref[0], n_ref[1]]
        for ln in lanes:
            d, g_b, ig_b = ln["d"], ln["g_b"], ln["ig_b"]
            mask_b, eye_b, _ = consts[d]
            g_tot = g_b[0:1] if d == 1 else g_b[CHUNK - 1:CHUNK]
            w_end = g_tot - g_b + ig_b
            b_max = jnp.max(w_end, axis=0, keepdims=True)
            ln["ke"] = ln["k"] * jnp.exp(w_end - b_max)
            m_in, n_in = m_now[d], n_now[d]
            m_new = jnp.maximum(g_tot + m_in, b_max)
            ln["old"], ln["new"] = jnp.exp(g_tot + m_in - m_new), jnp.exp(b_max - m_new)
            m_now[d] = m_new
            n_now[d] = ln["old"] * n_in + ln["new"] * jnp.sum(ln["ke"], axis=0, keepdims=True)
            if want_out:
                m_inter = g_b + m_in
                m_tot = jnp.maximum(m_inter, g_b + running_max(ig_b - g_b, d == 1))
                g_row = jnp.sum(jnp.where(eye_b, g_b, 0.0), axis=0, keepdims=True)
                ig_row = jnp.sum(jnp.where(eye_b, ig_b, 0.0), axis=0, keepdims=True)
                d_log = jnp.where(mask_b, g_b - g_row + ig_row, -jnp.inf)
                ln["qk"] = ln["scores"] * jnp.exp(d_log - m_tot)
                ln["s_inter"] = jnp.exp(m_inter - m_tot)
                ln["floor"] = jnp.exp(-m_tot)
                ln["den_terms"] = ln["qk"] + ln["s_inter"] * (ln["q"] * n_in)
        m_ref[0], m_ref[1] = m_now
        n_ref[0], n_ref[1] = n_now
        for ln in lanes:
            ln["kv"] = jnp.where(same_head, _bdot_tn(ln["ke"], ln["v"]), 0.0)
        if want_out:
            for ln in lanes:
                ln["num"] = _bdot(ln["qk"], _blockdiag(ln["v"], same_head))
                ln["den"] = _seg_sum(ln["den_terms"], same_head)
        for ln in lanes:
            d = ln["d"]
            c_in = c_ref[d]
            if want_out:
                num = ln["num"] + ln["s_inter"] * _bdot(ln["q"], c_in)
                acc_ref[d, pl.ds(ln["r"], CHUNK), :] = num / jnp.maximum(jnp.abs(ln["den"]), ln["floor"])
            c_ref[d] = ln["old"] * c_in + ln["new"] * ln["kv"]

    _scan_lanes(t_ctx // CHUNK, t_lat // CHUNK, multi_step)

    def post(u_ref, acc_ref, y_ref, t):
        for r, n in _post_tiles(t):
            o = _head_rms(acc_ref[0, r:r + n, :] + acc_ref[1, r:r + n, :], same_head, False)
            y_ref[r:r + n, :] = jax.nn.sigmoid(u_ref[r:r + n, 3 * GROUP_W:4 * GROUP_W]) * (o * nw_ref[...])

    post(ul_ref, accl_ref, yl_ref, t_lat)
    if ctx_out:
        post(uc_ref, accc_ref, yc_ref, t_ctx)


def _mlstm_call(u_lat, u_ctx, sm_lat, sm_ctx, gate_b, nw, ctx_out):
    batch, t_lat, _ = u_lat.shape
    t_ctx = u_ctx.shape[1]
    gb = jnp.zeros((SMALL_W,), F32).at[GATE_OFF:GATE_OFF + 4 * N_HEADS].set(gate_b.reshape(-1))[None]
    scratch = ([pltpu.VMEM((2, GROUP_W, GROUP_W), F32), pltpu.VMEM((2, 1, GROUP_W), F32),
                pltpu.VMEM((2, 1, GROUP_W), F32)]
               + [pltpu.VMEM((2, t_lat, GROUP_W), F32), pltpu.VMEM((2, t_ctx, GROUP_W), F32)])
    return _mixer_call(functools.partial(_mlstm_body, ctx_out=ctx_out), "mlstm", batch, t_lat, t_ctx, ctx_out,
                       [u_lat, u_ctx, sm_lat, sm_ctx], [gb, nw], scratch)


def _gla_body(ul_ref, uc_ref, sml_ref, smc_ref, gw_ref, gb_ref, nw_ref, *rest, ctx_out):
    if ctx_out:
        yl_ref, yc_ref, st_ref, accl_ref, accc_ref, gl_ref, gc_ref = rest
    else:
        yl_ref, st_ref, accl_ref, accc_ref, gl_ref, gc_ref = rest
        yc_ref = None
    t_lat, t_ctx = ul_ref.shape[0], uc_ref.shape[0]
    st_ref[...] = jnp.zeros_like(st_ref)
    n_sub = CHUNK // GLA_SUB
    i = lax.broadcasted_iota(I32, (CHUNK, CHUNK), 0)
    j = lax.broadcasted_iota(I32, (CHUNK, CHUNK), 1)
    same = (i // GLA_SUB) == (j // GLA_SUB)
    lcum = ((same & (j <= i)).astype(F32).astype(BF16), (same & (j >= i)).astype(F32).astype(BF16))
    expand = ((lax.broadcasted_iota(I32, (GLA_QK, GROUP_W), 0) // GLA_DK)
              == (lax.broadcasted_iota(I32, (GLA_QK, GROUP_W), 1) // HEAD_DIM)).astype(F32).astype(BF16)
    blockdiag = ((lax.broadcasted_iota(I32, (GROUP_W, GLA_QK), 0) // HEAD_DIM)
                 == (lax.broadcasted_iota(I32, (GROUP_W, GLA_QK), 1) // GLA_DK)).astype(F32)
    sub_row = lax.broadcasted_iota(I32, (CHUNK, 1), 0) % GLA_SUB

    def roll_sub(x, delta):
        return jnp.concatenate([pltpu.roll(x[a * GLA_SUB:(a + 1) * GLA_SUB], delta, 0) for a in range(n_sub)],
                               axis=0)

    def qkv(u_ref, r):
        return (u_ref[pl.ds(r, CHUNK), 0:GLA_QK] * (GLA_DK ** -0.5), u_ref[pl.ds(r, CHUNK), GLA_QK:2 * GLA_QK],
                u_ref[pl.ds(r, CHUNK), 2 * GLA_QK:2 * GLA_QK + GROUP_W])

    def intra(u_ref, sm_ref, g_ref, acc_ref, want_out, c):
        r = pl.multiple_of(c * CHUNK, CHUNK)
        lr = sm_ref[pl.ds(r, CHUNK), LR_OFF:LR_OFF + GLA_RANK]
        g = []
        for d in (0, 1):
            la = _log_sigmoid(_bdot(lr, gw_ref[d]) + gb_ref[d]) / GLA_TAU
            g.append(sum(jnp.dot(lcum[d], piece, preferred_element_type=F32) for piece in _split3(la)))
            g_ref[d, pl.ds(r, CHUNK), :] = g[d]
        if not want_out:
            return
        q, k, v = qkv(u_ref, r)
        prods, vals = [2.0 * q * k], [v]
        for delta in range(1, GLA_SUB):
            fwd_pair = sub_row >= delta
            e = jnp.exp(jnp.where(fwd_pair, g[0] - roll_sub(g[0], delta), g[1] - roll_sub(g[1], delta)))
            prods.append(q * roll_sub(k, delta) * e)
            vals.append(roll_sub(v, delta))
        w = jnp.dot(jnp.concatenate(prods, axis=0).astype(BF16), expand, preferred_element_type=F32)
        acc_ref[pl.ds(r, CHUNK), :] = sum(w[n * CHUNK:(n + 1) * CHUNK] * vals[n] for n in range(GLA_SUB))

    def intra_ctx(c, carry):
        intra(uc_ref, smc_ref, gc_ref, accc_ref, ctx_out, c)
        return carry

    def intra_lat(c, carry):
        intra(ul_ref, sml_ref, gl_ref, accl_ref, True, c)
        return carry

    lax.fori_loop(0, t_ctx // CHUNK, intra_ctx, 0)
    lax.fori_loop(0, t_lat // CHUNK, intra_lat, 0, unroll=4)

    def multi_step(is_lat, chunks):
        u_ref, g_ref, acc_ref = (ul_ref, gl_ref, accl_ref) if is_lat else (uc_ref, gc_ref, accc_ref)
        want_out = is_lat or ctx_out
        lanes = []
        for c, d in chunks:
            r = pl.multiple_of(c * CHUNK, CHUNK)
            q, k, v = qkv(u_ref, r)
            g = g_ref[d, pl.ds(r, CHUNK), :]
            subs = []
            for a in (range(n_sub) if d == 0 else range(n_sub - 1, -1, -1)):
                rows = slice(a * GLA_SUB, (a + 1) * GLA_SUB)
                g_a = g[rows]
                g_end = g_a[GLA_SUB - 1:GLA_SUB] if d == 0 else g_a[0:1]
                subs.append(dict(a=a, decay=jnp.exp(g_end),
                                 kv=_bdot_tn(v[rows], k[rows] * jnp.exp(g_end - g_a)),
                                 qg=q[rows] * jnp.exp(g_a) if want_out else None))
            lanes.append(dict(d=d, r=r, subs=subs, out=[None] * n_sub))
        for step_lanes in (lanes[n:n + 2] for n in range(0, len(lanes), 2)):
            for n in range(n_sub):
                for ln in step_lanes:
                    sub, st = ln["subs"][n], st_ref[ln["d"]]
                    if want_out:
                        st_b = st.astype(BF16)
                        ln["out"][sub["a"]] = _bdot_nt(sub["qg"], jnp.where(blockdiag > 0.0, st_b, jnp.zeros_like(st_b)))
                    st_ref[ln["d"]] = st * sub["decay"] + sub["kv"]
        if want_out:
            for ln in lanes:
                acc_ref[pl.ds(ln["r"], CHUNK), :] += jnp.concatenate(ln["out"], axis=0)

    _scan_lanes(t_ctx // CHUNK, t_lat // CHUNK, multi_step)

    def post(u_ref, acc_ref, y_ref, t):
        for r, n in _post_tiles(t):
            o = _head_rms(acc_ref[r:r + n, :], _same_head(), False)
            y_ref[r:r + n, :] = o * nw_ref[...] * _silu(u_ref[r:r + n, 2 * GLA_QK + GROUP_W:2 * GLA_QK + 2 * GROUP_W])

    post(ul_ref, accl_ref, yl_ref, t_lat)
    if ctx_out:
        post(uc_ref, accc_ref, yc_ref, t_ctx)


def _gla_call(u_lat, u_ctx, sm_lat, sm_ctx, gate_w, gate_b, nw, ctx_out):
    batch, t_lat, _ = u_lat.shape
    t_ctx = u_ctx.shape[1]
    scratch = [pltpu.VMEM((2, GROUP_W, GLA_QK), F32),
               pltpu.VMEM((t_lat, GROUP_W), F32), pltpu.VMEM((t_ctx, GROUP_W), F32),
               pltpu.VMEM((2, t_lat, GLA_QK), F32), pltpu.VMEM((2, t_ctx, GLA_QK), F32)]
    return _mixer_call(functools.partial(_gla_body, ctx_out=ctx_out), "gla", batch, t_lat, t_ctx, ctx_out,
                       [u_lat, u_ctx, sm_lat, sm_ctx], [gate_w, gate_b[:, None, :], nw], scratch)


def _out_body(yr_ref, yg_ref, ys_ref, ym_ref, h_ref, mod_ref, wo_ref, n2_ref, rw_ref, h1_ref, hn_ref, lt_ref):
    mod = mod_ref[...]
    proj = sum(_bdot(y_ref[...], wo_ref[n * GROUP_W:(n + 1) * GROUP_W, :])
               for n, y_ref in enumerate((yr_ref, yg_ref, ys_ref, ym_ref)))
    h1 = h_ref[...] + mod[2:3, :] * proj
    h1_ref[...] = h1
    xn = _rms(h1, n2_ref[...]) * (1.0 + mod[4:5, :]) + mod[3:4, :]
    hn = xn.astype(BF16)
    hn_ref[...] = hn
    parts = jnp.dot(hn, rw_ref[...], preferred_element_type=F32).T
    lt_ref[...] = parts[0:N_EXPERTS] + parts[N_EXPERTS:2 * N_EXPERTS] + parts[2 * N_EXPERTS:3 * N_EXPERTS]


def _out_call(ys, h, mod_l, wo, n2w, rw3, is_ctx):
    batch, t, d = h.shape
    tm = min(1024, t)
    y_spec = pl.BlockSpec((None, tm, GROUP_W), lambda b, i: (b, i, 0))
    return pl.pallas_call(
        _out_body,
        grid=(batch, t // tm),
        in_specs=[y_spec, y_spec, y_spec, y_spec,
                  pl.BlockSpec((None, tm, d), lambda b, i: (b, i, 0)),
                  _mod_spec(is_ctx, batch),
                  pl.BlockSpec((d, d), lambda b, i: (0, 0)),
                  pl.BlockSpec((1, d), lambda b, i: (0, 0)),
                  pl.BlockSpec(rw3.shape, lambda b, i: (0, 0))],
        out_specs=[pl.BlockSpec((None, tm, d), lambda b, i: (b, i, 0)),
                   pl.BlockSpec((None, tm, d), lambda b, i: (b, i, 0)),
                   pl.BlockSpec((None, N_EXPERTS, tm), lambda b, i: (b, 0, i))],
        out_shape=[jax.ShapeDtypeStruct((batch, t, d), F32),
                   jax.ShapeDtypeStruct((batch, t, d), BF16),
                   jax.ShapeDtypeStruct((batch, N_EXPERTS, t), F32)],
        compiler_params=_cparams("parallel", "parallel"),
        name="out_proj",
    )(*ys, h, mod_l, wo, n2w, rw3)


def _excl_prefix(flags, t):
    blk = min(256, t)
    below = (lax.broadcasted_iota(I32, (blk, blk), 0) < lax.broadcasted_iota(I32, (blk, blk), 1))
    below = below.astype(F32).astype(BF16)
    outs = []
    carry = jnp.zeros((flags.shape[0], 1), F32)
    for n in range(t // blk):
        f = flags[:, n * blk:(n + 1) * blk]
        outs.append(jnp.dot(f.astype(BF16), below, preferred_element_type=F32) + carry)
        carry = carry + jnp.sum(f, axis=1, keepdims=True)
    return jnp.concatenate(outs, axis=1)


def _route_body(lt_ref, hn_ref, xg_ref, affc_ref, post_ref, pos_s, aff_s, *, cap):
    t = hn_ref.shape[0]
    logits = lt_ref[...]
    ex = jnp.exp(logits - jnp.max(logits, axis=0, keepdims=True))
    aff = ex / jnp.sum(ex, axis=0, keepdims=True)
    def count_at_least(cand):
        return jnp.sum((aff >= pltpu.bitcast(cand, F32)).astype(F32), axis=1, keepdims=True)

    def fix_two_bits(n, thr):
        lo = 28 - 2 * n
        best = thr
        for digit in (1, 2, 3):
            cand = thr | (jnp.int32(digit) << lo)
            best = jnp.where(count_at_least(cand) >= float(cap), cand, best)
        return best

    thr = lax.fori_loop(0, 15, fix_two_bits, jnp.zeros((N_EXPERTS, 1), I32))
    above = (aff >= pltpu.bitcast(thr + 1, F32)).astype(F32)
    tied = (aff >= pltpu.bitcast(thr, F32)).astype(F32) - above
    need = float(cap) - jnp.sum(above, axis=1, keepdims=True)
    sel = above + tied * (_excl_prefix(tied, t) < need).astype(F32)
    pos = jnp.where(sel > 0.0, _excl_prefix(sel, t), -1.0)
    pos_s[...] = pos
    aff_s[...] = aff
    post_ref[...] = jnp.concatenate([pos, jnp.full((SMALL_W - N_EXPERTS, t), -1.0, F32)], axis=0).T
    slots = lax.broadcasted_iota(I32, (cap, t), 0).astype(F32)

    def gather(e, carry):
        hit = slots == pos_s[pl.ds(e, 1), :]
        xg_ref[e] = jnp.dot(hit.astype(BF16), hn_ref[...], preferred_element_type=F32).astype(BF16)
        a = jnp.sum(jnp.where(hit, aff_s[pl.ds(e, 1), :], 0.0), axis=1, keepdims=True)
        affc_ref[e] = jnp.broadcast_to(a, (cap, SMALL_W))
        return carry

    lax.fori_loop(0, N_EXPERTS, gather, 0, unroll=2)


def _route_call(lt, hn, cap):
    batch, t, d = hn.shape
    return pl.pallas_call(
        functools.partial(_route_body, cap=cap),
        grid=(batch,),
        in_specs=[pl.BlockSpec((None, N_EXPERTS, t), lambda b: (b, 0, 0)),
                  pl.BlockSpec((None, t, d), lambda b: (b, 0, 0))],
        out_specs=[pl.BlockSpec((N_EXPERTS, cap, d), lambda b: (0, b, 0)),
                   pl.BlockSpec((N_EXPERTS, cap, SMALL_W), lambda b: (0, b, 0)),
                   pl.BlockSpec((None, t, SMALL_W), lambda b: (b, 0, 0))],
        out_shape=[jax.ShapeDtypeStruct((N_EXPERTS, batch * cap, d), BF16),
                   jax.ShapeDtypeStruct((N_EXPERTS, batch * cap, SMALL_W), F32),
                   jax.ShapeDtypeStruct((batch, t, SMALL_W), F32)],
        scratch_shapes=[pltpu.VMEM((N_EXPERTS, t), F32), pltpu.VMEM((N_EXPERTS, t), F32)],
        compiler_params=_cparams("parallel"),
        name="route_gather",
    )(lt, hn)


FFN_TF = 768
FFN_TM = 512


def _ffn_body(*refs, n_sets):
    ins = refs[:2 * n_sets]
    wg_ref, wu_ref, wd_ref = refs[2 * n_sets:2 * n_sets + 3]
    outs = refs[2 * n_sets + 3:3 * n_sets + 3]
    accs = refs[3 * n_sets + 3:4 * n_sets + 3]
    f = pl.program_id(1)
    dot = functools.partial(jnp.dot, preferred_element_type=F32)
    for s in range(n_sets):
        x_ref, a_ref, y_ref, acc_ref = ins[2 * s], ins[2 * s + 1], outs[s], accs[s]
        rows = x_ref.shape[0]

        @pl.when(f == 0)
        def _():
            acc_ref[...] = jnp.zeros_like(acc_ref)

        for r in range(0, rows, FFN_TM):
            n = min(FFN_TM, rows - r)
            x = x_ref[r:r + n, :]
            hid = _silu(dot(x, wg_ref[...].astype(BF16))) * dot(x, wu_ref[...].astype(BF16))
            acc_ref[r:r + n, :] += dot(hid.astype(BF16), wd_ref[...].astype(BF16))

        @pl.when(f == pl.num_programs(1) - 1)
        def _():
            for r in range(0, rows, FFN_TM):
                n = min(FFN_TM, rows - r)
                y_ref[r:r + n, :] = (acc_ref[r:r + n, :] * a_ref[r:r + n, 0:1]).astype(BF16)


def _ffn_call(sets, wg, wu, wd, layer):
    _, n_exp, d, ff = wg.shape
    in_specs, args, out_specs, out_shape, scratch = [], [], [], [], []
    for xg, affc in sets:
        m = xg.shape[1]
        in_specs += [pl.BlockSpec((None, m, d), lambda e, f: (e, 0, 0)),
                     pl.BlockSpec((None, m, SMALL_W), lambda e, f: (e, 0, 0))]
        args += [xg, affc]
        out_specs.append(pl.BlockSpec((None, m, d), lambda e, f: (e, 0, 0)))
        out_shape.append(jax.ShapeDtypeStruct((n_exp, m, d), BF16))
        scratch.append(pltpu.VMEM((m, d), F32))
    in_specs += [pl.BlockSpec((None, None, d, FFN_TF), lambda e, f: (layer, e, 0, f)),
                 pl.BlockSpec((None, None, d, FFN_TF), lambda e, f: (layer, e, 0, f)),
                 pl.BlockSpec((None, None, FFN_TF, d), lambda e, f: (layer, e, f, 0))]
    return pl.pallas_call(
        functools.partial(_ffn_body, n_sets=len(sets)),
        grid=(n_exp, ff // FFN_TF),
        in_specs=in_specs, out_specs=out_specs, out_shape=out_shape, scratch_shapes=scratch,
        compiler_params=_cparams("parallel", "arbitrary"),
        name="expert_ffn",
    )(*args, wg, wu, wd)


def _scatter_body(post_ref, y_ref, h1_ref, mod_ref, fw_ref, o_ref, *, final):
    n_exp, cap, d = y_ref.shape
    pos = post_ref[...]
    slots = lax.broadcasted_iota(I32, (pos.shape[0], cap), 1).astype(F32)
    hit = jnp.concatenate([(pos[:, e:e+1] == slots).astype(BF16) for e in range(n_exp)], axis=1)
    upd = jnp.dot(hit, y_ref[...].reshape(n_exp * cap, d), preferred_element_type=F32)
    h2 = h1_ref[...] + mod_ref[5:6, :] * upd
    o_ref[...] = _rms(h2, fw_ref[...]) if final else h2


def _scatter_call(post, y, h1, mod_l, is_ctx, final_w):
    batch, t, d = h1.shape
    n_exp = y.shape[0]
    cap = y.shape[1] // batch
    tm = min(512, t)
    final = final_w is not None
    fw = final_w if final else jnp.ones((1, d), F32)
    return pl.pallas_call(
        functools.partial(_scatter_body, final=final),
        grid=(batch, t // tm),
        in_specs=[pl.BlockSpec((None, tm, SMALL_W), lambda b, i: (b, i, 0)),
                  pl.BlockSpec((n_exp, cap, d), lambda b, i: (0, b, 0)),
                  pl.BlockSpec((None, tm, d), lambda b, i: (b, i, 0)),
                  _mod_spec(is_ctx, batch),
                  pl.BlockSpec((1, d), lambda b, i: (0, 0))],
        out_specs=pl.BlockSpec((None, tm, d), lambda b, i: (b, i, 0)),
        out_shape=jax.ShapeDtypeStruct((batch, t, d), F32),
        compiler_params=_cparams("parallel", "parallel"),
        name="scatter_residual",
    )(post, y, h1, mod_l, fw)


def kernel(x, c, ctx, c_ctx, w_mod, b_mod, norm1_w, norm2_w, w_in, w_out, ret_decay_logit, ret_norm_w,
           gla_gate_w, gla_gate_b, gla_norm_w, ssd_conv_w, ssd_conv_b, ssd_dt_bias, ssd_a_log, ssd_d,
           ssd_norm_w, mlstm_gate_b, mlstm_norm_w, router_w, expert_w_gate, expert_w_up, expert_w_down,
           final_norm_w):
    batch, t_lat, d = x.shape
    t_ctx = ctx.shape[1]
    depth = w_mod.shape[0]
    assert batch + 1 <= ROW_PAD and d == D_MODEL
    for t in (t_lat, t_ctx):
        assert t % 128 == 0 and (t // CHUNK) % min(SCAN_STEPS, t // CHUNK) == 0
    assert t_lat % 256 == 0
    cs = jnp.concatenate([c, c_ctx[None], jnp.zeros((ROW_PAD - batch - 1, d), F32)], axis=0)
    mod = _mod_call(cs, w_mod, b_mod).reshape(depth, ROW_PAD, 6, d)
    cos, sin = _rope_tables(t_lat)
    cap_lat = EC_CAPACITY * t_lat // N_EXPERTS
    cap_ctx = EC_CAPACITY * t_ctx // N_EXPERTS
    h_lat, h_ctx = x, ctx
    for l in range(depth):
        ctx_out = l < depth - 1
        mod_l = mod[l]
        w_big = _arrange_w_in(w_in[l])
        ul = _in_call(h_lat, mod_l, norm1_w[l][None], w_big, False)
        uc = _in_call(h_ctx, mod_l, norm1_w[l][None], w_big, True)
        ys = [_ret_call(ul[0], uc[0], jax.nn.log_sigmoid(ret_decay_logit[l].astype(F32)), cos, sin,
                        ret_norm_w[l][None], ctx_out),
              _gla_call(ul[1], uc[1], ul[4], uc[4], gla_gate_w[l], gla_gate_b[l], gla_norm_w[l][None], ctx_out),
              _ssd_call(ul[2], uc[2], ul[4], uc[4], ssd_conv_w[l], ssd_conv_b[l], ssd_dt_bias[l], ssd_a_log[l],
                        ssd_d[l], ssd_norm_w[l][None], ctx_out),
              _mlstm_call(ul[3], uc[3], ul[4], uc[4], mlstm_gate_b[l], mlstm_norm_w[l][None], ctx_out)]
        wo = w_out[l].astype(BF16)
        rw3 = jnp.pad(jnp.concatenate(_split3(router_w[l]), axis=1), ((0, 0), (0, SMALL_W - 3 * N_EXPERTS)))
        n2w = norm2_w[l][None]
        h1_lat, hn_lat, lt_lat = _out_call([y[0] for y in ys], h_lat, mod_l, wo, n2w, rw3, False)
        xg_lat, affc_lat, post_lat = _route_call(lt_lat, hn_lat, cap_lat)
        sets = [(xg_lat, affc_lat)]
        if ctx_out:
            h1_ctx, hn_ctx, lt_ctx = _out_call([y[1] for y in ys], h_ctx, mod_l, wo, n2w, rw3, True)
            xg_ctx, affc_ctx, post_ctx = _route_call(lt_ctx, hn_ctx, cap_ctx)
            sets.append((xg_ctx, affc_ctx))
        y_exp = _ffn_call(sets, expert_w_gate, expert_w_up, expert_w_down, l)
        h_lat = _scatter_call(post_lat, y_exp[0], h1_lat, mod_l, False,
                              None if ctx_out else final_norm_w[None])
        if ctx_out:
            h_ctx = _scatter_call(post_ctx, y_exp[1], h1_ctx, mod_l, True, None)
    return h_lat
```
